```python
import math, functools
import jax, jax.numpy as jnp
from jax import lax
import numpy as np

D_MODEL = 1024
BATCH = 8
SEQ = 2048
DEPTH = 1
DEC_BATCH = 128
DEC_SEQ = 4
PAST_LEN = 8192
PAGE_SIZE = 128

N_META = 16
H_A = 4
DK_A = 128
DV_A = 128
CONV_W = 4
CHUNK = 64
H_B = 4
DK_B = 64
DV_B = 2 * DK_B
Q_BLOCK = 128
N_BUCKETS = 32
MAX_DIST = 128
N_EXPERTS = 32
TOP_K = 4
D_FF = D_MODEL
SWIGLU_LIMIT = 7.0
SWIGLU_ALPHA = 1.702
MOE_BLOCK = 128
EPS = 1e-6

QA = H_A * DK_A
VA = H_A * DV_A
QB = H_B * 2 * DK_B
VB = H_B * DV_B
CONV_DIM = 2 * QA + VA
PROJ_SIZES = (CONV_DIM, VA, H_A, H_A, QB, QB, VB, D_MODEL, D_MODEL)
IN_DIM = CONV_DIM + VA + 2 * H_A + 2 * QB + VB + 2 * D_MODEL

kernel_name = 'hybrid_gdn_diffattn_moe_step'


def rms_norm(x, g):
    xf = x.astype(jnp.float32)
    y = xf * lax.rsqrt(jnp.mean(xf * xf, axis=-1, keepdims=True) + EPS)
    return (y * g.astype(jnp.float32)).astype(x.dtype)


def l2_norm(x):
    return x * lax.rsqrt(jnp.sum(x * x, axis=-1, keepdims=True) + EPS)


def split_proj(p):
    offs, acc = [], 0
    for s in PROJ_SIZES[:-1]:
        acc += s
        offs.append(acc)
    return jnp.split(p, offs, axis=-1)


def rel_bias(dist, table):
    max_exact = N_BUCKETS // 2
    d = jnp.maximum(dist, 0)
    df = jnp.maximum(d, 1).astype(jnp.float32)
    large = max_exact + (jnp.log(df / max_exact) / math.log(MAX_DIST / max_exact)
                         * (N_BUCKETS - max_exact)).astype(jnp.int32)
    bucket = jnp.where(d < max_exact, d, jnp.minimum(large, N_BUCKETS - 1))
    return jnp.transpose(table.astype(jnp.float32)[bucket], (2, 0, 1))


def causal_conv_silu(xpad, w):
    t = xpad.shape[1] - (CONV_W - 1)
    out = xpad[:, 0:t] * w[0]
    for j in range(1, CONV_W):
        out = out + xpad[:, j:j + t] * w[j]
    return jax.nn.silu(out)


def gdn_chunks(s0, q, k, v, g, beta, csz):
    bsz, nh, t, _ = q.shape
    n = t // csz
    rs = lambda a: a.reshape(bsz, nh, n, csz, *a.shape[3:])
    q, k, v, g, beta = rs(q), rs(k), rs(v), rs(g), rs(beta)
    G = jnp.cumsum(g, axis=-1)
    incl = jnp.tril(jnp.ones((csz, csz), bool))
    strict = jnp.tril(jnp.ones((csz, csz), bool), -1)
    decay = jnp.exp(jnp.where(incl, G[..., :, None] - G[..., None, :], -jnp.inf))
    kb = k * beta[..., None]
    lower = jnp.where(strict, jnp.einsum('bhntd,bhnsd->bhnts', kb, k) * decay, 0.0)
    unit = lower + jnp.eye(csz, dtype=jnp.float32)
    rhs = jnp.concatenate([v * beta[..., None], kb * jnp.exp(G)[..., None]], axis=-1)
    sol = lax.linalg.triangular_solve(unit, rhs, left_side=True, lower=True, unit_diagonal=True)
    w_v, w_k = sol[..., :DV_A], sol[..., DV_A:]
    intra = jnp.einsum('bhntd,bhnsd->bhnts', q, k) * decay
    q_g = q * jnp.exp(G)[..., None]
    k_d = k * jnp.exp(G[..., -1:] - G)[..., None]
    g_last = jnp.exp(G[..., -1])

    def step(s, xs):
        wv, wk, att, qg, kd, gl = xs
        u = wv - jnp.einsum('bhcd,bhde->bhce', wk, s)
        o = jnp.einsum('bhcd,bhde->bhce', qg, s) + jnp.einsum('bhts,bhse->bhte', att, u)
        s = s * gl[..., None, None] + jnp.einsum('bhcd,bhce->bhde', kd, u)
        return s, o

    xs = tuple(jnp.moveaxis(a, 2, 0) for a in (w_v, w_k, intra, q_g, k_d, g_last))
    s, o = lax.scan(step, s0, xs)
    o = jnp.moveaxis(o, 0, 2).reshape(bsz, nh, t, DV_A)
    return o, s


def diff_attn_prompt(q, k, v, table):
    bsz, L = q.shape[:2]
    nb = -(-L // Q_BLOCK)
    qf = jnp.pad(q.astype(jnp.float32) * (DK_B ** -0.5), ((0, 0), (0, nb * Q_BLOCK - L), (0, 0), (0, 0), (0, 0)))
    qblocks = jnp.moveaxis(qf.reshape(bsz, nb, Q_BLOCK, H_B, 2, DK_B), 1, 0)
    kf = k.astype(jnp.float32)
    vf = v.astype(jnp.float32)
    k_pos = jnp.arange(L, dtype=jnp.int32)

    def one(args):
        qblk, start = args
        q_pos = start + jnp.arange(Q_BLOCK, dtype=jnp.int32)
        dist = q_pos[:, None] - k_pos[None, :]
        s = jnp.einsum('bqhcd,bkhcd->bhcqk', qblk, kf) + rel_bias(dist, table)[None, :, None]
        s = jnp.where(dist >= 0, s, -jnp.inf)
        p = jax.nn.softmax(s, axis=-1)
        return jnp.einsum('bhcqk,bkhe->bhcqe', p, vf)

    o = lax.map(one, (qblocks, jnp.arange(nb, dtype=jnp.int32) * Q_BLOCK))
    o = jnp.moveaxis(o, 0, 3).reshape(bsz, H_B, 2, nb * Q_BLOCK, DV_B)
    return o[:, :, :, :L]


def diff_attn_sample(q, k, v, cache_k, cache_v, page_table, table, layer):
    t = q.shape[1]
    q_pos = PAST_LEN + jnp.arange(t, dtype=jnp.int32)
    qs = q.astype(jnp.float32) * (DK_B ** -0.5)
    dist = q_pos[:, None] - q_pos[None, :]
    s = jnp.einsum('bqhcd,bkhcd->bhcqk', qs, k.astype(jnp.float32)) + rel_bias(dist, table)[None, :, None]
    s = jnp.where(dist >= 0, s, -jnp.inf)
    m = jnp.max(s, axis=-1)
    p = jnp.exp(s - m[..., None])
    l = jnp.sum(p, axis=-1)
    acc = jnp.einsum('bhcqk,bkhe->bhcqe', p, v.astype(jnp.float32))

    def step(carry, xs):
        m, l, acc = carry
        page_idx, phys = xs
        kp = cache_k[layer, phys].astype(jnp.float32)
        vp = cache_v[layer, phys].astype(jnp.float32)
        k_pos = page_idx * PAGE_SIZE + jnp.arange(PAGE_SIZE, dtype=jnp.int32)
        sp = jnp.einsum('bqhcd,bkhcd->bhcqk', qs, kp) + rel_bias(q_pos[:, None] - k_pos[None, :], table)[None, :, None]
        m_new = jnp.maximum(m, jnp.max(sp, axis=-1))
        corr = jnp.exp(m - m_new)
        pp = jnp.exp(sp - m_new[..., None])
        l = l * corr + jnp.sum(pp, axis=-1)
        acc = acc * corr[..., None] + jnp.einsum('bhcqk,bkhe->bhcqe', pp, vp)
        return (m_new, l, acc), None

    n_pages = page_table.shape[1]
    (m, l, acc), _ = lax.scan(step, (m, l, acc), (jnp.arange(n_pages, dtype=jnp.int32), page_table.T))
    return acc / l[..., None]


def moe(x2d, w_router, b_router, w_gu, b_gu, w_down, b_down):
    n = x2d.shape[0]
    logits = (x2d @ w_router).astype(jnp.float32) + b_router.astype(jnp.float32)
    top_v, top_i = lax.top_k(logits, TOP_K)
    gates = jax.nn.softmax(top_v, axis=-1)
    flat_e = top_i.reshape(-1).astype(jnp.int32)
    flat_g = gates.reshape(-1)
    n_assign = flat_e.shape[0]
    counts = jnp.bincount(flat_e, length=N_EXPERTS)
    padded = (counts + MOE_BLOCK - 1) // MOE_BLOCK * MOE_BLOCK
    pad_end = jnp.cumsum(padded)
    pad_start = pad_end - padded
    grp_start = jnp.cumsum(counts) - counts
    order = jnp.argsort(flat_e)
    e_sorted = flat_e[order]
    dest = pad_start[e_sorted] + jnp.arange(n_assign) - grp_start[e_sorted]
    n_blocks = -(-n_assign // MOE_BLOCK) + N_EXPERTS
    n_slots = n_blocks * MOE_BLOCK
    slot_tok = jnp.full((n_slots,), n, jnp.int32).at[dest].set((order // TOP_K).astype(jnp.int32))
    slot_gate = jnp.zeros((n_slots,), jnp.float32).at[dest].set(flat_g[order])
    block_expert = jnp.minimum(jnp.searchsorted(pad_end, jnp.arange(n_blocks) * MOE_BLOCK, side='right'),
                               N_EXPERTS - 1).astype(jnp.int32)
    x_pad = jnp.concatenate([x2d, jnp.zeros((1, x2d.shape[1]), x2d.dtype)], axis=0)
    xb = x_pad[slot_tok].reshape(n_blocks, MOE_BLOCK, x2d.shape[1])

    def expert_block(args):
        xblk, e = args
        hgu = xblk @ w_gu[e] + b_gu[e]
        gate = jnp.minimum(hgu[:, :D_FF], SWIGLU_LIMIT)
        up = jnp.clip(hgu[:, D_FF:], -SWIGLU_LIMIT, SWIGLU_LIMIT)
        act = (up + 1.0) * gate * jax.nn.sigmoid(SWIGLU_ALPHA * gate)
        return act @ w_down[e] + b_down[e]

    yb = lax.map(expert_block, (xb, block_expert)).reshape(n_slots, x2d.shape[1])
    y = jax.ops.segment_sum(yb.astype(jnp.float32) * slot_gate[:, None], slot_tok, num_segments=n + 1)[:n]
    return y.astype(x2d.dtype)


def trunk_layer(x, conv_buf, ssm0, plan, attend, lw, layer_idx):
    bsz, t, _ = x.shape
    dt = x.dtype
    f32 = jnp.float32
    h = rms_norm(x, lw['norm1'])
    conv_in, z, b_in, a_in, q_b, k_b, v_b, g_a, g_b = split_proj(h @ lw['w_in'])
    xpad = jnp.concatenate([conv_buf.astype(dt), conv_in], axis=1)
    new_conv = xpad[:, xpad.shape[1] - (CONV_W - 1):]
    c = causal_conv_silu(xpad, lw['conv_w']).astype(f32)
    qa = l2_norm(c[..., :QA].reshape(bsz, t, H_A, DK_A)) * (DK_A ** -0.5)
    ka = l2_norm(c[..., QA:2 * QA].reshape(bsz, t, H_A, DK_A))
    va = c[..., 2 * QA:].reshape(bsz, t, H_A, DV_A)
    beta = jax.nn.sigmoid(b_in.astype(f32))
    g = -jnp.exp(lw['a_log'].astype(f32)) * jax.nn.softplus(a_in.astype(f32) + lw['dt_bias'].astype(f32))
    qa, ka, va = jnp.swapaxes(qa, 1, 2), jnp.swapaxes(ka, 1, 2), jnp.swapaxes(va, 1, 2)
    beta, g = jnp.swapaxes(beta, 1, 2), jnp.swapaxes(g, 1, 2)
    s = ssm0.astype(f32)
    outs, off = [], 0
    for length, csz in plan:
        sl = slice(off, off + length)
        o, s = gdn_chunks(s, qa[:, :, sl], ka[:, :, sl], va[:, :, sl], g[:, :, sl], beta[:, :, sl], csz)
        outs.append(o)
        off += length
    oa = jnp.swapaxes(jnp.concatenate(outs, axis=2), 1, 2)
    oa = rms_norm(oa, lw['gdn_norm']) * jax.nn.silu(z.astype(f32).reshape(bsz, t, H_A, DV_A))
    y_a = oa.reshape(bsz, t, VA).astype(dt) @ lw['w_br_a']
    qd = rms_norm(q_b.reshape(bsz, t, H_B, 2, DK_B), lw['q_norm'])
    kd = rms_norm(k_b.reshape(bsz, t, H_B, 2, DK_B), lw['k_norm'])
    vd = v_b.reshape(bsz, t, H_B, DV_B)
    ob = attend(qd, kd, vd)
    lam_init = 0.8 - 0.6 * math.exp(-0.3 * layer_idx)
    lam = (jnp.exp(jnp.sum(lw['lam_q1'].astype(f32) * lw['lam_k1'].astype(f32)))
           - jnp.exp(jnp.sum(lw['lam_q2'].astype(f32) * lw['lam_k2'].astype(f32))) + lam_init)
    ob = jnp.swapaxes(ob[:, :, 0] - lam * ob[:, :, 1], 1, 2)
    ob = rms_norm(ob, lw['sub_norm']) * (1.0 - lam_init)
    y_b = ob.reshape(bsz, t, VB).astype(dt) @ lw['w_br_b']
    merged = jax.nn.sigmoid(g_a) * y_a + jax.nn.sigmoid(g_b) * y_b
    x = x + merged @ lw['w_out']
    hm = rms_norm(x, lw['norm2']).reshape(bsz * t, D_MODEL)
    x = x + moe(hm, lw['w_router'], lw['b_router'], lw['w_gu'], lw['b_gu'],
                lw['w_down'], lw['b_down']).reshape(bsz, t, D_MODEL)
    return x, kd, vd, s.astype(dt), new_conv


def setup_inputs(seed: int = 0) -> dict:
    key = jax.random.key(seed)
    ks = jax.random.split(key, 40)
    f32 = jnp.float32
    n_pages = PAST_LEN // PAGE_SIZE
    n_pool = (DEC_BATCH * n_pages * 5) // 4
    nrm = lambda kk, shape, sc: jax.random.normal(kk, shape, f32) * sc
    dt = jnp.exp(jax.random.uniform(ks[13], (DEPTH, H_A), f32, math.log(1e-3), math.log(1e-1)))
    return {
        'x_prompt': nrm(ks[0], (BATCH, SEQ, D_MODEL), 1.0),
        'x_sample': nrm(ks[1], (DEC_BATCH, DEC_SEQ, D_MODEL), 1.0),
        'cache_k': nrm(ks[2], (DEPTH, n_pool, PAGE_SIZE, H_B, 2, DK_B), 1.0),
        'cache_v': nrm(ks[3], (DEPTH, n_pool, PAGE_SIZE, H_B, DV_B), 1.0),
        'state_ssm': nrm(ks[4], (DEPTH, DEC_BATCH, H_A, DK_A, DV_A), 0.05),
        'state_conv': nrm(ks[5], (DEPTH, DEC_BATCH, CONV_W - 1, CONV_DIM), 1.0),
        'page_table': jax.random.permutation(ks[6], n_pool)[:DEC_BATCH * n_pages]
                      .reshape(DEC_BATCH, n_pages).astype(jnp.int32),
        'meta_tokens': nrm(ks[7], (N_META, D_MODEL), 1.0),
        'rel_bias_table': nrm(ks[8], (N_BUCKETS, H_B), 0.1),
        'norm1': 1.0 + nrm(ks[9], (DEPTH, D_MODEL), 0.02),
        'w_in': nrm(ks[10], (DEPTH, D_MODEL, IN_DIM), D_MODEL ** -0.5),
        'conv_w': nrm(ks[11], (DEPTH, CONV_W, CONV_DIM), 0.5),
        'a_log': jnp.log(jax.random.uniform(ks[12], (DEPTH, H_A), f32, 1.0, 16.0)),
        'dt_bias': dt + jnp.log(-jnp.expm1(-dt)),
        'gdn_norm': 1.0 + nrm(ks[14], (DEPTH, DV_A), 0.02),
        'q_norm': 1.0 + nrm(ks[15], (DEPTH, DK_B), 0.02),
        'k_norm': 1.0 + nrm(ks[16], (DEPTH, DK_B), 0.02),
        'lam_q1': nrm(ks[17], (DEPTH, DK_B), 0.1),
        'lam_k1': nrm(ks[18], (DEPTH, DK_B), 0.1),
        'lam_q2': nrm(ks[19], (DEPTH, DK_B), 0.1),
        'lam_k2': nrm(ks[20], (DEPTH, DK_B), 0.1),
        'sub_norm': 1.0 + nrm(ks[21], (DEPTH, DV_B), 0.02),
        'w_br_a': nrm(ks[22], (DEPTH, VA, D_MODEL), VA ** -0.5),
        'w_br_b': nrm(ks[23], (DEPTH, VB, D_MODEL), VB ** -0.5),
        'w_out': nrm(ks[24], (DEPTH, D_MODEL, D_MODEL), D_MODEL ** -0.5),
        'norm2': 1.0 + nrm(ks[25], (DEPTH, D_MODEL), 0.02),
        'w_router': nrm(ks[26], (DEPTH, D_MODEL, N_EXPERTS), D_MODEL ** -0.5),
        'b_router': nrm(ks[27], (DEPTH, N_EXPERTS), 0.01),
        'w_gu': nrm(ks[28], (DEPTH, N_EXPERTS, D_MODEL, 2 * D_FF), D_MODEL ** -0.5),
        'b_gu': nrm(ks[29], (DEPTH, N_EXPERTS, 2 * D_FF), 0.01),
        'w_down': nrm(ks[30], (DEPTH, N_EXPERTS, D_FF, D_MODEL), D_FF ** -0.5),
        'b_down': nrm(ks[31], (DEPTH, N_EXPERTS, D_MODEL), 0.01),
    }


def reference(x_prompt, x_sample, cache_k, cache_v, state_ssm, state_conv, page_table,
              meta_tokens, rel_bias_table, norm1, w_in, conv_w, a_log, dt_bias, gdn_norm,
              q_norm, k_norm, lam_q1, lam_k1, lam_q2, lam_k2, sub_norm, w_br_a, w_br_b, w_out,
              norm2, w_router, b_router, w_gu, b_gu, w_down, b_down):
    bsz, seq, dm = x_prompt.shape
    dec_seq = x_sample.shape[1]
    xp = jnp.concatenate([jnp.broadcast_to(meta_tokens.astype(x_prompt.dtype)[None], (bsz, N_META, dm)),
                          x_prompt], axis=1)
    xs = x_sample
    prompt_plan = ((N_META, N_META), (seq, CHUNK))
    sample_plan = ((dec_seq, dec_seq),)
    kp_l, vp_l, sp_l, cp_l, ks_l, vs_l, ss_l, cs_l = [], [], [], [], [], [], [], []
    for l in range(DEPTH):
        lw = dict(norm1=norm1[l], w_in=w_in[l], conv_w=conv_w[l], a_log=a_log[l], dt_bias=dt_bias[l],
                  gdn_norm=gdn_norm[l], q_norm=q_norm[l], k_norm=k_norm[l], lam_q1=lam_q1[l],
                  lam_k1=lam_k1[l], lam_q2=lam_q2[l], lam_k2=lam_k2[l], sub_norm=sub_norm[l],
                  w_br_a=w_br_a[l], w_br_b=w_br_b[l], w_out=w_out[l], norm2=norm2[l],
                  w_router=w_router[l], b_router=b_router[l], w_gu=w_gu[l], b_gu=b_gu[l],
                  w_down=w_down[l], b_down=b_down[l])
        attend_p = functools.partial(diff_attn_prompt, table=rel_bias_table)
        zero_conv = jnp.zeros((bsz, CONV_W - 1, CONV_DIM), xp.dtype)
        zero_ssm = jnp.zeros((bsz, H_A, DK_A, DV_A), jnp.float32)
        xp, kp, vp, sp, cp = trunk_layer(xp, zero_conv, zero_ssm, prompt_plan, attend_p, lw, l)
        attend_s = functools.partial(diff_attn_sample, cache_k=cache_k, cache_v=cache_v,
                                     page_table=page_table, table=rel_bias_table, layer=l)
        xs, ks, vs, ss, cs = trunk_layer(xs, state_conv[l], state_ssm[l], sample_plan, attend_s, lw, l)
        kp_l.append(kp); vp_l.append(vp); sp_l.append(sp); cp_l.append(cp)
        ks_l.append(ks); vs_l.append(vs); ss_l.append(ss); cs_l.append(cs)
    y_prompt = xp[:, N_META:]
    y_sample = xs
    return (y_prompt, y_sample, jnp.stack(kp_l), jnp.stack(vp_l), jnp.stack(sp_l), jnp.stack(cp_l),
            jnp.stack(ks_l), jnp.stack(vs_l), jnp.stack(ss_l), jnp.stack(cs_l))
```

```python
import functools
import math

import numpy as np
import jax
import jax.numpy as jnp
from jax import lax
from jax.experimental import pallas as pl
from jax.experimental.pallas import tpu as pltpu

F32 = jnp.float32
BF16 = jnp.bfloat16
HI = lax.Precision.HIGHEST

N_META = 16
H_A, DK_A, DV_A = 4, 128, 128
CONV_W = 4
H_B, DK_B = 4, 64
DV_B = 2 * DK_B
N_BUCKETS, MAX_DIST = 32, 128
N_EXPERTS, TOP_K = 32, 4
SWIGLU_LIMIT, SWIGLU_ALPHA = 7.0, 1.702
EPS = 1e-6
PAGE = 128
QA = H_A * DK_A
VA = H_A * DV_A
QB = H_B * 2 * DK_B
VB = H_B * DV_B
CONV_DIM = 2 * QA + VA

LANES = 128
SUBLANES = 8
VMEM_LIMIT = 56 * 1024 * 1024

GDN_CHUNK = 64
ATT_BLK = 128
MOE_BLK = 256
ROW_TILE = 128
NEG = -1e30


def _dot(a, b, prec=None):
    return jnp.dot(a, b, preferred_element_type=F32, precision=prec)


def _dot_nt(a, b, prec=None):
    return lax.dot_general(a, b, (((1,), (1,)), ((), ())), preferred_element_type=F32, precision=prec)


def _dot_tn(a, b, prec=None):
    return lax.dot_general(a, b, (((0,), (0,)), ((), ())), preferred_element_type=F32, precision=prec)


def _sigmoid(x):
    return 1.0 / (1.0 + jnp.exp(-x))


def _cparams(sem):
    return pltpu.CompilerParams(dimension_semantics=sem, vmem_limit_bytes=VMEM_LIMIT)


def _bucket_np(d):
    d = np.maximum(d, 0)
    df = np.maximum(d, 1).astype(np.float32)
    max_exact = N_BUCKETS // 2
    large = max_exact + (np.log(df / np.float32(max_exact)) / np.float32(math.log(MAX_DIST / max_exact))
                         * np.float32(N_BUCKETS - max_exact)).astype(np.int32)
    return np.where(d < max_exact, d, np.minimum(large, N_BUCKETS - 1)).astype(np.int32)


def _bias_kernel(tab_ref, bk_ref, o_ref):
    for t in range(bk_ref.shape[0]):
        bk = bk_ref[t]
        for h in range(H_B):
            acc = jnp.zeros(bk.shape, F32)
            for b in range(N_BUCKETS):
                acc = jnp.where(bk == b, tab_ref[b, h], acc)
            o_ref[h, t] = acc


def _bias_tiles(table, buckets):
    t, r, _ = buckets.shape
    return pl.pallas_call(
        _bias_kernel,
        out_shape=jax.ShapeDtypeStruct((H_B, t, r, LANES), F32),
        in_specs=[pl.BlockSpec(memory_space=pltpu.SMEM), pl.BlockSpec(memory_space=pltpu.VMEM)],
        out_specs=pl.BlockSpec(memory_space=pltpu.VMEM),
        name="rel_bias_tiles",
    )(table, jnp.asarray(buckets))


def _inproj_kernel(x_ref, n1_ref, w_ref, wba_ref, gm_ref, qn_ref, kn_ref, alog_ref, dtb_ref,
                   conv_ref, z_ref, bg_ref, qd_ref, kd_ref, vd_ref, sga_ref, sgb_ref):
    x = x_ref[...]
    ms = jnp.mean(x * x, axis=-1, keepdims=True)
    h = (x * lax.rsqrt(ms + EPS) * n1_ref[...]).astype(BF16)
    o = 0
    conv_ref[...] = _dot(h, w_ref[:, o:o + CONV_DIM]); o += CONV_DIM
    z_ref[...] = _dot(h, w_ref[:, o:o + VA]).astype(z_ref.dtype); o += VA

    def group_norm(y, g):
        sq = y * y
        hi = sq.astype(BF16)
        lo = (sq - hi.astype(F32)).astype(BF16)
        msq = _dot(hi, gm_ref[...]) + _dot(lo, gm_ref[...])
        return y * lax.rsqrt(msq + EPS) * g

    qb = _dot(h, w_ref[:, o:o + QB]); o += QB
    qd_ref[...] = (group_norm(qb, qn_ref[...]) * (DK_B ** -0.5)).astype(qd_ref.dtype)
    kb = _dot(h, w_ref[:, o:o + QB]); o += QB
    kd_ref[...] = group_norm(kb, kn_ref[...])
    vd_ref[...] = _dot(h, w_ref[:, o:o + VB]); o += VB
    d_model = x.shape[1]
    sga_ref[...] = _sigmoid(_dot(h, w_ref[:, o:o + d_model])).astype(sga_ref.dtype); o += d_model
    sgb_ref[...] = _sigmoid(_dot(h, w_ref[:, o:o + d_model])).astype(sgb_ref.dtype)
    t = _dot(h, wba_ref[...])
    lane = lax.broadcasted_iota(jnp.int32, t.shape, 1)
    ta = t + dtb_ref[...]
    sp = jnp.maximum(ta, 0.0) + jnp.log(1.0 + jnp.exp(-jnp.abs(ta)))
    bg_ref[...] = jnp.where(lane < H_A, _sigmoid(t), -jnp.exp(alog_ref[...]) * sp)


def _inproj(x2d, tm, n1, w_main, w_ba, gmat, qn_t, kn_t, alog_p, dtb_p):
    n, d = x2d.shape
    assert n % tm == 0
    row = lambda c: pl.BlockSpec((tm, c), lambda i: (i, 0))
    full = lambda a: pl.BlockSpec(a.shape, lambda i: (0,) * a.ndim)
    outs = [(CONV_DIM, F32), (VA, BF16), (LANES, F32), (QB, BF16), (QB, F32), (VB, F32), (d, BF16), (d, BF16)]
    return pl.pallas_call(
        _inproj_kernel,
        grid=(n // tm,),
        in_specs=[row(d)] + [full(a) for a in (n1, w_main, w_ba, gmat, qn_t, kn_t, alog_p, dtb_p)],
        out_specs=[row(c) for c, _ in outs],
        out_shape=[jax.ShapeDtypeStruct((n, c), dt) for c, dt in outs],
        compiler_params=_cparams(("parallel",)),
        name="in_proj",
    )(x2d, n1, w_main, w_ba, gmat, qn_t, kn_t, alog_p, dtb_p)


def _gdn_kernel(fv, qc_ref, kc_ref, vc_ref, cwq_ref, cwk_ref, cwv_ref, bg_ref, z_ref, s0_ref, gn_ref,
                o_ref, st_ref, xq, xk, xv):
    c = GDN_CHUNK
    r = qc_ref.shape[1]
    h = pl.program_id(1)
    for src, dst in ((qc_ref, xq), (kc_ref, xk), (vc_ref, xv)):
        dst[0:SUBLANES, :] = jnp.zeros((SUBLANES, LANES), F32)
        dst[SUBLANES:SUBLANES + r, :] = src[0]

    ri = lax.broadcasted_iota(jnp.int32, (c, c), 0)
    ci = lax.broadcasted_iota(jnp.int32, (c, c), 1)
    incl = ri >= ci
    strict = ri > ci
    eye = (ri == ci).astype(F32)
    lane = lax.broadcasted_iota(jnp.int32, (c, LANES), 1)
    sub_t = lax.broadcasted_iota(jnp.int32, (LANES, c), 0)
    rowid = lax.broadcasted_iota(jnp.int32, (c, 1), 0)
    colid = lax.broadcasted_iota(jnp.int32, (1, c), 1)

    def conv(xs, cw_ref, r0):
        w = xs[pl.ds(r0, c + SUBLANES), :]
        acc = w[SUBLANES - 3:SUBLANES - 3 + c] * cw_ref[0:1, :]
        for j in range(1, CONV_W):
            acc = acc + w[SUBLANES - 3 + j:SUBLANES - 3 + j + c] * cw_ref[j:j + 1, :]
        return acc * _sigmoid(acc)

    def chunk(j, s):
        r0 = pl.multiple_of(j * c, c)
        vcol = ((r0 + rowid) >= fv).astype(F32)
        vrow = ((r0 + colid) >= fv).astype(F32)
        qv = conv(xq, cwq_ref, r0)
        kv = conv(xk, cwk_ref, r0)
        v = conv(xv, cwv_ref, r0) * vcol
        q = qv * lax.rsqrt(jnp.sum(qv * qv, axis=-1, keepdims=True) + EPS) * (DK_A ** -0.5) * vcol
        k = kv * lax.rsqrt(jnp.sum(kv * kv, axis=-1, keepdims=True) + EPS) * vcol
        bgc = bg_ref[0, pl.ds(r0, c), :]
        beta = jnp.sum(jnp.where(lane == h, bgc, 0.0), axis=-1, keepdims=True) * vcol
        g_col = jnp.sum(jnp.where(lane == H_A + h, bgc, 0.0), axis=-1, keepdims=True) * vcol
        g_row = jnp.sum(jnp.where(sub_t == H_A + h, bgc.T, 0.0), axis=0, keepdims=True) * vrow
        gc = jnp.sum(jnp.where(incl, g_row, 0.0), axis=-1, keepdims=True)
        gr = jnp.sum(jnp.where(ri <= ci, g_col, 0.0), axis=0, keepdims=True)
        decay = jnp.where(incl, jnp.exp(jnp.where(incl, gc - gr, 0.0)), 0.0)
        kb = k * beta
        low = jnp.where(strict, _dot_nt(kb, k, HI) * decay, 0.0)
        inv = eye - low
        pw = low
        for _ in range(int(math.log2(c)) - 1):
            pw = _dot(pw, pw, HI)
            inv = inv + _dot(inv, pw, HI)
        eg = jnp.exp(gc)
        rhs = jnp.concatenate([v * beta, kb * eg], axis=-1)
        sol = _dot(inv, rhs, HI)
        w_v, w_k = sol[:, :DV_A], sol[:, DV_A:]
        intra = _dot_nt(q, k, HI) * decay
        g_last = jnp.sum(jnp.where(rowid == c - 1, gc, 0.0), axis=0, keepdims=True)
        k_d = k * jnp.exp(g_last - gc)
        u = w_v - _dot(w_k, s, HI)
        o = _dot(q * eg, s, HI) + _dot(intra, u, HI)
        s = s * jnp.exp(g_last) + _dot_tn(k_d, u, HI)
        on = o * lax.rsqrt(jnp.mean(o * o, axis=-1, keepdims=True) + EPS) * gn_ref[...]
        zc = z_ref[0, pl.ds(r0, c), :].astype(F32)
        o_ref[0, pl.ds(r0, c), :] = (on * (zc * _sigmoid(zc))).astype(o_ref.dtype)
        return s

    j0 = fv // c
    if j0 > 0:
        o_ref[0, 0:j0 * c, :] = jnp.zeros((j0 * c, LANES), o_ref.dtype)
    st_ref[0, 0] = lax.fori_loop(j0, r // c, chunk, s0_ref[0, 0])


def _gdn(conv3, bg3, z3, s0, conv_w, gn, fv):
    bsz, r, _ = conv3.shape
    assert r % GDN_CHUNK == 0
    blk = lambda off: pl.BlockSpec((1, r, LANES), lambda b, h: (b, 0, off + h))
    cw = lambda off: pl.BlockSpec((CONV_W, LANES), lambda b, h: (0, off + h))
    st = pl.BlockSpec((1, 1, DK_A, DV_A), lambda b, h: (b, h, 0, 0))
    return pl.pallas_call(
        functools.partial(_gdn_kernel, fv),
        grid=(bsz, H_A),
        in_specs=[blk(0), blk(H_A), blk(2 * H_A), cw(0), cw(H_A), cw(2 * H_A),
                  pl.BlockSpec((1, r, LANES), lambda b, h: (b, 0, 0)), blk(0), st,
                  pl.BlockSpec((1, LANES), lambda b, h: (0, 0))],
        out_specs=[blk(0), st],
        out_shape=[jax.ShapeDtypeStruct((bsz, r, VA), F32), jax.ShapeDtypeStruct((bsz, H_A, DK_A, DV_A), F32)],
        scratch_shapes=[pltpu.VMEM((r + SUBLANES, LANES), F32)] * 3,
        compiler_params=_cparams(("parallel", "parallel")),
        name="gdn",
    )(conv3, conv3, conv3, conv_w, conv_w, conv_w, bg3, z3, s0, gn)


def _lambda(lam_ref, lam_init):
    l1 = jnp.sum(lam_ref[0:1, :] * lam_ref[1:2, :], axis=-1, keepdims=True)
    l2 = jnp.sum(lam_ref[2:3, :] * lam_ref[3:4, :], axis=-1, keepdims=True)
    return jnp.exp(l1) - jnp.exp(l2) + lam_init


def _sub_norm(o0, o1, lam, sn, lam_init):
    ob = o0 - lam * o1
    return ob * lax.rsqrt(jnp.mean(ob * ob, axis=-1, keepdims=True) + EPS) * sn * (1.0 - lam_init)


def _attn_prompt_kernel(fv, lam_init, q_ref, k_ref, v_ref, bt_ref, far_ref, lam_ref, sn_ref, o_ref):
    blk = ATT_BLK
    qi = pl.program_id(2)
    q = q_ref[0]
    lane = lax.broadcasted_iota(jnp.int32, q.shape, 1)
    qs = (jnp.where(lane < DK_B, q, jnp.zeros_like(q)), jnp.where(lane >= DK_B, q, jnp.zeros_like(q)))
    rows = lax.broadcasted_iota(jnp.int32, (blk, blk), 0)
    cols = lax.broadcasted_iota(jnp.int32, (blk, blk), 1)
    far = far_ref[0]
    t0 = bt_ref[0, 0]
    t1 = bt_ref[0, 1]

    def body(kj, carry):
        k0 = pl.multiple_of(kj * blk, blk)
        ks = k_ref[0, pl.ds(k0, blk), :].astype(BF16)
        vs = v_ref[0, pl.ds(k0, blk), :].astype(BF16)
        rel = qi - kj
        bias = jnp.where(rel == 0, t0, jnp.where(rel == 1, t1, far))
        kpos = k0 + cols
        ok = (kpos <= qi * blk + rows) & (kpos >= fv)
        out = []
        for c in range(2):
            m, l, acc = carry[c]
            s = jnp.where(ok, _dot_nt(qs[c], ks) + bias, NEG)
            m_new = jnp.maximum(m, jnp.max(s, axis=-1, keepdims=True))
            alpha = jnp.exp(m - m_new)
            p = jnp.exp(s - m_new)
            l = alpha * l + jnp.sum(p, axis=-1, keepdims=True)
            acc = alpha * acc + _dot(p.astype(BF16), vs)
            out.append((m_new, l, acc))
        return tuple(out)

    init = (jnp.full((blk, 1), NEG, F32), jnp.zeros((blk, 1), F32), jnp.zeros((blk, DV_B), F32))
    (_, l0, a0), (_, l1, a1) = lax.fori_loop(0, qi + 1, body, (init, init))
    o_ref[0] = _sub_norm(a0 / l0, a1 / l1, _lambda(lam_ref, lam_init), sn_ref[...], lam_init).astype(o_ref.dtype)


def _attn_prompt(qd3, kd3, vd3, btiles, far, lam4, sn, fv, lam_init):
    bsz, lp, _ = qd3.shape
    nq = lp // ATT_BLK
    kv = pl.BlockSpec((1, lp, LANES), lambda b, h, i: (b, 0, h))
    qo = pl.BlockSpec((1, ATT_BLK, LANES), lambda b, h, i: (b, i, h))
    return pl.pallas_call(
        functools.partial(_attn_prompt_kernel, fv, lam_init),
        grid=(bsz, H_B, nq),
        in_specs=[qo, kv, kv,
                  pl.BlockSpec((1, 2, ATT_BLK, LANES), lambda b, h, i: (h, 0, 0, 0)),
                  pl.BlockSpec((1, 1, LANES), lambda b, h, i: (h, 0, 0)),
                  pl.BlockSpec(lam4.shape, lambda b, h, i: (0, 0)),
                  pl.BlockSpec(sn.shape, lambda b, h, i: (0, 0))],
        out_specs=qo,
        out_shape=jax.ShapeDtypeStruct((bsz, lp, VB), F32),
        compiler_params=_cparams(("parallel", "parallel", "parallel")),
        name="diff_attn_prompt",
    )(qd3, kd3, vd3, btiles, far, lam4, sn)


def _attn_sample_kernel(t_new, lam_init, pt_ref, q_ref, kn_ref, vn_ref, ck_ref, cv_ref, last_ref, self_ref,
                        far_ref, lam_ref, sn_ref, o_ref, qm, kself, vself, m_s, l_s, acc_s):
    p = pl.program_id(1)
    npg = pl.num_programs(1)
    nrow = H_B * 2 * SUBLANES

    def update(s, vs):
        m_old = m_s[...]
        m_new = jnp.maximum(m_old, jnp.max(s, axis=-1, keepdims=True))
        alpha = jnp.exp(m_old - m_new)
        pr = jnp.exp(s - m_new)
        l_s[...] = alpha * l_s[...] + jnp.sum(pr, axis=-1, keepdims=True)
        acc_s[...] = alpha * acc_s[...] + _dot(pr.astype(BF16), vs)
        m_s[...] = m_new

    @pl.when(p == 0)
    def _():
        q8 = q_ref[0]
        lane = lax.broadcasted_iota(jnp.int32, q8.shape, 1)
        for g in range(2 * H_B):
            qm[g * SUBLANES:(g + 1) * SUBLANES, :] = jnp.where(lane // DK_B == g, q8, 0.0)
        kself[...] = jnp.zeros(kself.shape, F32)
        vself[...] = jnp.zeros(vself.shape, F32)
        kself[0:SUBLANES, :] = kn_ref[0]
        vself[0:SUBLANES, :] = vn_ref[0]
        m_s[...] = jnp.full(m_s.shape, NEG, F32)
        l_s[...] = jnp.zeros(l_s.shape, F32)
        acc_s[...] = jnp.zeros(acc_s.shape, F32)
        rows = lax.broadcasted_iota(jnp.int32, (nrow, PAGE), 0) % SUBLANES
        cols = lax.broadcasted_iota(jnp.int32, (nrow, PAGE), 1)
        ok = (cols <= rows) & (cols < t_new)
        s = jnp.where(ok, _dot_nt(qm[...].astype(BF16), kself[...].astype(BF16)) + self_ref[...], NEG)
        update(s, vself[...].astype(BF16))

    s = _dot_nt(qm[...].astype(BF16), ck_ref[0].astype(BF16))
    s = s + jnp.where(p == npg - 1, last_ref[...], far_ref[...])
    update(s, cv_ref[0].astype(BF16))

    @pl.when(p == npg - 1)
    def _():
        o = acc_s[...] / l_s[...]
        lam = _lambda(lam_ref, lam_init)
        for h in range(H_B):
            r0 = h * 2 * SUBLANES
            o0 = o[r0:r0 + SUBLANES, h * DV_B:(h + 1) * DV_B]
            o1 = o[r0 + SUBLANES:r0 + 2 * SUBLANES, h * DV_B:(h + 1) * DV_B]
            o_ref[0, :, h * DV_B:(h + 1) * DV_B] = _sub_norm(o0, o1, lam, sn_ref[...], lam_init)


def _attn_sample(page_table, q8, kn8, vn8, ck, cv, last_t, self_t, far_col, lam4, sn, t_new, lam_init):
    bsz, npg = page_table.shape
    nrow = H_B * 2 * SUBLANES
    tok = pl.BlockSpec((1, SUBLANES, QB), lambda b, p, pt: (b, 0, 0))
    page = pl.BlockSpec((1, PAGE, QB), lambda b, p, pt: (pt[b, p], 0, 0))
    full = lambda a: pl.BlockSpec(a.shape, lambda b, p, pt: (0,) * a.ndim)
    return pl.pallas_call(
        functools.partial(_attn_sample_kernel, t_new, lam_init),
        grid_spec=pltpu.PrefetchScalarGridSpec(
            num_scalar_prefetch=1,
            grid=(bsz, npg),
            in_specs=[tok, tok, tok, page, page, full(last_t), full(self_t), full(far_col), full(lam4), full(sn)],
            out_specs=tok,
            scratch_shapes=[pltpu.VMEM((nrow, QB), F32), pltpu.VMEM((PAGE, QB), F32), pltpu.VMEM((PAGE, VB), F32),
                            pltpu.VMEM((nrow, 1), F32), pltpu.VMEM((nrow, 1), F32), pltpu.VMEM((nrow, VB), F32)],
        ),
        out_shape=jax.ShapeDtypeStruct((bsz, SUBLANES, VB), F32),
        compiler_params=_cparams(("parallel", "arbitrary")),
        name="diff_attn_sample",
    )(page_table, q8, kn8, vn8, ck, cv, last_t, self_t, far_col, lam4, sn)


def _post_kernel(x_ref, oa_ref, ob_ref, sga_ref, sgb_ref, wa_ref, wb_ref, wo_ref, n2_ref, wr_ref, br_ref,
                 x1_ref, hm_ref, ti_ref, tg_ref):
    ya = _dot(oa_ref[...].astype(BF16), wa_ref[...])
    yb = _dot(ob_ref[...].astype(BF16), wb_ref[...])
    merged = sga_ref[...].astype(F32) * ya + sgb_ref[...].astype(F32) * yb
    x1 = x_ref[...] + _dot(merged.astype(BF16), wo_ref[...])
    x1_ref[...] = x1
    hm = x1 * lax.rsqrt(jnp.mean(x1 * x1, axis=-1, keepdims=True) + EPS) * n2_ref[...]
    hm_ref[...] = hm
    logits = _dot(hm, wr_ref[...], HI) + br_ref[...]
    lane = lax.broadcasted_iota(jnp.int32, logits.shape, 1)
    lane_f = lane.astype(F32)
    work = jnp.where(lane < N_EXPERTS, logits, -jnp.inf)
    ti = jnp.zeros(logits.shape, F32)
    tg = jnp.zeros(logits.shape, F32)
    v0 = None
    den = None
    for k in range(TOP_K):
        vk = jnp.max(work, axis=-1, keepdims=True)
        ik = jnp.min(jnp.where(work == vk, lane_f, float(LANES)), axis=-1, keepdims=True)
        work = jnp.where(lane_f == ik, -jnp.inf, work)
        if k == 0:
            v0 = vk
        ek = jnp.exp(vk - v0)
        den = ek if k == 0 else den + ek
        ti = jnp.where(lane == k, ik, ti)
        tg = jnp.where(lane == k, ek, tg)
    ti_ref[...] = ti.astype(jnp.int32)
    tg_ref[...] = tg / den


def _post(x2d, oa, ob, sga, sgb, wa, wb, wo, n2, wr, br, n_out, x_map, in_map):
    d = x2d.shape[1]
    tm = ROW_TILE
    full = lambda a: pl.BlockSpec(a.shape, lambda i: (0,) * a.ndim)
    rin = lambda c: pl.BlockSpec((tm, c), lambda i: (in_map(i), 0))
    rout = lambda c: pl.BlockSpec((tm, c), lambda i: (i, 0))
    return pl.pallas_call(
        _post_kernel,
        grid=(n_out // tm,),
        in_specs=[pl.BlockSpec((tm, d), lambda i: (x_map(i), 0)), rin(VA), rin(VB), rin(d), rin(d)]
                 + [full(a) for a in (wa, wb, wo, n2, wr, br)],
        out_specs=[rout(d), rout(d), rout(LANES), rout(LANES)],
        out_shape=[jax.ShapeDtypeStruct((n_out, d), F32), jax.ShapeDtypeStruct((n_out, d), F32),
                   jax.ShapeDtypeStruct((n_out, LANES), jnp.int32), jax.ShapeDtypeStruct((n_out, LANES), F32)],
        compiler_params=_cparams(("parallel",)),
        name="merge_outproj_router",
    )(x2d, oa, ob, sga, sgb, wa, wb, wo, n2, wr, br)


def _rank_kernel(ti_ref, rank_ref, cnt_ref, base):
    i = pl.program_id(0)
    tm = ti_ref.shape[0]

    @pl.when(i == 0)
    def _():
        base[...] = jnp.zeros(base.shape, F32)

    ti = ti_ref[...].astype(F32)
    lane = lax.broadcasted_iota(jnp.int32, ti.shape, 1)
    lane_f = lane.astype(F32)
    sel = [jnp.sum(jnp.where(lane == k, ti, 0.0), axis=-1, keepdims=True) for k in range(TOP_K)]
    onehot = jnp.zeros(ti.shape, F32)
    for k in range(TOP_K):
        onehot = onehot + (lane_f == sel[k]).astype(F32)
    ri = lax.broadcasted_iota(jnp.int32, (tm, tm), 0)
    ci = lax.broadcasted_iota(jnp.int32, (tm, tm), 1)
    before = _dot((ri > ci).astype(BF16), onehot.astype(BF16)) + base[...]
    rank = jnp.zeros(ti.shape, jnp.int32)
    for k in range(TOP_K):
        rk = jnp.sum(jnp.where(lane_f == sel[k], before, 0.0), axis=-1, keepdims=True)
        rank = jnp.where(lane == k, rk.astype(jnp.int32), rank)
    rank_ref[...] = rank
    base[...] = base[...] + jnp.sum(onehot, axis=0, keepdims=True)
    cnt_ref[...] = base[...]


def _rank(ti):
    n = ti.shape[0]
    tm = ROW_TILE
    return pl.pallas_call(
        _rank_kernel,
        grid=(n // tm,),
        in_specs=[pl.BlockSpec((tm, LANES), lambda i: (i, 0))],
        out_specs=[pl.BlockSpec((tm, LANES), lambda i: (i, 0)), pl.BlockSpec((1, LANES), lambda i: (0, 0))],
        out_shape=[jax.ShapeDtypeStruct((n, LANES), jnp.int32), jax.ShapeDtypeStruct((1, LANES), F32)],
        scratch_shapes=[pltpu.VMEM((1, LANES), F32)],
        compiler_params=_cparams(("arbitrary",)),
        name="moe_rank",
    )(ti)


def _dispatch_kernel(dest_ref, hm_ref, xs_in, xs_out, sem):
    del xs_in
    tm = hm_ref.shape[0]

    def row_copy(r, d):
        return pltpu.make_async_copy(hm_ref.at[pl.ds(r, 1)], xs_out.at[pl.ds(d, 1)], sem)

    def issue(r, c):
        for k in range(TOP_K):
            row_copy(r, dest_ref[0, 0, r * TOP_K + k]).start()
        return c

    def drain(r, c):
        for k in range(TOP_K):
            row_copy(r, dest_ref[0, 0, r * TOP_K + k]).wait()
        return c

    lax.fori_loop(0, tm, issue, 0)
    lax.fori_loop(0, tm, drain, 0)


def _dispatch(dest3, hm, xs_zero):
    n, d = hm.shape
    tm = ROW_TILE
    return pl.pallas_call(
        _dispatch_kernel,
        grid=(n // tm,),
        in_specs=[pl.BlockSpec((1, 1, tm * TOP_K), lambda i: (i, 0, 0), memory_space=pltpu.SMEM),
                  pl.BlockSpec((tm, d), lambda i: (i, 0)),
                  pl.BlockSpec(memory_space=pl.ANY)],
        out_specs=pl.BlockSpec(memory_space=pl.ANY),
        out_shape=jax.ShapeDtypeStruct(xs_zero.shape, xs_zero.dtype),
        scratch_shapes=[pltpu.SemaphoreType.DMA],
        input_output_aliases={2: 0},
        compiler_params=_cparams(("arbitrary",)),
        name="moe_dispatch",
    )(dest3, hm, xs_zero)


def _expert_kernel(be_ref, bv_ref, xs_ref, wgu_ref, bgu_ref, wd_ref, bd_ref, y_ref):
    i = pl.program_id(0)
    d_ff = wd_ref.shape[1]

    @pl.when(bv_ref[i] > 0)
    def _():
        x = xs_ref[...].astype(BF16)
        hgu = _dot(x, wgu_ref[0].astype(BF16)) + bgu_ref[0]
        gate = jnp.minimum(hgu[:, :d_ff], SWIGLU_LIMIT)
        up = jnp.clip(hgu[:, d_ff:], -SWIGLU_LIMIT, SWIGLU_LIMIT)
        act = (up + 1.0) * gate * _sigmoid(SWIGLU_ALPHA * gate)
        y_ref[...] = _dot(act.astype(BF16), wd_ref[0].astype(BF16)) + bd_ref[0]

    @pl.when(bv_ref[i] == 0)
    def _():
        y_ref[...] = jnp.zeros(y_ref.shape, y_ref.dtype)


def _experts(block_expert, block_valid, xs, w_gu, b_gu3, w_down, b_down3):
    n_slots, d = xs.shape
    n_blocks = n_slots // MOE_BLK
    d_ff = w_down.shape[1]
    return pl.pallas_call(
        _expert_kernel,
        grid_spec=pltpu.PrefetchScalarGridSpec(
            num_scalar_prefetch=2,
            grid=(n_blocks,),
            in_specs=[pl.BlockSpec((MOE_BLK, d), lambda i, be, bv: (i, 0)),
                      pl.BlockSpec((1, d, 2 * d_ff), lambda i, be, bv: (be[i], 0, 0)),
                      pl.BlockSpec((1, 1, 2 * d_ff), lambda i, be, bv: (be[i], 0, 0)),
                      pl.BlockSpec((1, d_ff, d), lambda i, be, bv: (be[i], 0, 0)),
                      pl.BlockSpec((1, 1, d), lambda i, be, bv: (be[i], 0, 0))],
            out_specs=pl.BlockSpec((MOE_BLK, d), lambda i, be, bv: (i, 0)),
        ),
        out_shape=jax.ShapeDtypeStruct((n_slots, d), F32),
        compiler_params=_cparams(("arbitrary",)),
        name="moe_experts",
    )(block_expert, block_valid, xs, w_gu, b_gu3, w_down, b_down3)


def _combine_kernel(dest_ref, yb_ref, tg_ref, x1_ref, y_ref, buf, sem):
    tm = x1_ref.shape[0]

    def row_copy(r, k, d):
        return pltpu.make_async_copy(yb_ref.at[pl.ds(d, 1)], buf.at[k, pl.ds(r, 1)], sem)

    def issue(r, c):
        for k in range(TOP_K):
            row_copy(r, k, dest_ref[0, 0, r * TOP_K + k]).start()
        return c

    def drain(r, c):
        for k in range(TOP_K):
            row_copy(r, k, dest_ref[0, 0, r * TOP_K + k]).wait()
        return c

    lax.fori_loop(0, tm, issue, 0)
    lax.fori_loop(0, tm, drain, 0)
    tg = tg_ref[...]
    lane = lax.broadcasted_iota(jnp.int32, tg.shape, 1)
    acc = jnp.zeros(x1_ref.shape, F32)
    for k in range(TOP_K):
        gk = jnp.sum(jnp.where(lane == k, tg, 0.0), axis=-1, keepdims=True)
        acc = acc + gk * buf[k]
    y_ref[...] = x1_ref[...] + acc


def _combine(dest3, yb, tg, x1):
    n, d = x1.shape
    tm = ROW_TILE
    return pl.pallas_call(
        _combine_kernel,
        grid=(n // tm,),
        in_specs=[pl.BlockSpec((1, 1, tm * TOP_K), lambda i: (i, 0, 0), memory_space=pltpu.SMEM),
                  pl.BlockSpec(memory_space=pl.ANY),
                  pl.BlockSpec((tm, LANES), lambda i: (i, 0)),
                  pl.BlockSpec((tm, d), lambda i: (i, 0))],
        out_specs=pl.BlockSpec((tm, d), lambda i: (i, 0)),
        out_shape=jax.ShapeDtypeStruct((n, d), F32),
        scratch_shapes=[pltpu.VMEM((TOP_K, tm, d), F32), pltpu.SemaphoreType.DMA],
        compiler_params=_cparams(("arbitrary",)),
        name="moe_combine",
    )(dest3, yb, tg, x1)


def _moe(hm, ti, tg, x1, w_gu, b_gu, w_down, b_down):
    n, d = hm.shape
    rank, cnt = _rank(ti)
    counts = cnt[0, :N_EXPERTS].astype(jnp.int32)
    padded = (counts + MOE_BLK - 1) // MOE_BLK * MOE_BLK
    pad_end = jnp.cumsum(padded)
    pad_start = pad_end - padded
    top_i = ti[:, :TOP_K]
    dest = (pad_start[top_i] + rank[:, :TOP_K]).astype(jnp.int32)
    dest3 = dest.reshape(n // ROW_TILE, 1, ROW_TILE * TOP_K)
    n_blocks = (n * TOP_K) // MOE_BLK + N_EXPERTS
    blk_start = jnp.arange(n_blocks, dtype=jnp.int32) * MOE_BLK
    block_expert = jnp.minimum(jnp.searchsorted(pad_end, blk_start, side='right'), N_EXPERTS - 1).astype(jnp.int32)
    block_valid = (blk_start < pad_end[-1]).astype(jnp.int32)
    xs = _dispatch(dest3, hm, jnp.zeros((n_blocks * MOE_BLK, d), F32))
    yb = _experts(block_expert, block_valid, xs, w_gu, b_gu.reshape(N_EXPERTS, 1, -1), w_down,
                  b_down.reshape(N_EXPERTS, 1, -1))
    return _combine(dest3, yb, tg, x1)


def _pad_lanes(v, fill=0.0):
    v = v.reshape(1, -1).astype(F32)
    return jnp.pad(v, ((0, 0), (0, LANES - v.shape[1])), constant_values=fill)


def kernel(x_prompt, x_sample, cache_k, cache_v, state_ssm, state_conv, page_table, meta_tokens, rel_bias_table, norm1, w_in, conv_w, a_log, dt_bias, gdn_norm, q_norm, k_norm, lam_q1, lam_k1, lam_q2, lam_k2, sub_norm, w_br_a, w_br_b, w_out, norm2, w_router, b_router, w_gu, b_gu, w_down, b_down):
    bsz, seq, d = x_prompt.shape
    dbs, dseq, _ = x_sample.shape
    depth = w_in.shape[0]
    assert depth == 1 and dseq <= SUBLANES and seq % ATT_BLK == 0
    n_pages = page_table.shape[1]
    lam_init = 0.8 - 0.6 * math.exp(-0.3 * 0)
    l = 0

    w = w_in[l]
    o_z = CONV_DIM + VA
    w_main = jnp.concatenate([w[:, :o_z], w[:, o_z + 2 * H_A:]], axis=1).astype(BF16)
    w_ba = jnp.pad(w[:, o_z:o_z + 2 * H_A], ((0, 0), (0, LANES - 2 * H_A))).astype(BF16)
    grp = np.arange(QB) // DK_B
    gmat = jnp.asarray((grp[:, None] == grp[None, :]).astype(np.float32) / DK_B, dtype=BF16)
    qn_t = jnp.tile(q_norm[l].astype(F32), QB // DK_B).reshape(1, QB)
    kn_t = jnp.tile(k_norm[l].astype(F32), QB // DK_B).reshape(1, QB)
    alog_p = jnp.pad(a_log[l].astype(F32), (H_A, LANES - 2 * H_A)).reshape(1, LANES)
    dtb_p = jnp.pad(dt_bias[l].astype(F32), (H_A, LANES - 2 * H_A)).reshape(1, LANES)
    n1 = norm1[l].reshape(1, d).astype(F32)
    proj = functools.partial(_inproj, n1=n1, w_main=w_main, w_ba=w_ba, gmat=gmat, qn_t=qn_t, kn_t=kn_t,
                             alog_p=alog_p, dtb_p=dtb_p)
    lam4 = jnp.stack([lam_q1[l], lam_k1[l], lam_q2[l], lam_k2[l]]).astype(F32)
    sn = sub_norm[l].reshape(1, DV_B).astype(F32)
    gn = gdn_norm[l].reshape(1, DV_A).astype(F32)
    cw = conv_w[l].astype(F32)

    ii = np.arange(ATT_BLK)[:, None]
    jj = np.arange(LANES)[None, :]
    bk_prompt = np.stack([_bucket_np(ii - jj), _bucket_np(ATT_BLK + ii - jj)])
    t8 = np.arange(SUBLANES)[:, None]
    bk_sample = np.stack([_bucket_np(PAGE + t8 - jj), _bucket_np(t8 - jj)])
    bt_prompt = _bias_tiles(rel_bias_table.astype(F32), bk_prompt)
    bt_sample = _bias_tiles(rel_bias_table.astype(F32), bk_sample)
    far_h = rel_bias_table[N_BUCKETS - 1].astype(F32)
    far_p = jnp.broadcast_to(far_h[:, None, None], (H_B, 1, LANES))
    nrow = H_B * 2 * SUBLANES
    rows_of = lambda t: jnp.broadcast_to(t[:, None], (H_B, 2, SUBLANES, LANES)).reshape(nrow, LANES)
    last_t, self_t = rows_of(bt_sample[:, 0]), rows_of(bt_sample[:, 1])
    far_col = jnp.broadcast_to(far_h[:, None, None], (H_B, 2 * SUBLANES, LANES)).reshape(nrow, LANES)

    lreal = N_META + seq
    lp = -(-lreal // ATT_BLK) * ATT_BLK
    fp = lp - lreal
    assert fp % SUBLANES == 0 and fp >= CONV_W - 1 and lp % GDN_CHUNK == 0 and (fp + N_META) % ATT_BLK == 0
    xp = jnp.concatenate([jnp.zeros((bsz, fp, d), F32),
                          jnp.broadcast_to(meta_tokens.astype(F32)[None], (bsz, N_META, d)), x_prompt], axis=1)
    conv_p, z_p, bg_p, qd_p, kd_p, vd_p, sga_p, sgb_p = proj(xp.reshape(bsz * lp, d), ATT_BLK)
    r3 = lambda a, b_, r_: a.reshape(b_, r_, a.shape[-1])
    oa_p, ssm_p = _gdn(r3(conv_p, bsz, lp), r3(bg_p, bsz, lp), r3(z_p, bsz, lp),
                       jnp.zeros((bsz, H_A, DK_A, DV_A), F32), cw, gn, fp)
    ob_p = _attn_prompt(r3(qd_p, bsz, lp), r3(kd_p, bsz, lp), r3(vd_p, bsz, lp), bt_prompt, far_p, lam4, sn,
                        fp, lam_init)

    ns = dbs * dseq
    conv_s, z_s, bg_s, qd_s, kd_s, vd_s, sga_s, sgb_s = proj(x_sample.reshape(ns, d), min(ns, 256))
    c = GDN_CHUNK
    fs = c - dseq

    def chunk_rows(a, head=None):
        a = a.reshape(dbs, dseq, a.shape[-1])
        parts = [jnp.zeros((dbs, fs - (0 if head is None else head.shape[1]), a.shape[-1]), a.dtype)]
        if head is not None:
            parts.append(head.astype(a.dtype))
        return jnp.concatenate(parts + [a], axis=1)

    oa_s, ssm_s = _gdn(chunk_rows(conv_s, state_conv[l]), chunk_rows(bg_s), chunk_rows(z_s),
                       state_ssm[l].astype(F32), cw, gn, fs)
    pad8 = lambda a: jnp.pad(a.reshape(dbs, dseq, a.shape[-1]), ((0, 0), (0, SUBLANES - dseq), (0, 0)))
    ob_s = _attn_sample(page_table, pad8(qd_s).astype(F32), pad8(kd_s), pad8(vd_s),
                        cache_k[l].reshape(-1, PAGE, QB), cache_v[l].reshape(-1, PAGE, VB),
                        last_t, self_t, far_col, lam4, sn, dseq, lam_init)

    wa, wb, wo = w_br_a[l].astype(BF16), w_br_b[l].astype(BF16), w_out[l].astype(BF16)
    n2 = norm2[l].reshape(1, d).astype(F32)
    wr = jnp.pad(w_router[l].astype(F32), ((0, 0), (0, LANES - N_EXPERTS)))
    br = _pad_lanes(b_router[l])
    tiles_b = seq // ROW_TILE
    tiles_lp = lp // ROW_TILE
    skip = (fp + N_META) // ROW_TILE
    x1_p, hm_p, ti_p, tg_p = _post(x_prompt.reshape(bsz * seq, d), oa_p.reshape(bsz * lp, VA),
                                   ob_p.reshape(bsz * lp, VB), sga_p, sgb_p, wa, wb, wo, n2, wr, br, bsz * seq,
                                   lambda i: i, lambda i: (i // tiles_b) * tiles_lp + skip + i % tiles_b)
    assert ns % ROW_TILE == 0
    oa_s2 = oa_s[:, fs:].reshape(ns, VA)
    ob_s2 = ob_s[:, :dseq].reshape(ns, VB)
    x1_s, hm_s, ti_s, tg_s = _post(x_sample.reshape(ns, d), oa_s2, ob_s2, sga_s, sgb_s, wa, wb, wo, n2, wr, br, ns,
                                   lambda i: i, lambda i: i)

    cat = lambda a, b_: jnp.concatenate([a, b_], axis=0)
    y = _moe(cat(hm_p, hm_s), cat(ti_p, ti_s), cat(tg_p, tg_s), cat(x1_p, x1_s),
             w_gu[l], b_gu[l], w_down[l], b_down[l])
    npt = bsz * seq
    y_prompt = y[:npt].reshape(bsz, seq, d)
    y_sample = y[npt:].reshape(dbs, dseq, d)

    k_prompt = r3(kd_p, bsz, lp)[:, fp:].reshape(1, bsz, lreal, H_B, 2, DK_B)
    v_prompt = r3(vd_p, bsz, lp)[:, fp:].reshape(1, bsz, lreal, H_B, DV_B)
    conv_prompt = r3(conv_p, bsz, lp)[:, lp - (CONV_W - 1):][None]
    xpad_s = jnp.concatenate([state_conv[l].astype(F32), conv_s.reshape(dbs, dseq, CONV_DIM)], axis=1)
    conv_sample = xpad_s[:, dseq:][None]
    return (y_prompt, y_sample, k_prompt, v_prompt, ssm_p[None], conv_prompt,
            kd_s.reshape(1, dbs, dseq, H_B, 2, DK_B), vd_s.reshape(1, dbs, dseq, H_B, DV_B), ssm_s[None], conv_sample)
```

```python
import functools
import math

import numpy as np
import jax
import jax.numpy as jnp
from jax import lax
from jax.experimental import pallas as pl
from jax.experimental.pallas import tpu as pltpu

F32 = jnp.float32
BF16 = jnp.bfloat16
HI = lax.Precision.HIGHEST

N_META = 16
H_A, DK_A, DV_A = 4, 128, 128
CONV_W = 4
H_B, DK_B = 4, 64
DV_B = 2 * DK_B
N_BUCKETS, MAX_DIST = 32, 128
N_EXPERTS, TOP_K = 32, 4
SWIGLU_LIMIT, SWIGLU_ALPHA = 7.0, 1.702
EPS = 1e-6
PAGE = 128
QA = H_A * DK_A
VA = H_A * DV_A
QB = H_B * 2 * DK_B
VB = H_B * DV_B
CONV_DIM = 2 * QA + VA

LANES = 128
SUBLANES = 8
VMEM_LIMIT = 56 * 1024 * 1024

GDN_CHUNK = 64
ATT_BLK = 128
ATT_KV = 2 * ATT_BLK
PAGES_PER_STEP = 8
MOE_BLK = 256
ROW_TILE = 128
NEG = -1e30


def _dot(a, b, prec=None):
    return jnp.dot(a, b, preferred_element_type=F32, precision=prec)


def _dot_nt(a, b, prec=None):
    return lax.dot_general(a, b, (((1,), (1,)), ((), ())), preferred_element_type=F32, precision=prec)


def _dot_tn(a, b, prec=None):
    return lax.dot_general(a, b, (((0,), (0,)), ((), ())), preferred_element_type=F32, precision=prec)


def _split(a):
    hi = a.astype(BF16)
    return hi, (a - hi.astype(F32)).astype(BF16)


def _dot3(a, b):
    return _dot(a[0], b[0]) + _dot(a[1], b[0]) + _dot(a[0], b[1])


def _sigmoid(x):
    return 1.0 / (1.0 + jnp.exp(-x))


def _cparams(sem):
    return pltpu.CompilerParams(dimension_semantics=sem, vmem_limit_bytes=VMEM_LIMIT)


def _bucket_np(d):
    d = np.maximum(d, 0)
    df = np.maximum(d, 1).astype(np.float32)
    max_exact = N_BUCKETS // 2
    large = max_exact + (np.log(df / np.float32(max_exact)) / np.float32(math.log(MAX_DIST / max_exact))
                         * np.float32(N_BUCKETS - max_exact)).astype(np.int32)
    return np.where(d < max_exact, d, np.minimum(large, N_BUCKETS - 1)).astype(np.int32)


def _bias_kernel(tab_ref, bk_ref, o_ref):
    for t in range(bk_ref.shape[0]):
        bk = bk_ref[t]
        for h in range(H_B):
            acc = jnp.zeros(bk.shape, F32)
            for b in range(N_BUCKETS):
                acc = jnp.where(bk == b, tab_ref[b, h], acc)
            o_ref[h, t] = acc


def _bias_tiles(table, buckets):
    t, r, _ = buckets.shape
    return pl.pallas_call(
        _bias_kernel,
        out_shape=jax.ShapeDtypeStruct((H_B, t, r, LANES), F32),
        in_specs=[pl.BlockSpec(memory_space=pltpu.SMEM), pl.BlockSpec(memory_space=pltpu.VMEM)],
        out_specs=pl.BlockSpec(memory_space=pltpu.VMEM),
        name="rel_bias_tiles",
    )(table, jnp.asarray(buckets))


def _inproj_kernel(x_ref, n1_ref, w_ref, wba_ref, gm_ref, qn_ref, kn_ref, alog_ref, dtb_ref,
                   conv_ref, z_ref, bg_ref, qd_ref, kd_ref, vd_ref, sga_ref, sgb_ref, kdb_ref, vdb_ref):
    x = x_ref[...]
    ms = jnp.mean(x * x, axis=-1, keepdims=True)
    h = (x * lax.rsqrt(ms + EPS) * n1_ref[...]).astype(BF16)
    o = 0
    conv_ref[...] = _dot(h, w_ref[:, o:o + CONV_DIM]); o += CONV_DIM
    z_ref[...] = _dot(h, w_ref[:, o:o + VA]).astype(z_ref.dtype); o += VA

    def group_norm(y, g):
        sq = y * y
        hi = sq.astype(BF16)
        lo = (sq - hi.astype(F32)).astype(BF16)
        msq = _dot(hi, gm_ref[...]) + _dot(lo, gm_ref[...])
        return y * lax.rsqrt(msq + EPS) * g

    qb = _dot(h, w_ref[:, o:o + QB]); o += QB
    qd_ref[...] = (group_norm(qb, qn_ref[...]) * (DK_B ** -0.5)).astype(qd_ref.dtype)
    kb = _dot(h, w_ref[:, o:o + QB]); o += QB
    kd = group_norm(kb, kn_ref[...])
    kd_ref[...] = kd
    kdb_ref[...] = kd.astype(kdb_ref.dtype)
    vd = _dot(h, w_ref[:, o:o + VB]); o += VB
    vd_ref[...] = vd
    vdb_ref[...] = vd.astype(vdb_ref.dtype)
    d_model = x.shape[1]
    sga_ref[...] = _sigmoid(_dot(h, w_ref[:, o:o + d_model])).astype(sga_ref.dtype); o += d_model
    sgb_ref[...] = _sigmoid(_dot(h, w_ref[:, o:o + d_model])).astype(sgb_ref.dtype)
    t = _dot(h, wba_ref[...])
    lane = lax.broadcasted_iota(jnp.int32, t.shape, 1)
    ta = t + dtb_ref[...]
    sp = jnp.maximum(ta, 0.0) + jnp.log(1.0 + jnp.exp(-jnp.abs(ta)))
    bg_ref[...] = jnp.where(lane < H_A, _sigmoid(t), -jnp.exp(alog_ref[...]) * sp)


def _inproj(x2d, tm, n1, w_main, w_ba, gmat, qn_t, kn_t, alog_p, dtb_p):
    n, d = x2d.shape
    assert n % tm == 0
    row = lambda c: pl.BlockSpec((tm, c), lambda i: (i, 0))
    full = lambda a: pl.BlockSpec(a.shape, lambda i: (0,) * a.ndim)
    outs = [(CONV_DIM, F32), (VA, BF16), (LANES, F32), (QB, BF16), (QB, F32), (VB, F32), (d, BF16), (d, BF16),
            (QB, BF16), (VB, BF16)]
    return pl.pallas_call(
        _inproj_kernel,
        grid=(n // tm,),
        in_specs=[row(d)] + [full(a) for a in (n1, w_main, w_ba, gmat, qn_t, kn_t, alog_p, dtb_p)],
        out_specs=[row(c) for c, _ in outs],
        out_shape=[jax.ShapeDtypeStruct((n, c), dt) for c, dt in outs],
        compiler_params=_cparams(("parallel",)),
        name="in_proj",
    )(x2d, n1, w_main, w_ba, gmat, qn_t, kn_t, alog_p, dtb_p)


def _gdn_kernel(fv, c, q_ref, k_ref, v_ref, cwq_ref, cwk_ref, cwv_ref, bg_ref, z_ref, s0_ref, gn_ref,
                o_ref, st_ref):
    nb, r, wd = q_ref.shape
    hg = wd // LANES
    head0 = pl.program_id(1) * hg
    chains = [(b, hh) for b in range(nb) for hh in range(hg)]

    ri = lax.broadcasted_iota(jnp.int32, (c, c), 0)
    ci = lax.broadcasted_iota(jnp.int32, (c, c), 1)
    incl = ri >= ci
    strict = ri > ci
    eye = (ri == ci).astype(F32)
    lane = lax.broadcasted_iota(jnp.int32, (c, LANES), 1)
    sub_t = lax.broadcasted_iota(jnp.int32, (LANES, c), 0)
    rowid = lax.broadcasted_iota(jnp.int32, (c, 1), 0)
    colid = lax.broadcasted_iota(jnp.int32, (1, c), 1)

    def conv(x_ref, cw_ref, b, cols, r0):
        w = x_ref[b, pl.ds(r0 - SUBLANES, c + SUBLANES), cols]
        acc = w[SUBLANES - 3:SUBLANES - 3 + c] * cw_ref[0:1, cols]
        for j in range(1, CONV_W):
            acc = acc + w[SUBLANES - 3 + j:SUBLANES - 3 + j + c] * cw_ref[j:j + 1, cols]
        return acc * _sigmoid(acc)

    def prepare(b, hh, r0):
        cols = slice(hh * LANES, (hh + 1) * LANES)
        head = head0 + hh
        vcol = ((r0 + rowid) >= fv).astype(F32)
        vrow = ((r0 + colid) >= fv).astype(F32)
        qv = conv(q_ref, cwq_ref, b, cols, r0)
        kv = conv(k_ref, cwk_ref, b, cols, r0)
        v = conv(v_ref, cwv_ref, b, cols, r0) * vcol
        q = qv * lax.rsqrt(jnp.sum(qv * qv, axis=-1, keepdims=True) + EPS) * (DK_A ** -0.5) * vcol
        k = kv * lax.rsqrt(jnp.sum(kv * kv, axis=-1, keepdims=True) + EPS) * vcol
        bgc = bg_ref[b, pl.ds(r0, c), :]
        beta = jnp.sum(jnp.where(lane == head, bgc, 0.0), axis=-1, keepdims=True) * vcol
        g_col = jnp.sum(jnp.where(lane == H_A + head, bgc, 0.0), axis=-1, keepdims=True) * vcol
        g_row = jnp.sum(jnp.where(sub_t == H_A + head, bgc.T, 0.0), axis=0, keepdims=True) * vrow
        gc = jnp.sum(jnp.where(incl, g_row, 0.0), axis=-1, keepdims=True)
        gr = jnp.sum(jnp.where(ri <= ci, g_col, 0.0), axis=0, keepdims=True)
        decay = jnp.where(incl, jnp.exp(jnp.where(incl, gc - gr, 0.0)), 0.0)
        kb = k * beta
        kbh = kb.astype(BF16)
        kh = k.astype(BF16)
        low = jnp.where(strict, _dot_nt(kbh, kh) * decay, 0.0)
        inv = eye - low
        pw = _split(low)
        for _ in range(int(math.log2(c)) - 1):
            pw = _split(_dot3(pw, pw))
            inv = inv + _dot3(_split(inv), pw)
        eg = jnp.exp(gc)
        sol = _dot3(_split(inv), _split(jnp.concatenate([v * beta, kb * eg], axis=-1)))
        w_v, w_k = sol[:, :DV_A], sol[:, DV_A:].astype(BF16)
        intra = (_dot_nt(q.astype(BF16), kh) * decay).astype(BF16)
        g_last = jnp.sum(jnp.where(rowid == c - 1, gc, 0.0), axis=0, keepdims=True)
        k_dt = (k * jnp.exp(g_last - gc)).T.astype(BF16)
        return w_v, w_k, intra, (q * eg).astype(BF16), k_dt, jnp.exp(g_last)

    def apply(b, hh, r0, prepared, s):
        w_v, w_k, intra, q_g, k_dt, eg_last = prepared
        cols = slice(hh * LANES, (hh + 1) * LANES)
        sh = s.astype(BF16)
        u = w_v - _dot(w_k, sh)
        uh = u.astype(BF16)
        o = _dot(q_g, sh) + _dot(intra, uh)
        s = s * eg_last + _dot(k_dt, uh)
        on = o * lax.rsqrt(jnp.mean(o * o, axis=-1, keepdims=True) + EPS) * gn_ref[...]
        zc = z_ref[b, pl.ds(r0, c), cols].astype(F32)
        o_ref[b, pl.ds(r0, c), cols] = (on * (zc * _sigmoid(zc))).astype(o_ref.dtype)
        return s

    def prepare_all(j):
        r0 = pl.multiple_of(j * c, c)
        return tuple(prepare(b, hh, r0) for b, hh in chains)

    def apply_all(j, prepared, states):
        r0 = pl.multiple_of(j * c, c)
        return tuple(apply(b, hh, r0, p, s) for (b, hh), p, s in zip(chains, prepared, states))

    def chunk(j, carry):
        prepared, states = carry
        return prepare_all(j + 1), apply_all(j, prepared, states)

    j0 = fv // c
    last = r // c - 1
    o_ref[:, 0:j0 * c, :] = jnp.zeros((nb, j0 * c, wd), o_ref.dtype)
    init = (prepare_all(j0), tuple(s0_ref[b, hh] for b, hh in chains))
    prepared, states = lax.fori_loop(j0, last, chunk, init)
    for (b, hh), s in zip(chains, apply_all(last, prepared, states)):
        st_ref[b, hh] = s


def _gdn(conv3, bg3, z3, s0, conv_w, gn, fv, c, nb, hg):
    bsz, r, _ = conv3.shape
    assert r % c == 0 and fv // c >= 1 and c >= SUBLANES and bsz % nb == 0 and H_A % hg == 0
    ng = H_A // hg
    wd = hg * LANES
    blk = lambda off: pl.BlockSpec((nb, r, wd), lambda i, g: (i, 0, off * ng + g))
    cw = lambda off: pl.BlockSpec((CONV_W, wd), lambda i, g: (0, off * ng + g))
    st = pl.BlockSpec((nb, hg, DK_A, DV_A), lambda i, g: (i, g, 0, 0))
    return pl.pallas_call(
        functools.partial(_gdn_kernel, fv, c),
        grid=(bsz // nb, ng),
        in_specs=[blk(0), blk(1), blk(2), cw(0), cw(1), cw(2),
                  pl.BlockSpec((nb, r, LANES), lambda i, g: (i, 0, 0)), blk(0), st,
                  pl.BlockSpec((1, LANES), lambda i, g: (0, 0))],
        out_specs=[blk(0), st],
        out_shape=[jax.ShapeDtypeStruct((bsz, r, VA), F32), jax.ShapeDtypeStruct((bsz, H_A, DK_A, DV_A), F32)],
        compiler_params=_cparams(("parallel", "parallel")),
        name="gdn",
    )(conv3, conv3, conv3, conv_w, conv_w, conv_w, bg3, z3, s0, gn)


def _lambda(lam_ref, lam_init):
    l1 = jnp.sum(lam_ref[0:1, :] * lam_ref[1:2, :], axis=-1, keepdims=True)
    l2 = jnp.sum(lam_ref[2:3, :] * lam_ref[3:4, :], axis=-1, keepdims=True)
    return jnp.exp(l1) - jnp.exp(l2) + lam_init


def _sub_norm(o0, o1, lam, sn, lam_init):
    ob = o0 - lam * o1
    return ob * lax.rsqrt(jnp.mean(ob * ob, axis=-1, keepdims=True) + EPS) * sn * (1.0 - lam_init)


def _attn_prompt_kernel(first_q, lam_init, q_ref, k_ref, v_ref, bt_ref, far_ref, kb_ref, lam_ref, sn_ref, o_ref):
    blk = ATT_BLK
    qi = pl.program_id(2)

    @pl.when(qi < first_q)
    def _():
        o_ref[0] = jnp.zeros(o_ref.shape[1:], o_ref.dtype)

    @pl.when(qi >= first_q)
    def _():
        q = q_ref[0]
        lane = lax.broadcasted_iota(jnp.int32, q.shape, 1)
        zero = jnp.zeros_like(q)
        qs = jnp.concatenate([jnp.where(lane < DK_B, q, zero), jnp.where(lane >= DK_B, q, zero)], axis=0)
        diag = bt_ref[0, 0]
        near = bt_ref[0, 1]
        far = far_ref[0]

        def body(j, carry):
            m, l, acc = carry
            k0 = pl.multiple_of(j * ATT_KV, ATT_KV)
            ks = k_ref[0, pl.ds(k0, ATT_KV), :]
            vs = v_ref[0, pl.ds(k0, ATT_KV), :]
            halves = []
            for hb in range(ATT_KV // blk):
                rel = qi - ((ATT_KV // blk) * j + hb)
                halves.append(jnp.where(rel == 0, diag, jnp.where(rel == 1, near, jnp.where(rel < 0, NEG, far))))
            bias = jnp.concatenate(halves, axis=1) + kb_ref[pl.ds(j, 1), :]
            s = _dot_nt(qs, ks) + jnp.concatenate([bias, bias], axis=0)
            m_new = jnp.maximum(m, jnp.max(s, axis=-1, keepdims=True))
            alpha = jnp.exp(m - m_new)
            p = jnp.exp(s - m_new)
            l = alpha * l + jnp.sum(p, axis=-1, keepdims=True)
            acc = alpha * acc + _dot(p.astype(BF16), vs)
            return m_new, l, acc

        init = (jnp.full((2 * blk, 1), NEG, F32), jnp.zeros((2 * blk, 1), F32), jnp.zeros((2 * blk, DV_B), F32))
        _, l, acc = lax.fori_loop(0, qi // (ATT_KV // blk) + 1, body, init)
        o = acc / l
        o_ref[0] = _sub_norm(o[:blk], o[blk:], _lambda(lam_ref, lam_init), sn_ref[...], lam_init).astype(o_ref.dtype)


def _attn_prompt(qd3, kd3, vd3, btiles, far, kbias, lam4, sn, first_q, lam_init):
    bsz, lp, _ = qd3.shape
    assert lp % ATT_KV == 0
    kv = pl.BlockSpec((1, lp, LANES), lambda b, h, i: (b, 0, h))
    qo = pl.BlockSpec((1, ATT_BLK, LANES), lambda b, h, i: (b, i, h))
    full = lambda a: pl.BlockSpec(a.shape, lambda b, h, i: (0,) * a.ndim)
    return pl.pallas_call(
        functools.partial(_attn_prompt_kernel, first_q, lam_init),
        grid=(bsz, H_B, lp // ATT_BLK),
        in_specs=[qo, kv, kv,
                  pl.BlockSpec((1, 2, ATT_BLK, LANES), lambda b, h, i: (h, 0, 0, 0)),
                  pl.BlockSpec((1, 1, LANES), lambda b, h, i: (h, 0, 0)),
                  full(kbias), full(lam4), full(sn)],
        out_specs=qo,
        out_shape=jax.ShapeDtypeStruct((bsz, lp, VB), F32),
        compiler_params=_cparams(("parallel", "parallel", "parallel")),
        name="diff_attn_prompt",
    )(qd3, kd3, vd3, btiles, far, kbias, lam4, sn)


def _attn_sample_kernel(npp, t_new, lam_init, pt_ref, q_ref, kn_ref, vn_ref, *rest):
    k_refs, v_refs = rest[:npp], rest[npp:2 * npp]
    last_ref, self_ref, far_ref, lam_ref, sn_ref, o_ref, qm, kself, vself, m_s, l_s, acc_s = rest[2 * npp:]
    st = pl.program_id(1)
    nst = pl.num_programs(1)
    nrow = H_B * 2 * SUBLANES
    hrows = 2 * SUBLANES

    def update(s, values_of_head):
        m_old = m_s[...]
        m_new = jnp.maximum(m_old, jnp.max(s, axis=-1, keepdims=True))
        alpha = jnp.exp(m_old - m_new)
        pr = jnp.exp(s - m_new)
        l_s[...] = alpha * l_s[...] + jnp.sum(pr, axis=-1, keepdims=True)
        prb = pr.astype(BF16)
        pv = [_dot(prb[h * hrows:(h + 1) * hrows], values_of_head(h)) for h in range(H_B)]
        acc_s[...] = alpha * acc_s[...] + jnp.concatenate(pv, axis=0)
        m_s[...] = m_new

    @pl.when(st == 0)
    def _():
        q8 = q_ref[0]
        lane = lax.broadcasted_iota(jnp.int32, q8.shape, 1)
        qm[...] = jnp.concatenate([jnp.where(lane // DK_B == g, q8, 0.0) for g in range(2 * H_B)],
                                  axis=0).astype(qm.dtype)
        kself[...] = jnp.zeros(kself.shape, F32)
        vself[...] = jnp.zeros(vself.shape, F32)
        kself[0:SUBLANES, :] = kn_ref[0]
        vself[0:SUBLANES, :] = vn_ref[0]
        m_s[...] = jnp.full(m_s.shape, NEG, F32)
        l_s[...] = jnp.zeros(l_s.shape, F32)
        acc_s[...] = jnp.zeros(acc_s.shape, F32)
        rows = lax.broadcasted_iota(jnp.int32, (nrow, PAGE), 0) % SUBLANES
        cols = lax.broadcasted_iota(jnp.int32, (nrow, PAGE), 1)
        ok = (cols <= rows) & (cols < t_new)
        s = jnp.where(ok, _dot_nt(qm[...], kself[...].astype(BF16)) + self_ref[...], NEG)
        update(s, lambda h: vself[:, h * DV_B:(h + 1) * DV_B].astype(BF16))

    far = far_ref[:, 0:1]
    parts = []
    for g in range(npp):
        bias = far if g < npp - 1 else jnp.where(st == nst - 1, last_ref[...], far)
        parts.append(_dot(qm[...], k_refs[g][0].astype(BF16)) + bias)

    def page_values(h):
        return jnp.concatenate([v_refs[g][0, pl.ds(h, PAGE, stride=H_B), :] for g in range(npp)],
                               axis=0).astype(BF16)

    update(jnp.concatenate(parts, axis=1), page_values)

    @pl.when(st == nst - 1)
    def _():
        o = acc_s[...] / l_s[...]
        lam = _lambda(lam_ref, lam_init)
        for h in range(H_B):
            r0 = h * hrows
            o_ref[0, :, h * DV_B:(h + 1) * DV_B] = _sub_norm(o[r0:r0 + SUBLANES], o[r0 + SUBLANES:r0 + hrows], lam,
                                                             sn_ref[...], lam_init)


def _attn_sample(page_table, q8, kn8, vn8, ck_t, cv_i, last_t, self_t, far_col, lam4, sn, t_new, lam_init):
    bsz, npg = page_table.shape
    npp = math.gcd(PAGES_PER_STEP, npg)
    nrow = H_B * 2 * SUBLANES
    tok = pl.BlockSpec((1, SUBLANES, QB), lambda b, s, pt: (b, 0, 0))
    page = lambda g: pl.BlockSpec((1, QB, PAGE), lambda b, s, pt: (pt[b, s * npp + g], 0, 0))
    full = lambda a: pl.BlockSpec(a.shape, lambda b, s, pt: (0,) * a.ndim)
    return pl.pallas_call(
        functools.partial(_attn_sample_kernel, npp, t_new, lam_init),
        grid_spec=pltpu.PrefetchScalarGridSpec(
            num_scalar_prefetch=1,
            grid=(bsz, npg // npp),
            in_specs=[tok, tok, tok] + [page(g) for g in range(npp)] * 2
                     + [full(last_t), full(self_t), full(far_col), full(lam4), full(sn)],
            out_specs=tok,
            scratch_shapes=[pltpu.VMEM((nrow, QB), BF16), pltpu.VMEM((PAGE, QB), F32), pltpu.VMEM((PAGE, VB), F32),
                            pltpu.VMEM((nrow, 1), F32), pltpu.VMEM((nrow, 1), F32), pltpu.VMEM((nrow, DV_B), F32)],
        ),
        out_shape=jax.ShapeDtypeStruct((bsz, SUBLANES, VB), F32),
        compiler_params=_cparams(("parallel", "arbitrary")),
        name="diff_attn_sample",
    )(page_table, q8, kn8, vn8, *([ck_t] * npp), *([cv_i] * npp), last_t, self_t, far_col, lam4, sn)


def _post_kernel(x_ref, oa_ref, ob_ref, sga_ref, sgb_ref, wa_ref, wb_ref, wo_ref, n2_ref, wr_ref, br_ref,
                 x1_ref, hm_ref, ti_ref, tg_ref):
    ya = _dot(oa_ref[...].astype(BF16), wa_ref[...])
    yb = _dot(ob_ref[...].astype(BF16), wb_ref[...])
    merged = sga_ref[...].astype(F32) * ya + sgb_ref[...].astype(F32) * yb
    x1 = x_ref[...] + _dot(merged.astype(BF16), wo_ref[...])
    x1_ref[...] = x1
    hm = x1 * lax.rsqrt(jnp.mean(x1 * x1, axis=-1, keepdims=True) + EPS) * n2_ref[...]
    hm_ref[...] = hm
    logits = _dot(hm, wr_ref[...], HI) + br_ref[...]
    lane = lax.broadcasted_iota(jnp.int32, logits.shape, 1)
    lane_f = lane.astype(F32)
    work = jnp.where(lane < N_EXPERTS, logits, -jnp.inf)
    ti = jnp.zeros(logits.shape, F32)
    tg = jnp.zeros(logits.shape, F32)
    v0 = None
    den = None
    for k in range(TOP_K):
        vk = jnp.max(work, axis=-1, keepdims=True)
        ik = jnp.min(jnp.where(work == vk, lane_f, float(LANES)), axis=-1, keepdims=True)
        work = jnp.where(lane_f == ik, -jnp.inf, work)
        if k == 0:
            v0 = vk
        ek = jnp.exp(vk - v0)
        den = ek if k == 0 else den + ek
        ti = jnp.where(lane == k, ik, ti)
        tg = jnp.where(lane == k, ek, tg)
    ti_ref[...] = ti.astype(jnp.int32)
    tg_ref[...] = tg / den


def _post(x2d, oa, ob, sga, sgb, wa, wb, wo, n2, wr, br, n_out, x_map, in_map):
    d = x2d.shape[1]
    tm = ROW_TILE
    full = lambda a: pl.BlockSpec(a.shape, lambda i: (0,) * a.ndim)
    rin = lambda c: pl.BlockSpec((tm, c), lambda i: (in_map(i), 0))
    rout = lambda c: pl.BlockSpec((tm, c), lambda i: (i, 0))
    return pl.pallas_call(
        _post_kernel,
        grid=(n_out // tm,),
        in_specs=[pl.BlockSpec((tm, d), lambda i: (x_map(i), 0)), rin(VA), rin(VB), rin(d), rin(d)]
                 + [full(a) for a in (wa, wb, wo, n2, wr, br)],
        out_specs=[rout(d), rout(d), rout(LANES), rout(LANES)],
        out_shape=[jax.ShapeDtypeStruct((n_out, d), F32), jax.ShapeDtypeStruct((n_out, d), F32),
                   jax.ShapeDtypeStruct((n_out, LANES), jnp.int32), jax.ShapeDtypeStruct((n_out, LANES), F32)],
        compiler_params=_cparams(("parallel",)),
        name="merge_outproj_router",
    )(x2d, oa, ob, sga, sgb, wa, wb, wo, n2, wr, br)


def _rank_kernel(ti_ref, rank_ref, cnt_ref, base):
    i = pl.program_id(0)
    tm = ti_ref.shape[0]

    @pl.when(i == 0)
    def _():
        base[...] = jnp.zeros(base.shape, F32)

    ti = ti_ref[...].astype(F32)
    lane = lax.broadcasted_iota(jnp.int32, ti.shape, 1)
    lane_f = lane.astype(F32)
    sel = [jnp.sum(jnp.where(lane == k, ti, 0.0), axis=-1, keepdims=True) for k in range(TOP_K)]
    onehot = jnp.zeros(ti.shape, F32)
    for k in range(TOP_K):
        onehot = onehot + (lane_f == sel[k]).astype(F32)
    ri = lax.broadcasted_iota(jnp.int32, (tm, tm), 0)
    ci = lax.broadcasted_iota(jnp.int32, (tm, tm), 1)
    before = _dot((ri > ci).astype(BF16), onehot.astype(BF16)) + base[...]
    rank = jnp.zeros(ti.shape, jnp.int32)
    for k in range(TOP_K):
        rk = jnp.sum(jnp.where(lane_f == sel[k], before, 0.0), axis=-1, keepdims=True)
        rank = jnp.where(lane == k, rk.astype(jnp.int32), rank)
    rank_ref[...] = rank
    base[...] = base[...] + jnp.sum(onehot, axis=0, keepdims=True)
    cnt_ref[...] = base[...]


def _rank(ti):
    n = ti.shape[0]
    tm = ROW_TILE
    return pl.pallas_call(
        _rank_kernel,
        grid=(n // tm,),
        in_specs=[pl.BlockSpec((tm, LANES), lambda i: (i, 0))],
        out_specs=[pl.BlockSpec((tm, LANES), lambda i: (i, 0)), pl.BlockSpec((1, LANES), lambda i: (0, 0))],
        out_shape=[jax.ShapeDtypeStruct((n, LANES), jnp.int32), jax.ShapeDtypeStruct((1, LANES), F32)],
        scratch_shapes=[pltpu.VMEM((1, LANES), F32)],
        compiler_params=_cparams(("arbitrary",)),
        name="moe_rank",
    )(ti)


def _dispatch_kernel(dest_ref, hm_ref, xs_in, xs_out, sem):
    del xs_in
    tm = hm_ref.shape[0]

    def row_copy(r, d):
        return pltpu.make_async_copy(hm_ref.at[pl.ds(r, 1)], xs_out.at[pl.ds(d, 1)], sem)

    def issue(r, c):
        for k in range(TOP_K):
            row_copy(r, dest_ref[0, 0, r * TOP_K + k]).start()
        return c

    def drain(r, c):
        for k in range(TOP_K):
            row_copy(r, dest_ref[0, 0, r * TOP_K + k]).wait()
        return c

    lax.fori_loop(0, tm, issue, 0)
    lax.fori_loop(0, tm, drain, 0)


def _dispatch(dest3, hm, xs_zero):
    n, d = hm.shape
    tm = ROW_TILE
    return pl.pallas_call(
        _dispatch_kernel,
        grid=(n // tm,),
        in_specs=[pl.BlockSpec((1, 1, tm * TOP_K), lambda i: (i, 0, 0), memory_space=pltpu.SMEM),
                  pl.BlockSpec((tm, d), lambda i: (i, 0)),
                  pl.BlockSpec(memory_space=pl.ANY)],
        out_specs=pl.BlockSpec(memory_space=pl.ANY),
        out_shape=jax.ShapeDtypeStruct(xs_zero.shape, xs_zero.dtype),
        scratch_shapes=[pltpu.SemaphoreType.DMA],
        input_output_aliases={2: 0},
        compiler_params=_cparams(("arbitrary",)),
        name="moe_dispatch",
    )(dest3, hm, xs_zero)


def _expert_kernel(be_ref, bv_ref, xs_ref, wgu_ref, bgu_ref, wd_ref, bd_ref, y_ref):
    i = pl.program_id(0)
    d_ff = wd_ref.shape[1]

    @pl.when(bv_ref[i] > 0)
    def _():
        x = xs_ref[...].astype(BF16)
        hgu = _dot(x, wgu_ref[0].astype(BF16)) + bgu_ref[0]
        gate = jnp.minimum(hgu[:, :d_ff], SWIGLU_LIMIT)
        up = jnp.clip(hgu[:, d_ff:], -SWIGLU_LIMIT, SWIGLU_LIMIT)
        act = (up + 1.0) * gate * _sigmoid(SWIGLU_ALPHA * gate)
        y_ref[...] = _dot(act.astype(BF16), wd_ref[0].astype(BF16)) + bd_ref[0]

    @pl.when(bv_ref[i] == 0)
    def _():
        y_ref[...] = jnp.zeros(y_ref.shape, y_ref.dtype)


def _experts(block_expert, block_valid, xs, w_gu, b_gu3, w_down, b_down3):
    n_slots, d = xs.shape
    n_blocks = n_slots // MOE_BLK
    d_ff = w_down.shape[1]
    return pl.pallas_call(
        _expert_kernel,
        grid_spec=pltpu.PrefetchScalarGridSpec(
            num_scalar_prefetch=2,
            grid=(n_blocks,),
            in_specs=[pl.BlockSpec((MOE_BLK, d), lambda i, be, bv: (i, 0)),
                      pl.BlockSpec((1, d, 2 * d_ff), lambda i, be, bv: (be[i], 0, 0)),
                      pl.BlockSpec((1, 1, 2 * d_ff), lambda i, be, bv: (be[i], 0, 0)),
                      pl.BlockSpec((1, d_ff, d), lambda i, be, bv: (be[i], 0, 0)),
                      pl.BlockSpec((1, 1, d), lambda i, be, bv: (be[i], 0, 0))],
            out_specs=pl.BlockSpec((MOE_BLK, d), lambda i, be, bv: (i, 0)),
        ),
        out_shape=jax.ShapeDtypeStruct((n_slots, d), F32),
        compiler_params=_cparams(("arbitrary",)),
        name="moe_experts",
    )(block_expert, block_valid, xs, w_gu, b_gu3, w_down, b_down3)


def _combine_kernel(dest_ref, yb_ref, tg_ref, x1_ref, y_ref, buf, sem):
    tm = x1_ref.shape[0]

    def row_copy(r, k, d):
        return pltpu.make_async_copy(yb_ref.at[pl.ds(d, 1)], buf.at[k, pl.ds(r, 1)], sem)

    def issue(r, c):
        for k in range(TOP_K):
            row_copy(r, k, dest_ref[0, 0, r * TOP_K + k]).start()
        return c

    def drain(r, c):
        for k in range(TOP_K):
            row_copy(r, k, dest_ref[0, 0, r * TOP_K + k]).wait()
        return c

    lax.fori_loop(0, tm, issue, 0)
    lax.fori_loop(0, tm, drain, 0)
    tg = tg_ref[...]
    lane = lax.broadcasted_iota(jnp.int32, tg.shape, 1)
    acc = jnp.zeros(x1_ref.shape, F32)
    for k in range(TOP_K):
        gk = jnp.sum(jnp.where(lane == k, tg, 0.0), axis=-1, keepdims=True)
        acc = acc + gk * buf[k]
    y_ref[...] = x1_ref[...] + acc


def _combine(dest3, yb, tg, x1):
    n, d = x1.shape
    tm = ROW_TILE
    return pl.pallas_call(
        _combine_kernel,
        grid=(n // tm,),
        in_specs=[pl.BlockSpec((1, 1, tm * TOP_K), lambda i: (i, 0, 0), memory_space=pltpu.SMEM),
                  pl.BlockSpec(memory_space=pl.ANY),
                  pl.BlockSpec((tm, LANES), lambda i: (i, 0)),
                  pl.BlockSpec((tm, d), lambda i: (i, 0))],
        out_specs=pl.BlockSpec((tm, d), lambda i: (i, 0)),
        out_shape=jax.ShapeDtypeStruct((n, d), F32),
        scratch_shapes=[pltpu.VMEM((TOP_K, tm, d), F32), pltpu.SemaphoreType.DMA],
        compiler_params=_cparams(("arbitrary",)),
        name="moe_combine",
    )(dest3, yb, tg, x1)


def _moe(hm, ti, tg, x1, w_gu, b_gu, w_down, b_down):
    n, d = hm.shape
    rank, cnt = _rank(ti)
    counts = cnt[0, :N_EXPERTS].astype(jnp.int32)
    padded = (counts + MOE_BLK - 1) // MOE_BLK * MOE_BLK
    pad_end = jnp.cumsum(padded)
    pad_start = pad_end - padded
    top_i = ti[:, :TOP_K]
    dest = (pad_start[top_i] + rank[:, :TOP_K]).astype(jnp.int32)
    dest3 = dest.reshape(n // ROW_TILE, 1, ROW_TILE * TOP_K)
    n_blocks = (n * TOP_K) // MOE_BLK + N_EXPERTS
    blk_start = jnp.arange(n_blocks, dtype=jnp.int32) * MOE_BLK
    block_expert = jnp.minimum(jnp.searchsorted(pad_end, blk_start, side='right'), N_EXPERTS - 1).astype(jnp.int32)
    block_valid = (blk_start < pad_end[-1]).astype(jnp.int32)
    xs = _dispatch(dest3, hm, jnp.zeros((n_blocks * MOE_BLK, d), F32))
    yb = _experts(block_expert, block_valid, xs, w_gu, b_gu.reshape(N_EXPERTS, 1, -1), w_down,
                  b_down.reshape(N_EXPERTS, 1, -1))
    return _combine(dest3, yb, tg, x1)


def _pad_lanes(v, fill=0.0):
    v = v.reshape(1, -1).astype(F32)
    return jnp.pad(v, ((0, 0), (0, LANES - v.shape[1])), constant_values=fill)


def kernel(x_prompt, x_sample, cache_k, cache_v, state_ssm, state_conv, page_table, meta_tokens, rel_bias_table, norm1, w_in, conv_w, a_log, dt_bias, gdn_norm, q_norm, k_norm, lam_q1, lam_k1, lam_q2, lam_k2, sub_norm, w_br_a, w_br_b, w_out, norm2, w_router, b_router, w_gu, b_gu, w_down, b_down):
    bsz, seq, d = x_prompt.shape
    dbs, dseq, _ = x_sample.shape
    depth = w_in.shape[0]
    assert depth == 1 and dseq <= SUBLANES - (CONV_W - 1) and seq % ATT_BLK == 0
    lam_init = 0.8 - 0.6 * math.exp(-0.3 * 0)
    l = 0

    w = w_in[l]
    o_z = CONV_DIM + VA
    w_main = jnp.concatenate([w[:, :o_z], w[:, o_z + 2 * H_A:]], axis=1).astype(BF16)
    w_ba = jnp.pad(w[:, o_z:o_z + 2 * H_A], ((0, 0), (0, LANES - 2 * H_A))).astype(BF16)
    grp = np.arange(QB) // DK_B
    gmat = jnp.asarray((grp[:, None] == grp[None, :]).astype(np.float32) / DK_B, dtype=BF16)
    qn_t = jnp.tile(q_norm[l].astype(F32), QB // DK_B).reshape(1, QB)
    kn_t = jnp.tile(k_norm[l].astype(F32), QB // DK_B).reshape(1, QB)
    alog_p = jnp.pad(a_log[l].astype(F32), (H_A, LANES - 2 * H_A)).reshape(1, LANES)
    dtb_p = jnp.pad(dt_bias[l].astype(F32), (H_A, LANES - 2 * H_A)).reshape(1, LANES)
    n1 = norm1[l].reshape(1, d).astype(F32)
    proj = functools.partial(_inproj, n1=n1, w_main=w_main, w_ba=w_ba, gmat=gmat, qn_t=qn_t, kn_t=kn_t,
                             alog_p=alog_p, dtb_p=dtb_p)
    lam4 = jnp.stack([lam_q1[l], lam_k1[l], lam_q2[l], lam_k2[l]]).astype(F32)
    sn = sub_norm[l].reshape(1, DV_B).astype(F32)
    gn = gdn_norm[l].reshape(1, DV_A).astype(F32)
    cw = conv_w[l].astype(F32)

    ii = np.arange(ATT_BLK)[:, None]
    jj = np.arange(LANES)[None, :]
    bk_prompt = np.stack([_bucket_np(ii - jj), _bucket_np(ATT_BLK + ii - jj)])
    t8 = np.arange(SUBLANES)[:, None]
    bk_sample = np.stack([_bucket_np(PAGE + t8 - jj), _bucket_np(t8 - jj)])
    bt_prompt = _bias_tiles(rel_bias_table.astype(F32), bk_prompt)
    bt_prompt = bt_prompt.at[:, 0].set(jnp.where(jnp.asarray(ii >= jj), bt_prompt[:, 0], NEG))
    bt_sample = _bias_tiles(rel_bias_table.astype(F32), bk_sample)
    far_h = rel_bias_table[N_BUCKETS - 1].astype(F32)
    far_p = jnp.broadcast_to(far_h[:, None, None], (H_B, 1, LANES))
    nrow = H_B * 2 * SUBLANES
    rows_of = lambda t: jnp.broadcast_to(t[:, None], (H_B, 2, SUBLANES, LANES)).reshape(nrow, LANES)
    last_t, self_t = rows_of(bt_sample[:, 0]), rows_of(bt_sample[:, 1])
    far_col = jnp.broadcast_to(far_h[:, None, None], (H_B, 2 * SUBLANES, LANES)).reshape(nrow, LANES)

    lreal = N_META + seq
    lp = -(-lreal // ATT_KV) * ATT_KV
    fp = lp - lreal
    assert fp % SUBLANES == 0 and fp >= GDN_CHUNK and (fp + N_META) % ATT_BLK == 0 and lp % GDN_CHUNK == 0
    xp = jnp.concatenate([jnp.zeros((bsz, fp, d), F32),
                          jnp.broadcast_to(meta_tokens.astype(F32)[None], (bsz, N_META, d)), x_prompt], axis=1)
    conv_p, z_p, bg_p, qd_p, kd_p, vd_p, sga_p, sgb_p, kdb_p, vdb_p = proj(xp.reshape(bsz * lp, d), ATT_KV)
    r3 = lambda a, b_, r_: a.reshape(b_, r_, a.shape[-1])
    oa_p, ssm_p = _gdn(r3(conv_p, bsz, lp), r3(bg_p, bsz, lp), r3(z_p, bsz, lp),
                       jnp.zeros((bsz, H_A, DK_A, DV_A), F32), cw, gn, fp, GDN_CHUNK, 1, 2)
    kbias = jnp.where(jnp.arange(lp) < fp, NEG, 0.0).astype(F32).reshape(lp // ATT_KV, ATT_KV)
    ob_p = _attn_prompt(r3(qd_p, bsz, lp), r3(kdb_p, bsz, lp), r3(vdb_p, bsz, lp), bt_prompt, far_p, kbias,
                        lam4, sn, fp // ATT_BLK, lam_init)

    ns = dbs * dseq
    conv_s, z_s, bg_s, qd_s, kd_s, vd_s, sga_s, sgb_s, _, _ = proj(x_sample.reshape(ns, d), min(ns, 256))
    cs = SUBLANES
    rs = 2 * cs
    fs = rs - dseq

    def chunk_rows(a, head=None):
        a = a.reshape(dbs, dseq, a.shape[-1]).astype(F32)
        parts = [jnp.zeros((dbs, fs - (0 if head is None else head.shape[1]), a.shape[-1]), F32)]
        if head is not None:
            parts.append(head.astype(F32))
        return jnp.concatenate(parts + [a], axis=1)

    oa_s, ssm_s = _gdn(chunk_rows(conv_s, state_conv[l]), chunk_rows(bg_s), chunk_rows(z_s),
                       state_ssm[l].astype(F32), cw, gn, fs, cs, math.gcd(dbs, 4), H_A)
    pad8 = lambda a: jnp.pad(a.reshape(dbs, dseq, a.shape[-1]), ((0, 0), (0, SUBLANES - dseq), (0, 0)))
    n_pool = cache_k.shape[1]
    ck_t = jnp.transpose(cache_k[l], (0, 2, 3, 4, 1)).reshape(n_pool, QB, PAGE)
    cv_i = cache_v[l].reshape(n_pool, PAGE * H_B, DV_B)
    ob_s = _attn_sample(page_table, pad8(qd_s).astype(F32), pad8(kd_s), pad8(vd_s), ck_t, cv_i,
                        last_t, self_t, far_col, lam4, sn, dseq, lam_init)

    wa, wb, wo = w_br_a[l].astype(BF16), w_br_b[l].astype(BF16), w_out[l].astype(BF16)
    n2 = norm2[l].reshape(1, d).astype(F32)
    wr = jnp.pad(w_router[l].astype(F32), ((0, 0), (0, LANES - N_EXPERTS)))
    br = _pad_lanes(b_router[l])
    tiles_b = seq // ROW_TILE
    tiles_lp = lp // ROW_TILE
    skip = (fp + N_META) // ROW_TILE
    x1_p, hm_p, ti_p, tg_p = _post(x_prompt.reshape(bsz * seq, d), oa_p.reshape(bsz * lp, VA),
                                   ob_p.reshape(bsz * lp, VB), sga_p, sgb_p, wa, wb, wo, n2, wr, br, bsz * seq,
                                   lambda i: i, lambda i: (i // tiles_b) * tiles_lp + skip + i % tiles_b)
    assert ns % ROW_TILE == 0
    oa_s2 = oa_s[:, fs:].reshape(ns, VA)
    ob_s2 = ob_s[:, :dseq].reshape(ns, VB)
    x1_s, hm_s, ti_s, tg_s = _post(x_sample.reshape(ns, d), oa_s2, ob_s2, sga_s, sgb_s, wa, wb, wo, n2, wr, br, ns,
                                   lambda i: i, lambda i: i)

    cat = lambda a, b_: jnp.concatenate([a, b_], axis=0)
    y = _moe(cat(hm_p, hm_s), cat(ti_p, ti_s), cat(tg_p, tg_s), cat(x1_p, x1_s),
             w_gu[l], b_gu[l], w_down[l], b_down[l])
    npt = bsz * seq
    y_prompt = y[:npt].reshape(bsz, seq, d)
    y_sample = y[npt:].reshape(dbs, dseq, d)

    k_prompt = r3(kd_p, bsz, lp)[:, fp:].reshape(1, bsz, lreal, H_B, 2, DK_B)
    v_prompt = r3(vd_p, bsz, lp)[:, fp:].reshape(1, bsz, lreal, H_B, DV_B)
    conv_prompt = r3(conv_p, bsz, lp)[:, lp - (CONV_W - 1):][None]
    xpad_s = jnp.concatenate([state_conv[l].astype(F32), conv_s.reshape(dbs, dseq, CONV_DIM)], axis=1)
    conv_sample = xpad_s[:, dseq:][None]
    return (y_prompt, y_sample, k_prompt, v_prompt, ssm_p[None], conv_prompt,
            kd_s.reshape(1, dbs, dseq, H_B, 2, DK_B), vd_s.reshape(1, dbs, dseq, H_B, DV_B), ssm_s[None], conv_sample)
```

```python
import functools
import math

import numpy as np
import jax
import jax.numpy as jnp
from jax import lax
from jax.experimental import pallas as pl
from jax.experimental.pallas import tpu as pltpu

F32 = jnp.float32
BF16 = jnp.bfloat16
HI = lax.Precision.HIGHEST

N_META = 16
H_A, DK_A, DV_A = 4, 128, 128
CONV_W = 4
H_B, DK_B = 4, 64
DV_B = 2 * DK_B
N_BUCKETS, MAX_DIST = 32, 128
N_EXPERTS, TOP_K = 32, 4
SWIGLU_LIMIT, SWIGLU_ALPHA = 7.0, 1.702
EPS = 1e-6
PAGE = 128
QA = H_A * DK_A
VA = H_A * DV_A
QB = H_B * 2 * DK_B
VB = H_B * DV_B
CONV_DIM = 2 * QA + VA

LANES = 128
SUBLANES = 8
VMEM_LIMIT = 56 * 1024 * 1024

GDN_CHUNK = 64
ATT_BLK = 128
ATT_KV = 2 * ATT_BLK
ATT_Q = 2 * ATT_BLK
PAGES_PER_STEP = 8
MOE_BLK = 256
ROW_TILE = 128
NEG = -1e30


def _dot(a, b, prec=None):
    return jnp.dot(a, b, preferred_element_type=F32, precision=prec)


def _dot_nt(a, b, prec=None):
    return lax.dot_general(a, b, (((1,), (1,)), ((), ())), preferred_element_type=F32, precision=prec)


def _dot_tn(a, b, prec=None):
    return lax.dot_general(a, b, (((0,), (0,)), ((), ())), preferred_element_type=F32, precision=prec)


def _split(a):
    hi = a.astype(BF16)
    return hi, (a - hi.astype(F32)).astype(BF16)


def _dot3(a, b):
    return _dot(a[0], b[0]) + _dot(a[1], b[0]) + _dot(a[0], b[1])


def _sigmoid(x):
    return 1.0 / (1.0 + jnp.exp(-x))


def _cparams(sem, flags=None):
    return pltpu.CompilerParams(dimension_semantics=sem, vmem_limit_bytes=VMEM_LIMIT, flags=flags)


def _bucket_np(d):
    d = np.maximum(d, 0)
    df = np.maximum(d, 1).astype(np.float32)
    max_exact = N_BUCKETS // 2
    large = max_exact + (np.log(df / np.float32(max_exact)) / np.float32(math.log(MAX_DIST / max_exact))
                         * np.float32(N_BUCKETS - max_exact)).astype(np.int32)
    return np.where(d < max_exact, d, np.minimum(large, N_BUCKETS - 1)).astype(np.int32)


def _bias_kernel(tab_ref, bk_ref, o_ref):
    for t in range(bk_ref.shape[0]):
        bk = bk_ref[t]
        for h in range(H_B):
            acc = jnp.zeros(bk.shape, F32)
            for b in range(N_BUCKETS):
                acc = jnp.where(bk == b, tab_ref[b, h], acc)
            o_ref[h, t] = acc


def _bias_tiles(table, buckets):
    t, r, _ = buckets.shape
    return pl.pallas_call(
        _bias_kernel,
        out_shape=jax.ShapeDtypeStruct((H_B, t, r, LANES), F32),
        in_specs=[pl.BlockSpec(memory_space=pltpu.SMEM), pl.BlockSpec(memory_space=pltpu.VMEM)],
        out_specs=pl.BlockSpec(memory_space=pltpu.VMEM),
        name="rel_bias_tiles",
    )(table, jnp.asarray(buckets))


def _inproj_kernel(x_ref, n1_ref, w_ref, wba_ref, gm_ref, qn_ref, kn_ref, alog_ref, dtb_ref,
                   conv_ref, z_ref, bg_ref, qd_ref, kd_ref, vd_ref, sga_ref, sgb_ref, kdb_ref, vdb_ref):
    x = x_ref[...]
    ms = jnp.mean(x * x, axis=-1, keepdims=True)
    h = (x * lax.rsqrt(ms + EPS) * n1_ref[...]).astype(BF16)
    o = 0
    conv_ref[...] = _dot(h, w_ref[:, o:o + CONV_DIM]); o += CONV_DIM
    z_ref[...] = _dot(h, w_ref[:, o:o + VA]).astype(z_ref.dtype); o += VA

    def group_norm(y, g):
        sq = y * y
        hi = sq.astype(BF16)
        lo = (sq - hi.astype(F32)).astype(BF16)
        msq = _dot(hi, gm_ref[...]) + _dot(lo, gm_ref[...])
        return y * lax.rsqrt(msq + EPS) * g

    qb = _dot(h, w_ref[:, o:o + QB]); o += QB
    qd_ref[...] = (group_norm(qb, qn_ref[...]) * (DK_B ** -0.5)).astype(qd_ref.dtype)
    kb = _dot(h, w_ref[:, o:o + QB]); o += QB
    kd = group_norm(kb, kn_ref[...])
    kd_ref[...] = kd
    kdb_ref[...] = kd.astype(kdb_ref.dtype)
    vd = _dot(h, w_ref[:, o:o + VB]); o += VB
    vd_ref[...] = vd
    vdb_ref[...] = vd.astype(vdb_ref.dtype)
    d_model = x.shape[1]
    sga_ref[...] = _sigmoid(_dot(h, w_ref[:, o:o + d_model])).astype(sga_ref.dtype); o += d_model
    sgb_ref[...] = _sigmoid(_dot(h, w_ref[:, o:o + d_model])).astype(sgb_ref.dtype)
    t = _dot(h, wba_ref[...])
    lane = lax.broadcasted_iota(jnp.int32, t.shape, 1)
    ta = t + dtb_ref[...]
    sp = jnp.maximum(ta, 0.0) + jnp.log(1.0 + jnp.exp(-jnp.abs(ta)))
    bg_ref[...] = jnp.where(lane < H_A, _sigmoid(t), -jnp.exp(alog_ref[...]) * sp)


def _inproj(x2d, tm, n1, w_main, w_ba, gmat, qn_t, kn_t, alog_p, dtb_p):
    n, d = x2d.shape
    assert n % tm == 0
    row = lambda c: pl.BlockSpec((tm, c), lambda i: (i, 0))
    full = lambda a: pl.BlockSpec(a.shape, lambda i: (0,) * a.ndim)
    outs = [(CONV_DIM, F32), (VA, BF16), (LANES, F32), (QB, BF16), (QB, F32), (VB, F32), (d, BF16), (d, BF16),
            (QB, BF16), (VB, BF16)]
    return pl.pallas_call(
        _inproj_kernel,
        grid=(n // tm,),
        in_specs=[row(d)] + [full(a) for a in (n1, w_main, w_ba, gmat, qn_t, kn_t, alog_p, dtb_p)],
        out_specs=[row(c) for c, _ in outs],
        out_shape=[jax.ShapeDtypeStruct((n, c), dt) for c, dt in outs],
        compiler_params=_cparams(("parallel",)),
        name="in_proj",
    )(x2d, n1, w_main, w_ba, gmat, qn_t, kn_t, alog_p, dtb_p)


def _gdn_kernel(fv, c, q_ref, k_ref, v_ref, cwq_ref, cwk_ref, cwv_ref, bg_ref, z_ref, s0_ref, gn_ref,
                o_ref, st_ref):
    nb, r, wd = q_ref.shape
    hg = wd // LANES
    head0 = pl.program_id(1) * hg
    chains = [(b, hh) for b in range(nb) for hh in range(hg)]

    ri = lax.broadcasted_iota(jnp.int32, (c, c), 0)
    ci = lax.broadcasted_iota(jnp.int32, (c, c), 1)
    incl = ri >= ci
    strict = ri > ci
    eye = (ri == ci).astype(F32)
    lane = lax.broadcasted_iota(jnp.int32, (c, LANES), 1)
    sub_t = lax.broadcasted_iota(jnp.int32, (LANES, c), 0)
    rowid = lax.broadcasted_iota(jnp.int32, (c, 1), 0)
    colid = lax.broadcasted_iota(jnp.int32, (1, c), 1)

    def conv(x_ref, cw_ref, b, cols, r0):
        w = x_ref[b, pl.ds(r0 - SUBLANES, c + SUBLANES), cols]
        acc = w[SUBLANES - 3:SUBLANES - 3 + c] * cw_ref[0:1, cols]
        for j in range(1, CONV_W):
            acc = acc + w[SUBLANES - 3 + j:SUBLANES - 3 + j + c] * cw_ref[j:j + 1, cols]
        return acc * _sigmoid(acc)

    def prepare(b, hh, r0):
        cols = slice(hh * LANES, (hh + 1) * LANES)
        head = head0 + hh
        vcol = ((r0 + rowid) >= fv).astype(F32)
        vrow = ((r0 + colid) >= fv).astype(F32)
        qv = conv(q_ref, cwq_ref, b, cols, r0)
        kv = conv(k_ref, cwk_ref, b, cols, r0)
        v = conv(v_ref, cwv_ref, b, cols, r0) * vcol
        q = qv * lax.rsqrt(jnp.sum(qv * qv, axis=-1, keepdims=True) + EPS) * (DK_A ** -0.5) * vcol
        k = kv * lax.rsqrt(jnp.sum(kv * kv, axis=-1, keepdims=True) + EPS) * vcol
        bgc = bg_ref[b, pl.ds(r0, c), :]
        beta = jnp.sum(jnp.where(lane == head, bgc, 0.0), axis=-1, keepdims=True) * vcol
        g_col = jnp.sum(jnp.where(lane == H_A + head, bgc, 0.0), axis=-1, keepdims=True) * vcol
        g_row = jnp.sum(jnp.where(sub_t == H_A + head, bgc.T, 0.0), axis=0, keepdims=True) * vrow
        gc = jnp.sum(jnp.where(incl, g_row, 0.0), axis=-1, keepdims=True)
        gr = jnp.sum(jnp.where(ri <= ci, g_col, 0.0), axis=0, keepdims=True)
        decay = jnp.where(incl, jnp.exp(jnp.where(incl, gc - gr, 0.0)), 0.0)
        kb = k * beta
        kbh = kb.astype(BF16)
        kh = k.astype(BF16)
        low = jnp.where(strict, _dot_nt(kbh, kh) * decay, 0.0)
        inv = eye - low
        pw = _split(low)
        for _ in range(int(math.log2(c)) - 1):
            pw = _split(_dot3(pw, pw))
            inv = inv + _dot3(_split(inv), pw)
        eg = jnp.exp(gc)
        sol = _dot3(_split(inv), _split(jnp.concatenate([v * beta, kb * eg], axis=-1)))
        w_v, w_k = sol[:, :DV_A], sol[:, DV_A:].astype(BF16)
        intra = (_dot_nt(q.astype(BF16), kh) * decay).astype(BF16)
        g_last = jnp.sum(jnp.where(rowid == c - 1, gc, 0.0), axis=0, keepdims=True)
        k_dt = (k * jnp.exp(g_last - gc)).T.astype(BF16)
        return w_v, w_k, intra, (q * eg).astype(BF16), k_dt, jnp.exp(g_last)

    def apply(b, hh, r0, prepared, s):
        w_v, w_k, intra, q_g, k_dt, eg_last = prepared
        cols = slice(hh * LANES, (hh + 1) * LANES)
        sh = s.astype(BF16)
        u = w_v - _dot(w_k, sh)
        uh = u.astype(BF16)
        o = _dot(q_g, sh) + _dot(intra, uh)
        s = s * eg_last + _dot(k_dt, uh)
        on = o * lax.rsqrt(jnp.mean(o * o, axis=-1, keepdims=True) + EPS) * gn_ref[...]
        zc = z_ref[b, pl.ds(r0, c), cols].astype(F32)
        o_ref[b, pl.ds(r0, c), cols] = (on * (zc * _sigmoid(zc))).astype(o_ref.dtype)
        return s

    def prepare_all(j):
        r0 = pl.multiple_of(j * c, c)
        return tuple(prepare(b, hh, r0) for b, hh in chains)

    def apply_all(j, prepared, states):
        r0 = pl.multiple_of(j * c, c)
        return tuple(apply(b, hh, r0, p, s) for (b, hh), p, s in zip(chains, prepared, states))

    def chunk(j, carry):
        prepared, states = carry
        return prepare_all(j + 1), apply_all(j, prepared, states)

    j0 = fv // c
    last = r // c - 1
    o_ref[:, 0:j0 * c, :] = jnp.zeros((nb, j0 * c, wd), o_ref.dtype)
    init = (prepare_all(j0), tuple(s0_ref[b, hh] for b, hh in chains))
    prepared, states = lax.fori_loop(j0, last, chunk, init)
    for (b, hh), s in zip(chains, apply_all(last, prepared, states)):
        st_ref[b, hh] = s


def _gdn(conv3, bg3, z3, s0, conv_w, gn, fv, c, nb, hg):
    bsz, r, _ = conv3.shape
    assert r % c == 0 and fv // c >= 1 and c >= SUBLANES and bsz % nb == 0 and H_A % hg == 0
    ng = H_A // hg
    wd = hg * LANES
    blk = lambda off: pl.BlockSpec((nb, r, wd), lambda i, g: (i, 0, off * ng + g))
    cw = lambda off: pl.BlockSpec((CONV_W, wd), lambda i, g: (0, off * ng + g))
    st = pl.BlockSpec((nb, hg, DK_A, DV_A), lambda i, g: (i, g, 0, 0))
    return pl.pallas_call(
        functools.partial(_gdn_kernel, fv, c),
        grid=(bsz // nb, ng),
        in_specs=[blk(0), blk(1), blk(2), cw(0), cw(1), cw(2),
                  pl.BlockSpec((nb, r, LANES), lambda i, g: (i, 0, 0)), blk(0), st,
                  pl.BlockSpec((1, LANES), lambda i, g: (0, 0))],
        out_specs=[blk(0), st],
        out_shape=[jax.ShapeDtypeStruct((bsz, r, VA), F32), jax.ShapeDtypeStruct((bsz, H_A, DK_A, DV_A), F32)],
        compiler_params=_cparams(("parallel", "parallel")),
        name="gdn",
    )(conv3, conv3, conv3, conv_w, conv_w, conv_w, bg3, z3, s0, gn)


def _lambda(lam_ref, lam_init):
    l1 = jnp.sum(lam_ref[0:1, :] * lam_ref[1:2, :], axis=-1, keepdims=True)
    l2 = jnp.sum(lam_ref[2:3, :] * lam_ref[3:4, :], axis=-1, keepdims=True)
    return jnp.exp(l1) - jnp.exp(l2) + lam_init


def _sub_norm(o0, o1, lam, sn, lam_init):
    ob = o0 - lam * o1
    return ob * lax.rsqrt(jnp.mean(ob * ob, axis=-1, keepdims=True) + EPS) * sn * (1.0 - lam_init)


def _attn_prompt_kernel(first_q, lam_init, q_ref, k_ref, v_ref, bt_ref, far_ref, kb_ref, lam_ref, sn_ref, o_ref):
    blk = ATT_BLK
    nq, nk = ATT_Q // blk, ATT_KV // blk
    qi = pl.program_id(2)

    @pl.when(qi < first_q)
    def _():
        o_ref[0] = jnp.zeros(o_ref.shape[1:], o_ref.dtype)

    @pl.when(qi >= first_q)
    def _():
        q = q_ref[0]
        lane = lax.broadcasted_iota(jnp.int32, q.shape, 1)
        zero = jnp.zeros_like(q)
        qs = jnp.concatenate([jnp.where(lane < DK_B, q, zero), jnp.where(lane >= DK_B, q, zero)], axis=0)
        diag = bt_ref[0, 0]
        near = bt_ref[0, 1]
        far = far_ref[0]

        def body(j, carry):
            m, l, acc = carry
            k0 = pl.multiple_of(j * ATT_KV, ATT_KV)
            ks = k_ref[0, pl.ds(k0, ATT_KV), :]
            vs = v_ref[0, pl.ds(k0, ATT_KV), :]
            rows = []
            for a in range(nq):
                tiles = []
                for hb in range(nk):
                    rel = (nq * qi + a) - (nk * j + hb)
                    tiles.append(jnp.where(rel == 0, diag, jnp.where(rel == 1, near, jnp.where(rel < 0, NEG, far))))
                rows.append(jnp.concatenate(tiles, axis=1))
            bias = jnp.concatenate(rows, axis=0) + kb_ref[pl.ds(j, 1), :]
            s = _dot_nt(qs, ks) + jnp.concatenate([bias, bias], axis=0)
            m_new = jnp.maximum(m, jnp.max(s, axis=-1, keepdims=True))
            alpha = jnp.exp(m - m_new)
            p = jnp.exp(s - m_new)
            l = alpha * l + jnp.sum(p, axis=-1, keepdims=True)
            acc = alpha * acc + _dot(p.astype(BF16), vs)
            return m_new, l, acc

        init = (jnp.full((2 * ATT_Q, 1), NEG, F32), jnp.zeros((2 * ATT_Q, 1), F32), jnp.zeros((2 * ATT_Q, DV_B), F32))
        _, l, acc = lax.fori_loop(0, (nq * qi + nq - 1) // nk + 1, body, init)
        o = acc / l
        o_ref[0] = _sub_norm(o[:ATT_Q], o[ATT_Q:], _lambda(lam_ref, lam_init), sn_ref[...],
                             lam_init).astype(o_ref.dtype)


def _attn_prompt(qd3, kd3, vd3, btiles, far, kbias, lam4, sn, first_q, lam_init):
    bsz, lp, _ = qd3.shape
    assert lp % ATT_KV == 0 and lp % ATT_Q == 0
    kv = pl.BlockSpec((1, lp, LANES), lambda b, h, i: (b, 0, h))
    qo = pl.BlockSpec((1, ATT_Q, LANES), lambda b, h, i: (b, i, h))
    full = lambda a: pl.BlockSpec(a.shape, lambda b, h, i: (0,) * a.ndim)
    return pl.pallas_call(
        functools.partial(_attn_prompt_kernel, first_q, lam_init),
        grid=(bsz, H_B, lp // ATT_Q),
        in_specs=[qo, kv, kv,
                  pl.BlockSpec((1, 2, ATT_BLK, LANES), lambda b, h, i: (h, 0, 0, 0)),
                  pl.BlockSpec((1, 1, LANES), lambda b, h, i: (h, 0, 0)),
                  full(kbias), full(lam4), full(sn)],
        out_specs=qo,
        out_shape=jax.ShapeDtypeStruct((bsz, lp, VB), F32),
        compiler_params=_cparams(("parallel", "parallel", "parallel")),
        name="diff_attn_prompt",
    )(qd3, kd3, vd3, btiles, far, kbias, lam4, sn)


def _attn_sample_kernel(npp, t_new, lam_init, pt_ref, q_ref, kn_ref, vn_ref, *rest):
    k_refs, v_refs = rest[:npp], rest[npp:2 * npp]
    last_ref, self_ref, far_ref, lam_ref, sn_ref, o_ref, qm, kself, vself, m_s, l_s, acc_s = rest[2 * npp:]
    st = pl.program_id(1)
    nst = pl.num_programs(1)
    nrow = H_B * 2 * SUBLANES
    hrows = 2 * SUBLANES

    def update(s, values_of_head):
        m_old = m_s[...]
        m_new = jnp.maximum(m_old, jnp.max(s, axis=-1, keepdims=True))
        alpha = jnp.exp(m_old - m_new)
        pr = jnp.exp(s - m_new)
        l_s[...] = alpha * l_s[...] + jnp.sum(pr, axis=-1, keepdims=True)
        prb = pr.astype(BF16)
        pv = [_dot(prb[h * hrows:(h + 1) * hrows], values_of_head(h)) for h in range(H_B)]
        acc_s[...] = alpha * acc_s[...] + jnp.concatenate(pv, axis=0)
        m_s[...] = m_new

    @pl.when(st == 0)
    def _():
        q8 = q_ref[0]
        lane = lax.broadcasted_iota(jnp.int32, q8.shape, 1)
        qm[...] = jnp.concatenate([jnp.where(lane // DK_B == g, q8, 0.0) for g in range(2 * H_B)],
                                  axis=0).astype(qm.dtype)
        kself[...] = jnp.zeros(kself.shape, F32)
        vself[...] = jnp.zeros(vself.shape, F32)
        kself[0:SUBLANES, :] = kn_ref[0]
        vself[0:SUBLANES, :] = vn_ref[0]
        m_s[...] = jnp.full(m_s.shape, NEG, F32)
        l_s[...] = jnp.zeros(l_s.shape, F32)
        acc_s[...] = jnp.zeros(acc_s.shape, F32)
        rows = lax.broadcasted_iota(jnp.int32, (nrow, PAGE), 0) % SUBLANES
        cols = lax.broadcasted_iota(jnp.int32, (nrow, PAGE), 1)
        ok = (cols <= rows) & (cols < t_new)
        s = jnp.where(ok, _dot_nt(qm[...], kself[...].astype(BF16)) + self_ref[...], NEG)
        update(s, lambda h: vself[:, h * DV_B:(h + 1) * DV_B].astype(BF16))

    far = far_ref[:, 0:1]
    parts = []
    for g in range(npp):
        bias = far if g < npp - 1 else jnp.where(st == nst - 1, last_ref[...], far)
        parts.append(_dot(qm[...], k_refs[g][0].astype(BF16)) + bias)

    def page_values(h):
        return jnp.concatenate([v_refs[g][0, pl.ds(h, PAGE, stride=H_B), :] for g in range(npp)],
                               axis=0).astype(BF16)

    update(jnp.concatenate(parts, axis=1), page_values)

    @pl.when(st == nst - 1)
    def _():
        o = acc_s[...] / l_s[...]
        lam = _lambda(lam_ref, lam_init)
        for h in range(H_B):
            r0 = h * hrows
            o_ref[0, :, h * DV_B:(h + 1) * DV_B] = _sub_norm(o[r0:r0 + SUBLANES], o[r0 + SUBLANES:r0 + hrows], lam,
                                                             sn_ref[...], lam_init)


def _attn_sample(page_table, q8, kn8, vn8, ck_t, cv_i, last_t, self_t, far_col, lam4, sn, t_new, lam_init):
    bsz, npg = page_table.shape
    npp = math.gcd(PAGES_PER_STEP, npg)
    nrow = H_B * 2 * SUBLANES
    tok = pl.BlockSpec((1, SUBLANES, QB), lambda b, s, pt: (b, 0, 0))
    page = lambda g: pl.BlockSpec((1, QB, PAGE), lambda b, s, pt: (pt[b, s * npp + g], 0, 0))
    full = lambda a: pl.BlockSpec(a.shape, lambda b, s, pt: (0,) * a.ndim)
    return pl.pallas_call(
        functools.partial(_attn_sample_kernel, npp, t_new, lam_init),
        grid_spec=pltpu.PrefetchScalarGridSpec(
            num_scalar_prefetch=1,
            grid=(bsz, npg // npp),
            in_specs=[tok, tok, tok] + [page(g) for g in range(npp)] * 2
                     + [full(last_t), full(self_t), full(far_col), full(lam4), full(sn)],
            out_specs=tok,
            scratch_shapes=[pltpu.VMEM((nrow, QB), BF16), pltpu.VMEM((PAGE, QB), F32), pltpu.VMEM((PAGE, VB), F32),
                            pltpu.VMEM((nrow, 1), F32), pltpu.VMEM((nrow, 1), F32), pltpu.VMEM((nrow, DV_B), F32)],
        ),
        out_shape=jax.ShapeDtypeStruct((bsz, SUBLANES, VB), F32),
        compiler_params=_cparams(("parallel", "arbitrary")),
        name="diff_attn_sample",
    )(page_table, q8, kn8, vn8, *([ck_t] * npp), *([cv_i] * npp), last_t, self_t, far_col, lam4, sn)


def _post_kernel(prompt_tiles, *refs):
    wa_ref, wb_ref, wo_ref, n2_ref, wr_ref, br_ref, x1_ref, hm_ref, ti_ref, tg_ref = refs[10:]
    is_sample = pl.program_id(0) >= prompt_tiles
    x, oa, ob, sga, sgb = (jnp.where(is_sample, s_ref[...], p_ref[...]) for p_ref, s_ref in zip(refs[:5], refs[5:10]))
    ya = _dot(oa.astype(BF16), wa_ref[...])
    yb = _dot(ob.astype(BF16), wb_ref[...])
    merged = sga.astype(F32) * ya + sgb.astype(F32) * yb
    x1 = x + _dot(merged.astype(BF16), wo_ref[...])
    x1_ref[...] = x1
    hm = x1 * lax.rsqrt(jnp.mean(x1 * x1, axis=-1, keepdims=True) + EPS) * n2_ref[...]
    hm_ref[...] = hm
    logits = _dot(hm, wr_ref[...], HI) + br_ref[...]
    lane = lax.broadcasted_iota(jnp.int32, logits.shape, 1)
    lane_f = lane.astype(F32)
    work = jnp.where(lane < N_EXPERTS, logits, -jnp.inf)
    ti = jnp.zeros(logits.shape, F32)
    tg = jnp.zeros(logits.shape, F32)
    v0 = None
    den = None
    for k in range(TOP_K):
        vk = jnp.max(work, axis=-1, keepdims=True)
        ik = jnp.min(jnp.where(work == vk, lane_f, float(LANES)), axis=-1, keepdims=True)
        work = jnp.where(lane_f == ik, -jnp.inf, work)
        if k == 0:
            v0 = vk
        ek = jnp.exp(vk - v0)
        den = ek if k == 0 else den + ek
        ti = jnp.where(lane == k, ik, ti)
        tg = jnp.where(lane == k, ek, tg)
    ti_ref[...] = ti.astype(jnp.int32)
    tg_ref[...] = tg / den


def _post(prompt_in, sample_in, dense, n_prompt, n_sample, prompt_map):
    d = prompt_in[0].shape[1]
    tm = ROW_TILE
    tp = n_prompt // tm
    n_total = n_prompt + n_sample
    widths = (d, VA, VB, d, d)
    full = lambda a: pl.BlockSpec(a.shape, lambda i: (0,) * a.ndim)
    p_spec = lambda k, c: pl.BlockSpec((tm, c), lambda i: ((jnp.minimum(i, tp - 1) if k == 0
                                                             else prompt_map(jnp.minimum(i, tp - 1))), 0))
    s_spec = lambda c: pl.BlockSpec((tm, c), lambda i: (jnp.maximum(i - tp, 0), 0))
    rout = lambda c: pl.BlockSpec((tm, c), lambda i: (i, 0))
    return pl.pallas_call(
        functools.partial(_post_kernel, tp),
        grid=(n_total // tm,),
        in_specs=[p_spec(k, c) for k, c in enumerate(widths)] + [s_spec(c) for c in widths]
                 + [full(a) for a in dense],
        out_specs=[rout(d), rout(d), rout(LANES), rout(LANES)],
        out_shape=[jax.ShapeDtypeStruct((n_total, d), F32), jax.ShapeDtypeStruct((n_total, d), F32),
                   jax.ShapeDtypeStruct((n_total, LANES), jnp.int32), jax.ShapeDtypeStruct((n_total, LANES), F32)],
        compiler_params=_cparams(("parallel",)),
        name="merge_outproj_router",
    )(*prompt_in, *sample_in, *dense)


def _rank_kernel(ti_ref, rank_ref, cnt_ref, base):
    i = pl.program_id(0)
    tm = ti_ref.shape[0]

    @pl.when(i == 0)
    def _():
        base[...] = jnp.zeros(base.shape, F32)

    ti = ti_ref[...].astype(F32)
    lane = lax.broadcasted_iota(jnp.int32, ti.shape, 1)
    lane_f = lane.astype(F32)
    sel = [jnp.sum(jnp.where(lane == k, ti, 0.0), axis=-1, keepdims=True) for k in range(TOP_K)]
    onehot = jnp.zeros(ti.shape, F32)
    for k in range(TOP_K):
        onehot = onehot + (lane_f == sel[k]).astype(F32)
    ri = lax.broadcasted_iota(jnp.int32, (tm, tm), 0)
    ci = lax.broadcasted_iota(jnp.int32, (tm, tm), 1)
    before = _dot((ri > ci).astype(BF16), onehot.astype(BF16)) + base[...]
    rank = jnp.zeros(ti.shape, jnp.int32)
    for k in range(TOP_K):
        rk = jnp.sum(jnp.where(lane_f == sel[k], before, 0.0), axis=-1, keepdims=True)
        rank = jnp.where(lane == k, rk.astype(jnp.int32), rank)
    rank_ref[...] = rank
    base[...] = base[...] + jnp.sum(onehot, axis=0, keepdims=True)
    cnt_ref[...] = base[...]


def _rank(ti):
    n = ti.shape[0]
    tm = ROW_TILE
    return pl.pallas_call(
        _rank_kernel,
        grid=(n // tm,),
        in_specs=[pl.BlockSpec((tm, LANES), lambda i: (i, 0))],
        out_specs=[pl.BlockSpec((tm, LANES), lambda i: (i, 0)), pl.BlockSpec((1, LANES), lambda i: (0, 0))],
        out_shape=[jax.ShapeDtypeStruct((n, LANES), jnp.int32), jax.ShapeDtypeStruct((1, LANES), F32)],
        scratch_shapes=[pltpu.VMEM((1, LANES), F32)],
        compiler_params=_cparams(("arbitrary",)),
        name="moe_rank",
    )(ti)


def _dispatch_kernel(dest_ref, hm_ref, xs_in, xs_out, sem):
    del xs_in
    tm = hm_ref.shape[0]

    def row_copy(r, d):
        return pltpu.make_async_copy(hm_ref.at[pl.ds(r, 1)], xs_out.at[pl.ds(d, 1)], sem)

    def issue(r, c):
        for k in range(TOP_K):
            row_copy(r, dest_ref[0, 0, r * TOP_K + k]).start(priority=k % 2)
        return c

    def drain(r, c):
        for k in range(TOP_K):
            row_copy(r, dest_ref[0, 0, r * TOP_K + k]).wait()
        return c

    lax.fori_loop(0, tm, issue, 0)
    lax.fori_loop(0, tm, drain, 0)


def _dispatch(dest3, hm, xs_zero):
    n, d = hm.shape
    tm = ROW_TILE
    return pl.pallas_call(
        _dispatch_kernel,
        grid=(n // tm,),
        in_specs=[pl.BlockSpec((1, 1, tm * TOP_K), lambda i: (i, 0, 0), memory_space=pltpu.SMEM),
                  pl.BlockSpec((tm, d), lambda i: (i, 0)),
                  pl.BlockSpec(memory_space=pl.ANY)],
        out_specs=pl.BlockSpec(memory_space=pl.ANY),
        out_shape=jax.ShapeDtypeStruct(xs_zero.shape, xs_zero.dtype),
        scratch_shapes=[pltpu.SemaphoreType.DMA],
        input_output_aliases={2: 0},
        compiler_params=_cparams(("arbitrary",)),
        name="moe_dispatch",
    )(dest3, hm, xs_zero)


def _expert_kernel(be_ref, bv_ref, xs_ref, wgu_ref, bgu_ref, wd_ref, bd_ref, y_ref, wgu_bf, wd_bf):
    i = pl.program_id(0)
    d_ff = wd_ref.shape[1]

    @pl.when((i == 0) | (be_ref[i] != be_ref[jnp.maximum(i - 1, 0)]))
    def _():
        wgu_bf[...] = wgu_ref[0].astype(BF16)
        wd_bf[...] = wd_ref[0].astype(BF16)

    @pl.when(bv_ref[i] > 0)
    def _():
        x = xs_ref[...].astype(BF16)
        hgu = _dot(x, wgu_bf[...]) + bgu_ref[0]
        gate = jnp.minimum(hgu[:, :d_ff], SWIGLU_LIMIT)
        up = jnp.clip(hgu[:, d_ff:], -SWIGLU_LIMIT, SWIGLU_LIMIT)
        act = (up + 1.0) * gate * _sigmoid(SWIGLU_ALPHA * gate)
        y_ref[...] = _dot(act.astype(BF16), wd_bf[...]) + bd_ref[0]

    @pl.when(bv_ref[i] == 0)
    def _():
        y_ref[...] = jnp.zeros(y_ref.shape, y_ref.dtype)


def _experts(block_expert, block_valid, xs, w_gu, b_gu3, w_down, b_down3):
    n_slots, d = xs.shape
    n_blocks = n_slots // MOE_BLK
    d_ff = w_down.shape[1]
    return pl.pallas_call(
        _expert_kernel,
        grid_spec=pltpu.PrefetchScalarGridSpec(
            num_scalar_prefetch=2,
            grid=(n_blocks,),
            in_specs=[pl.BlockSpec((MOE_BLK, d), lambda i, be, bv: (i, 0)),
                      pl.BlockSpec((1, d, 2 * d_ff), lambda i, be, bv: (be[i], 0, 0)),
                      pl.BlockSpec((1, 1, 2 * d_ff), lambda i, be, bv: (be[i], 0, 0)),
                      pl.BlockSpec((1, d_ff, d), lambda i, be, bv: (be[i], 0, 0)),
                      pl.BlockSpec((1, 1, d), lambda i, be, bv: (be[i], 0, 0))],
            out_specs=pl.BlockSpec((MOE_BLK, d), lambda i, be, bv: (i, 0)),
            scratch_shapes=[pltpu.VMEM((d, 2 * d_ff), BF16), pltpu.VMEM((d_ff, d), BF16)],
        ),
        out_shape=jax.ShapeDtypeStruct((n_slots, d), F32),
        compiler_params=_cparams(("arbitrary",)),
        name="moe_experts",
    )(block_expert, block_valid, xs, w_gu, b_gu3, w_down, b_down3)


def _combine_kernel(dest_ref, yb_ref, tg_ref, x1_ref, y_ref, buf, sem):
    tm = x1_ref.shape[0]

    def row_copy(r, k, d):
        return pltpu.make_async_copy(yb_ref.at[pl.ds(d, 1)], buf.at[k, pl.ds(r, 1)], sem)

    def issue(r, c):
        for k in range(TOP_K):
            row_copy(r, k, dest_ref[0, 0, r * TOP_K + k]).start(priority=k % 2)
        return c

    def drain(r, c):
        for k in range(TOP_K):
            row_copy(r, k, dest_ref[0, 0, r * TOP_K + k]).wait()
        return c

    lax.fori_loop(0, tm, issue, 0)
    lax.fori_loop(0, tm, drain, 0)
    tg = tg_ref[...]
    lane = lax.broadcasted_iota(jnp.int32, tg.shape, 1)
    acc = jnp.zeros(x1_ref.shape, F32)
    for k in range(TOP_K):
        gk = jnp.sum(jnp.where(lane == k, tg, 0.0), axis=-1, keepdims=True)
        acc = acc + gk * buf[k]
    y_ref[...] = x1_ref[...] + acc


def _combine(dest3, yb, tg, x1):
    n, d = x1.shape
    tm = ROW_TILE
    return pl.pallas_call(
        _combine_kernel,
        grid=(n // tm,),
        in_specs=[pl.BlockSpec((1, 1, tm * TOP_K), lambda i: (i, 0, 0), memory_space=pltpu.SMEM),
                  pl.BlockSpec(memory_space=pl.ANY),
                  pl.BlockSpec((tm, LANES), lambda i: (i, 0)),
                  pl.BlockSpec((tm, d), lambda i: (i, 0))],
        out_specs=pl.BlockSpec((tm, d), lambda i: (i, 0)),
        out_shape=jax.ShapeDtypeStruct((n, d), F32),
        scratch_shapes=[pltpu.VMEM((TOP_K, tm, d), F32), pltpu.SemaphoreType.DMA],
        compiler_params=_cparams(("arbitrary",)),
        name="moe_combine",
    )(dest3, yb, tg, x1)


def _moe(hm, ti, tg, x1, w_gu, b_gu, w_down, b_down):
    n, d = hm.shape
    rank, cnt = _rank(ti)
    counts = cnt[0, :N_EXPERTS].astype(jnp.int32)
    padded = (counts + MOE_BLK - 1) // MOE_BLK * MOE_BLK
    pad_end = jnp.cumsum(padded)
    pad_start = pad_end - padded
    top_i = ti[:, :TOP_K]
    dest = (pad_start[top_i] + rank[:, :TOP_K]).astype(jnp.int32)
    dest3 = dest.reshape(n // ROW_TILE, 1, ROW_TILE * TOP_K)
    n_blocks = (n * TOP_K) // MOE_BLK + N_EXPERTS
    blk_start = jnp.arange(n_blocks, dtype=jnp.int32) * MOE_BLK
    block_expert = jnp.minimum(jnp.sum((pad_end[None, :] <= blk_start[:, None]).astype(jnp.int32), axis=1),
                               N_EXPERTS - 1)
    block_valid = (blk_start < pad_end[-1]).astype(jnp.int32)
    xs = _dispatch(dest3, hm, jnp.zeros((n_blocks * MOE_BLK, d), F32))
    yb = _experts(block_expert, block_valid, xs, w_gu, b_gu.reshape(N_EXPERTS, 1, -1), w_down,
                  b_down.reshape(N_EXPERTS, 1, -1))
    return _combine(dest3, yb, tg, x1)


def _pad_lanes(v, fill=0.0):
    v = v.reshape(1, -1).astype(F32)
    return jnp.pad(v, ((0, 0), (0, LANES - v.shape[1])), constant_values=fill)


def kernel(x_prompt, x_sample, cache_k, cache_v, state_ssm, state_conv, page_table, meta_tokens, rel_bias_table, norm1, w_in, conv_w, a_log, dt_bias, gdn_norm, q_norm, k_norm, lam_q1, lam_k1, lam_q2, lam_k2, sub_norm, w_br_a, w_br_b, w_out, norm2, w_router, b_router, w_gu, b_gu, w_down, b_down):
    bsz, seq, d = x_prompt.shape
    dbs, dseq, _ = x_sample.shape
    depth = w_in.shape[0]
    assert depth == 1 and dseq <= SUBLANES - (CONV_W - 1) and seq % ATT_BLK == 0
    lam_init = 0.8 - 0.6 * math.exp(-0.3 * 0)
    l = 0

    w = w_in[l]
    o_z = CONV_DIM + VA
    w_main = jnp.concatenate([w[:, :o_z], w[:, o_z + 2 * H_A:]], axis=1).astype(BF16)
    w_ba = jnp.pad(w[:, o_z:o_z + 2 * H_A], ((0, 0), (0, LANES - 2 * H_A))).astype(BF16)
    grp = np.arange(QB) // DK_B
    gmat = jnp.asarray((grp[:, None] == grp[None, :]).astype(np.float32) / DK_B, dtype=BF16)
    qn_t = jnp.tile(q_norm[l].astype(F32), QB // DK_B).reshape(1, QB)
    kn_t = jnp.tile(k_norm[l].astype(F32), QB // DK_B).reshape(1, QB)
    alog_p = jnp.pad(a_log[l].astype(F32), (H_A, LANES - 2 * H_A)).reshape(1, LANES)
    dtb_p = jnp.pad(dt_bias[l].astype(F32), (H_A, LANES - 2 * H_A)).reshape(1, LANES)
    n1 = norm1[l].reshape(1, d).astype(F32)
    proj = functools.partial(_inproj, n1=n1, w_main=w_main, w_ba=w_ba, gmat=gmat, qn_t=qn_t, kn_t=kn_t,
                             alog_p=alog_p, dtb_p=dtb_p)
    lam4 = jnp.stack([lam_q1[l], lam_k1[l], lam_q2[l], lam_k2[l]]).astype(F32)
    sn = sub_norm[l].reshape(1, DV_B).astype(F32)
    gn = gdn_norm[l].reshape(1, DV_A).astype(F32)
    cw = conv_w[l].astype(F32)

    ii = np.arange(ATT_BLK)[:, None]
    jj = np.arange(LANES)[None, :]
    bk_prompt = np.stack([_bucket_np(ii - jj), _bucket_np(ATT_BLK + ii - jj)])
    t8 = np.arange(SUBLANES)[:, None]
    bk_sample = np.stack([_bucket_np(PAGE + t8 - jj), _bucket_np(t8 - jj)])
    bt_prompt = _bias_tiles(rel_bias_table.astype(F32), bk_prompt)
    bt_prompt = bt_prompt.at[:, 0].set(jnp.where(jnp.asarray(ii >= jj), bt_prompt[:, 0], NEG))
    bt_sample = _bias_tiles(rel_bias_table.astype(F32), bk_sample)
    far_h = rel_bias_table[N_BUCKETS - 1].astype(F32)
    far_p = jnp.broadcast_to(far_h[:, None, None], (H_B, 1, LANES))
    nrow = H_B * 2 * SUBLANES
    rows_of = lambda t: jnp.broadcast_to(t[:, None], (H_B, 2, SUBLANES, LANES)).reshape(nrow, LANES)
    last_t, self_t = rows_of(bt_sample[:, 0]), rows_of(bt_sample[:, 1])
    far_col = jnp.broadcast_to(far_h[:, None, None], (H_B, 2 * SUBLANES, LANES)).reshape(nrow, LANES)

    lreal = N_META + seq
    lp = -(-lreal // ATT_KV) * ATT_KV
    fp = lp - lreal
    assert fp % SUBLANES == 0 and fp >= GDN_CHUNK and (fp + N_META) % ATT_BLK == 0 and lp % GDN_CHUNK == 0
    xp = jnp.concatenate([jnp.zeros((bsz, fp, d), F32),
                          jnp.broadcast_to(meta_tokens.astype(F32)[None], (bsz, N_META, d)), x_prompt], axis=1)
    conv_p, z_p, bg_p, qd_p, kd_p, vd_p, sga_p, sgb_p, kdb_p, vdb_p = proj(xp.reshape(bsz * lp, d), ATT_KV)
    r3 = lambda a, b_, r_: a.reshape(b_, r_, a.shape[-1])
    oa_p, ssm_p = _gdn(r3(conv_p, bsz, lp), r3(bg_p, bsz, lp), r3(z_p, bsz, lp),
                       jnp.zeros((bsz, H_A, DK_A, DV_A), F32), cw, gn, fp, GDN_CHUNK, 1, 2)
    kbias = jnp.where(jnp.arange(lp) < fp, NEG, 0.0).astype(F32).reshape(lp // ATT_KV, ATT_KV)
    ob_p = _attn_prompt(r3(qd_p, bsz, lp), r3(kdb_p, bsz, lp), r3(vdb_p, bsz, lp), bt_prompt, far_p, kbias,
                        lam4, sn, fp // ATT_Q, lam_init)

    ns = dbs * dseq
    conv_s, z_s, bg_s, qd_s, kd_s, vd_s, sga_s, sgb_s, _, _ = proj(x_sample.reshape(ns, d), min(ns, 256))
    cs = SUBLANES
    rs = 2 * cs
    fs = rs - dseq

    def chunk_rows(a, head=None):
        a = a.reshape(dbs, dseq, a.shape[-1]).astype(F32)
        parts = [jnp.zeros((dbs, fs - (0 if head is None else head.shape[1]), a.shape[-1]), F32)]
        if head is not None:
            parts.append(head.astype(F32))
        return jnp.concatenate(parts + [a], axis=1)

    oa_s, ssm_s = _gdn(chunk_rows(conv_s, state_conv[l]), chunk_rows(bg_s), chunk_rows(z_s),
                       state_ssm[l].astype(F32), cw, gn, fs, cs, math.gcd(dbs, 4), H_A)
    pad8 = lambda a: jnp.pad(a.reshape(dbs, dseq, a.shape[-1]), ((0, 0), (0, SUBLANES - dseq), (0, 0)))
    n_pool = cache_k.shape[1]
    ck_t = jnp.transpose(cache_k[l], (0, 2, 3, 4, 1)).reshape(n_pool, QB, PAGE)
    cv_i = cache_v[l].reshape(n_pool, PAGE * H_B, DV_B)
    ob_s = _attn_sample(page_table, pad8(qd_s).astype(F32), pad8(kd_s), pad8(vd_s), ck_t, cv_i,
                        last_t, self_t, far_col, lam4, sn, dseq, lam_init)

    wa, wb, wo = w_br_a[l].astype(BF16), w_br_b[l].astype(BF16), w_out[l].astype(BF16)
    n2 = norm2[l].reshape(1, d).astype(F32)
    wr = jnp.pad(w_router[l].astype(F32), ((0, 0), (0, LANES - N_EXPERTS)))
    br = _pad_lanes(b_router[l])
    tiles_b = seq // ROW_TILE
    tiles_lp = lp // ROW_TILE
    skip = (fp + N_META) // ROW_TILE
    npt = bsz * seq
    assert ns % ROW_TILE == 0 and npt % ROW_TILE == 0
    oa_s2 = oa_s[:, fs:].reshape(ns, VA)
    ob_s2 = ob_s[:, :dseq].reshape(ns, VB)
    x1, hm, ti, tg = _post(
        (x_prompt.reshape(npt, d), oa_p.reshape(bsz * lp, VA), ob_p.reshape(bsz * lp, VB), sga_p, sgb_p),
        (x_sample.reshape(ns, d), oa_s2, ob_s2, sga_s, sgb_s), (wa, wb, wo, n2, wr, br), npt, ns,
        lambda i: (i // tiles_b) * tiles_lp + skip + i % tiles_b)

    y = _moe(hm, ti, tg, x1, w_gu[l], b_gu[l], w_down[l], b_down[l])
    y_prompt = y[:npt].reshape(bsz, seq, d)
    y_sample = y[npt:].reshape(dbs, dseq, d)

    k_prompt = r3(kd_p, bsz, lp)[:, fp:].reshape(1, bsz, lreal, H_B, 2, DK_B)
    v_prompt = r3(vd_p, bsz, lp)[:, fp:].reshape(1, bsz, lreal, H_B, DV_B)
    conv_prompt = r3(conv_p, bsz, lp)[:, lp - (CONV_W - 1):][None]
    xpad_s = jnp.concatenate([state_conv[l].astype(F32), conv_s.reshape(dbs, dseq, CONV_DIM)], axis=1)
    conv_sample = xpad_s[:, dseq:][None]
    return (y_prompt, y_sample, k_prompt, v_prompt, ssm_p[None], conv_prompt,
            kd_s.reshape(1, dbs, dseq, H_B, 2, DK_B), vd_s.reshape(1, dbs, dseq, H_B, DV_B), ssm_s[None], conv_sample)
```

```python
import functools
import math

import numpy as np
import jax
import jax.numpy as jnp
from jax import lax
from jax.experimental import pallas as pl
from jax.experimental.pallas import tpu as pltpu

F32 = jnp.float32
BF16 = jnp.bfloat16
HI = lax.Precision.HIGHEST

N_META = 16
H_A, DK_A, DV_A = 4, 128, 128
CONV_W = 4
H_B, DK_B = 4, 64
DV_B = 2 * DK_B
N_BUCKETS, MAX_DIST = 32, 128
N_EXPERTS, TOP_K = 32, 4
SWIGLU_LIMIT, SWIGLU_ALPHA = 7.0, 1.702
EPS = 1e-6
PAGE = 128
QA = H_A * DK_A
VA = H_A * DV_A
QB = H_B * 2 * DK_B
VB = H_B * DV_B
CONV_DIM = 2 * QA + VA

LANES = 128
SUBLANES = 8
VMEM_LIMIT = 56 * 1024 * 1024

GDN_CHUNK = 64
ATT_BLK = 128
ATT_KV = 2 * ATT_BLK
ATT_Q = 2 * ATT_BLK
PAGES_PER_STEP = 16
MOE_BLK = 256
ROW_TILE = 128
NEG = -1e30


def _dot(a, b, prec=None):
    return jnp.dot(a, b, preferred_element_type=F32, precision=prec)


def _dot_nt(a, b, prec=None):
    return lax.dot_general(a, b, (((1,), (1,)), ((), ())), preferred_element_type=F32, precision=prec)


def _dot_tn(a, b, prec=None):
    return lax.dot_general(a, b, (((0,), (0,)), ((), ())), preferred_element_type=F32, precision=prec)


def _split(a):
    hi = a.astype(BF16)
    return hi, (a - hi.astype(F32)).astype(BF16)


def _lhs3(a):
    hi, lo = _split(a)
    return jnp.concatenate([hi, lo, hi], axis=1)


def _rhs3(b):
    hi, lo = _split(b)
    return jnp.concatenate([hi, hi, lo], axis=0)


def _sigmoid(x):
    return 1.0 / (1.0 + jnp.exp(-x))


def _cparams(sem, flags=None):
    return pltpu.CompilerParams(dimension_semantics=sem, vmem_limit_bytes=VMEM_LIMIT, flags=flags)


def _bucket_np(d):
    d = np.maximum(d, 0)
    df = np.maximum(d, 1).astype(np.float32)
    max_exact = N_BUCKETS // 2
    large = max_exact + (np.log(df / np.float32(max_exact)) / np.float32(math.log(MAX_DIST / max_exact))
                         * np.float32(N_BUCKETS - max_exact)).astype(np.int32)
    return np.where(d < max_exact, d, np.minimum(large, N_BUCKETS - 1)).astype(np.int32)


def _bias_kernel(tab_ref, bk_ref, o_ref):
    for t in range(bk_ref.shape[0]):
        bk = bk_ref[t]
        for h in range(H_B):
            acc = jnp.zeros(bk.shape, F32)
            for b in range(N_BUCKETS):
                acc = jnp.where(bk == b, tab_ref[b, h], acc)
            o_ref[h, t] = acc


def _bias_tiles(table, buckets):
    t, r, _ = buckets.shape
    return pl.pallas_call(
        _bias_kernel,
        out_shape=jax.ShapeDtypeStruct((H_B, t, r, LANES), F32),
        in_specs=[pl.BlockSpec(memory_space=pltpu.SMEM), pl.BlockSpec(memory_space=pltpu.VMEM)],
        out_specs=pl.BlockSpec(memory_space=pltpu.VMEM),
        name="rel_bias_tiles",
    )(table, jnp.asarray(buckets))


def _inproj_kernel(x_ref, n1_ref, w_ref, wba_ref, gm_ref, qn_ref, kn_ref, alog_ref, dtb_ref,
                   conv_ref, z_ref, bg_ref, qd_ref, kd_ref, vd_ref, sga_ref, sgb_ref, kdb_ref, vdb_ref):
    x = x_ref[...]
    ms = jnp.mean(x * x, axis=-1, keepdims=True)
    h = (x * lax.rsqrt(ms + EPS) * n1_ref[...]).astype(BF16)
    o = 0
    conv_ref[...] = _dot(h, w_ref[:, o:o + CONV_DIM]); o += CONV_DIM
    z_ref[...] = _dot(h, w_ref[:, o:o + VA]).astype(z_ref.dtype); o += VA

    def group_norm(y, g):
        sq = y * y
        hi = sq.astype(BF16)
        lo = (sq - hi.astype(F32)).astype(BF16)
        msq = _dot(hi, gm_ref[...]) + _dot(lo, gm_ref[...])
        return y * lax.rsqrt(msq + EPS) * g

    qb = _dot(h, w_ref[:, o:o + QB]); o += QB
    qd_ref[...] = (group_norm(qb, qn_ref[...]) * (DK_B ** -0.5)).astype(qd_ref.dtype)
    kb = _dot(h, w_ref[:, o:o + QB]); o += QB
    kd = group_norm(kb, kn_ref[...])
    kd_ref[...] = kd
    kdb_ref[...] = kd.astype(kdb_ref.dtype)
    vd = _dot(h, w_ref[:, o:o + VB]); o += VB
    vd_ref[...] = vd
    vdb_ref[...] = vd.astype(vdb_ref.dtype)
    d_model = x.shape[1]
    sga_ref[...] = _sigmoid(_dot(h, w_ref[:, o:o + d_model])).astype(sga_ref.dtype); o += d_model
    sgb_ref[...] = _sigmoid(_dot(h, w_ref[:, o:o + d_model])).astype(sgb_ref.dtype)
    t = _dot(h, wba_ref[...])
    lane = lax.broadcasted_iota(jnp.int32, t.shape, 1)
    ta = t + dtb_ref[...]
    sp = jnp.maximum(ta, 0.0) + jnp.log(1.0 + jnp.exp(-jnp.abs(ta)))
    bg_ref[...] = jnp.where(lane < H_A, _sigmoid(t), -jnp.exp(alog_ref[...]) * sp)


def _inproj(x2d, tm, n1, w_main, w_ba, gmat, qn_t, kn_t, alog_p, dtb_p):
    n, d = x2d.shape
    assert n % tm == 0
    row = lambda c: pl.BlockSpec((tm, c), lambda i: (i, 0))
    full = lambda a: pl.BlockSpec(a.shape, lambda i: (0,) * a.ndim)
    outs = [(CONV_DIM, F32), (VA, BF16), (LANES, F32), (QB, BF16), (QB, F32), (VB, F32), (d, BF16), (d, BF16),
            (QB, BF16), (VB, BF16)]
    return pl.pallas_call(
        _inproj_kernel,
        grid=(n // tm,),
        in_specs=[row(d)] + [full(a) for a in (n1, w_main, w_ba, gmat, qn_t, kn_t, alog_p, dtb_p)],
        out_specs=[row(c) for c, _ in outs],
        out_shape=[jax.ShapeDtypeStruct((n, c), dt) for c, dt in outs],
        compiler_params=_cparams(("parallel",)),
        name="in_proj",
    )(x2d, n1, w_main, w_ba, gmat, qn_t, kn_t, alog_p, dtb_p)


def _gdn_kernel(fv, c, q_ref, k_ref, v_ref, cwq_ref, cwk_ref, cwv_ref, bg_ref, z_ref, s0_ref, gn_ref,
                o_ref, st_ref):
    nb, r, wd = q_ref.shape
    hg = wd // LANES
    head0 = pl.program_id(1) * hg
    chains = [(b, hh) for b in range(nb) for hh in range(hg)]

    ri = lax.broadcasted_iota(jnp.int32, (c, c), 0)
    ci = lax.broadcasted_iota(jnp.int32, (c, c), 1)
    incl = ri >= ci
    strict = ri > ci
    eye = (ri == ci).astype(F32)
    lane = lax.broadcasted_iota(jnp.int32, (c, LANES), 1)
    sub_t = lax.broadcasted_iota(jnp.int32, (LANES, c), 0)
    rowid = lax.broadcasted_iota(jnp.int32, (c, 1), 0)
    colid = lax.broadcasted_iota(jnp.int32, (1, c), 1)

    def conv(x_ref, cw_ref, b, cols, r0):
        w = x_ref[b, pl.ds(r0 - SUBLANES, c + SUBLANES), cols]
        acc = w[SUBLANES - 3:SUBLANES - 3 + c] * cw_ref[0:1, cols]
        for j in range(1, CONV_W):
            acc = acc + w[SUBLANES - 3 + j:SUBLANES - 3 + j + c] * cw_ref[j:j + 1, cols]
        return acc * _sigmoid(acc)

    def elementwise(b, hh, r0):
        cols = slice(hh * LANES, (hh + 1) * LANES)
        head = head0 + hh
        vcol = ((r0 + rowid) >= fv).astype(F32)
        vrow = ((r0 + colid) >= fv).astype(F32)
        qv = conv(q_ref, cwq_ref, b, cols, r0)
        kv = conv(k_ref, cwk_ref, b, cols, r0)
        v = conv(v_ref, cwv_ref, b, cols, r0) * vcol
        q = qv * lax.rsqrt(jnp.sum(qv * qv, axis=-1, keepdims=True) + EPS) * (DK_A ** -0.5) * vcol
        k = kv * lax.rsqrt(jnp.sum(kv * kv, axis=-1, keepdims=True) + EPS) * vcol
        bgc = bg_ref[b, pl.ds(r0, c), :]
        beta = jnp.sum(jnp.where(lane == head, bgc, 0.0), axis=-1, keepdims=True) * vcol
        g_col = jnp.sum(jnp.where(lane == H_A + head, bgc, 0.0), axis=-1, keepdims=True) * vcol
        g_row = jnp.sum(jnp.where(sub_t == H_A + head, bgc.T, 0.0), axis=0, keepdims=True) * vrow
        gc = jnp.sum(jnp.where(incl, g_row, 0.0), axis=-1, keepdims=True)
        gr = jnp.sum(jnp.where(ri <= ci, g_col, 0.0), axis=0, keepdims=True)
        decay = jnp.where(incl, jnp.exp(jnp.where(incl, gc - gr, 0.0)), 0.0)
        kb = k * beta
        eg = jnp.exp(gc)
        g_last = jnp.sum(jnp.where(rowid == c - 1, gc, 0.0), axis=0, keepdims=True)
        k_dt = (k * jnp.exp(g_last - gc)).T
        lhs = jnp.concatenate([kb, q], axis=0).astype(BF16)
        rhs = jnp.concatenate([v * beta, kb * eg], axis=-1)
        return lhs, k.astype(BF16), decay, rhs, q * eg, k_dt, jnp.exp(g_last)

    def prepare_a(j):
        r0 = pl.multiple_of(j * c, c)
        vec = [elementwise(b, hh, r0) for b, hh in chains]
        kk = [_dot_nt(v_[0], v_[1]) for v_ in vec]
        return vec, kk

    def prepare_b(vec, kk):
        low = [jnp.where(strict, kk_[:c] * v_[2], 0.0) for v_, kk_ in zip(vec, kk)]
        intra = [kk_[c:] * v_[2] for v_, kk_ in zip(vec, kk)]
        inv = [eye - lw for lw in low]
        levels = int(math.log2(c)) - 1
        pw = [_dot(_lhs3(lw), _rhs3(lw)) for lw in low]
        for lev in range(levels):
            pw_r = [_rhs3(p) for p in pw]
            if lev + 1 < levels:
                pw = [_dot(_lhs3(p), r_) for p, r_ in zip(pw, pw_r)]
            inv = [iv + _dot(_lhs3(iv), r_) for iv, r_ in zip(inv, pw_r)]
        sol = [_dot(_lhs3(iv), _rhs3(v_[3])) for iv, v_ in zip(inv, vec)]
        out = []
        for v_, sl, it in zip(vec, sol, intra):
            on_state = jnp.concatenate([sl[:, DV_A:], v_[4]], axis=0).astype(BF16)
            on_u = jnp.concatenate([it, v_[5]], axis=0).astype(BF16)
            out.append((sl[:, :DV_A], on_state, on_u, v_[6]))
        return tuple(out)

    def apply_a(prepared, states):
        return [_dot(p[1], s.astype(BF16)) for p, s in zip(prepared, states)]

    def apply_b(j, prepared, states, ps):
        r0 = pl.multiple_of(j * c, c)
        u = [p[0] - ps_[:c] for p, ps_ in zip(prepared, ps)]
        pu = [_dot(p[2], u_.astype(BF16)) for p, u_ in zip(prepared, u)]
        new_states = []
        for (b, hh), p, s, ps_, pu_ in zip(chains, prepared, states, ps, pu):
            cols = slice(hh * LANES, (hh + 1) * LANES)
            o = ps_[c:] + pu_[:c]
            on = o * lax.rsqrt(jnp.mean(o * o, axis=-1, keepdims=True) + EPS) * gn_ref[...]
            zc = z_ref[b, pl.ds(r0, c), cols].astype(F32)
            o_ref[b, pl.ds(r0, c), cols] = (on * (zc * _sigmoid(zc))).astype(o_ref.dtype)
            new_states.append(s * p[3] + pu_[c:])
        return tuple(new_states)

    def chunk(j, carry):
        prepared, states = carry
        ps = apply_a(prepared, states)
        vec, kk = prepare_a(j + 1)
        states = apply_b(j, prepared, states, ps)
        return prepare_b(vec, kk), states

    j0 = fv // c
    last = r // c - 1
    o_ref[:, 0:j0 * c, :] = jnp.zeros((nb, j0 * c, wd), o_ref.dtype)
    init = (prepare_b(*prepare_a(j0)), tuple(s0_ref[b, hh] for b, hh in chains))
    prepared, states = lax.fori_loop(j0, last, chunk, init)
    for (b, hh), s in zip(chains, apply_b(last, prepared, states, apply_a(prepared, states))):
        st_ref[b, hh] = s


def _gdn(conv3, bg3, z3, s0, conv_w, gn, fv, c, nb, hg):
    bsz, r, _ = conv3.shape
    assert r % c == 0 and fv // c >= 1 and c >= SUBLANES and bsz % nb == 0 and H_A % hg == 0
    ng = H_A // hg
    wd = hg * LANES
    blk = lambda off: pl.BlockSpec((nb, r, wd), lambda i, g: (i, 0, off * ng + g))
    cw = lambda off: pl.BlockSpec((CONV_W, wd), lambda i, g: (0, off * ng + g))
    st = pl.BlockSpec((nb, hg, DK_A, DV_A), lambda i, g: (i, g, 0, 0))
    return pl.pallas_call(
        functools.partial(_gdn_kernel, fv, c),
        grid=(bsz // nb, ng),
        in_specs=[blk(0), blk(1), blk(2), cw(0), cw(1), cw(2),
                  pl.BlockSpec((nb, r, LANES), lambda i, g: (i, 0, 0)), blk(0), st,
                  pl.BlockSpec((1, LANES), lambda i, g: (0, 0))],
        out_specs=[blk(0), st],
        out_shape=[jax.ShapeDtypeStruct((bsz, r, VA), F32), jax.ShapeDtypeStruct((bsz, H_A, DK_A, DV_A), F32)],
        compiler_params=_cparams(("parallel", "parallel")),
        name="gdn",
    )(conv3, conv3, conv3, conv_w, conv_w, conv_w, bg3, z3, s0, gn)


def _lambda(lam_ref, lam_init):
    l1 = jnp.sum(lam_ref[0:1, :] * lam_ref[1:2, :], axis=-1, keepdims=True)
    l2 = jnp.sum(lam_ref[2:3, :] * lam_ref[3:4, :], axis=-1, keepdims=True)
    return jnp.exp(l1) - jnp.exp(l2) + lam_init


def _sub_norm(o0, o1, lam, sn, lam_init):
    ob = o0 - lam * o1
    return ob * lax.rsqrt(jnp.mean(ob * ob, axis=-1, keepdims=True) + EPS) * sn * (1.0 - lam_init)


def _attn_prompt_kernel(first_q, lam_init, q_ref, k_ref, v_ref, bt_ref, far_ref, kb_ref, lam_ref, sn_ref, o_ref):
    blk = ATT_BLK
    nq, nk = ATT_Q // blk, ATT_KV // blk
    qi = pl.program_id(2)

    @pl.when(qi < first_q)
    def _():
        o_ref[0] = jnp.zeros(o_ref.shape[1:], o_ref.dtype)

    @pl.when(qi >= first_q)
    def _():
        q = q_ref[0]
        lane = lax.broadcasted_iota(jnp.int32, q.shape, 1)
        zero = jnp.zeros_like(q)
        qs = jnp.concatenate([jnp.where(lane < DK_B, q, zero), jnp.where(lane >= DK_B, q, zero)], axis=0)
        diag = bt_ref[0, 0]
        near = bt_ref[0, 1]
        far = far_ref[0]

        def body(j, carry):
            m, l, acc = carry
            k0 = pl.multiple_of(j * ATT_KV, ATT_KV)
            ks = k_ref[0, pl.ds(k0, ATT_KV), :]
            vs = v_ref[0, pl.ds(k0, ATT_KV), :]
            rows = []
            for a in range(nq):
                tiles = []
                for hb in range(nk):
                    rel = (nq * qi + a) - (nk * j + hb)
                    tiles.append(jnp.where(rel == 0, diag, jnp.where(rel == 1, near, jnp.where(rel < 0, NEG, far))))
                rows.append(jnp.concatenate(tiles, axis=1))
            bias = jnp.concatenate(rows, axis=0) + kb_ref[pl.ds(j, 1), :]
            s = _dot_nt(qs, ks) + jnp.concatenate([bias, bias], axis=0)
            m_new = jnp.maximum(m, jnp.max(s, axis=-1, keepdims=True))
            alpha = jnp.exp(m - m_new)
            p = jnp.exp(s - m_new)
            l = alpha * l + jnp.sum(p, axis=-1, keepdims=True)
            acc = alpha * acc + _dot(p.astype(BF16), vs)
            return m_new, l, acc

        init = (jnp.full((2 * ATT_Q, 1), NEG, F32), jnp.zeros((2 * ATT_Q, 1), F32), jnp.zeros((2 * ATT_Q, DV_B), F32))
        _, l, acc = lax.fori_loop(0, (nq * qi + nq - 1) // nk + 1, body, init)
        o = acc / l
        o_ref[0] = _sub_norm(o[:ATT_Q], o[ATT_Q:], _lambda(lam_ref, lam_init), sn_ref[...],
                             lam_init).astype(o_ref.dtype)


def _attn_prompt(qd3, kd3, vd3, btiles, far, kbias, lam4, sn, first_q, lam_init):
    bsz, lp, _ = qd3.shape
    assert lp % ATT_KV == 0 and lp % ATT_Q == 0
    kv = pl.BlockSpec((1, lp, LANES), lambda b, h, i: (b, 0, h))
    qo = pl.BlockSpec((1, ATT_Q, LANES), lambda b, h, i: (b, i, h))
    full = lambda a: pl.BlockSpec(a.shape, lambda b, h, i: (0,) * a.ndim)
    return pl.pallas_call(
        functools.partial(_attn_prompt_kernel, first_q, lam_init),
        grid=(bsz, H_B, lp // ATT_Q),
        in_specs=[qo, kv, kv,
                  pl.BlockSpec((1, 2, ATT_BLK, LANES), lambda b, h, i: (h, 0, 0, 0)),
                  pl.BlockSpec((1, 1, LANES), lambda b, h, i: (h, 0, 0)),
                  full(kbias), full(lam4), full(sn)],
        out_specs=qo,
        out_shape=jax.ShapeDtypeStruct((bsz, lp, VB), F32),
        compiler_params=_cparams(("parallel", "parallel", "parallel")),
        name="diff_attn_prompt",
    )(qd3, kd3, vd3, btiles, far, kbias, lam4, sn)


def _attn_sample_kernel(npp, t_new, lam_init, pt_ref, q_ref, kn_ref, vn_ref, ck_ref, cv_ref, last_ref, self_ref,
                        far_ref, lam_ref, sn_ref, o_ref, qm, kself, vself, m_s, l_s, acc_s, kbuf, vbuf, sem):
    b = pl.program_id(0)
    st = pl.program_id(1)
    nb = pl.num_programs(0)
    nst = pl.num_programs(1)
    nrow = H_B * 2 * SUBLANES
    hrows = 2 * SUBLANES
    t = b * nst + st
    slot = t % 2

    def page_copies(bb, ss, sl):
        out = []
        for g in range(npp):
            page = pt_ref[bb, ss * npp + g]
            out.append(pltpu.make_async_copy(ck_ref.at[page], kbuf.at[sl, g], sem.at[sl]))
            out.append(pltpu.make_async_copy(cv_ref.at[page], vbuf.at[sl, g], sem.at[sl]))
        return out

    @pl.when(t == 0)
    def _():
        for cp in page_copies(0, 0, 0):
            cp.start()

    @pl.when(t + 1 < nb * nst)
    def _():
        wrap = st + 1 == nst
        for cp in page_copies(jnp.where(wrap, b + 1, b), jnp.where(wrap, 0, st + 1), 1 - slot):
            cp.start()

    for cp in page_copies(b, st, slot):
        cp.wait()

    def update(s, values_of_head):
        m_old = m_s[...]
        m_new = jnp.maximum(m_old, jnp.max(s, axis=-1, keepdims=True))
        alpha = jnp.exp(m_old - m_new)
        pr = jnp.exp(s - m_new)
        l_s[...] = alpha * l_s[...] + jnp.sum(pr, axis=-1, keepdims=True)
        prb = pr.astype(BF16)
        pv = [_dot(prb[h * hrows:(h + 1) * hrows], values_of_head(h)) for h in range(H_B)]
        acc_s[...] = alpha * acc_s[...] + jnp.concatenate(pv, axis=0)
        m_s[...] = m_new

    @pl.when(st == 0)
    def _():
        q8 = q_ref[0]
        lane = lax.broadcasted_iota(jnp.int32, q8.shape, 1)
        qm[...] = jnp.concatenate([jnp.where(lane // DK_B == g, q8, 0.0) for g in range(2 * H_B)],
                                  axis=0).astype(qm.dtype)
        kself[...] = jnp.zeros(kself.shape, F32)
        vself[...] = jnp.zeros(vself.shape, F32)
        kself[0:SUBLANES, :] = kn_ref[0]
        vself[0:SUBLANES, :] = vn_ref[0]
        m_s[...] = jnp.full(m_s.shape, NEG, F32)
        l_s[...] = jnp.zeros(l_s.shape, F32)
        acc_s[...] = jnp.zeros(acc_s.shape, F32)
        rows = lax.broadcasted_iota(jnp.int32, (nrow, PAGE), 0) % SUBLANES
        cols = lax.broadcasted_iota(jnp.int32, (nrow, PAGE), 1)
        ok = (cols <= rows) & (cols < t_new)
        s = jnp.where(ok, _dot_nt(qm[...], kself[...].astype(BF16)) + self_ref[...], NEG)
        update(s, lambda h: vself[:, h * DV_B:(h + 1) * DV_B].astype(BF16))

    far = far_ref[:, 0:1]
    parts = []
    for g in range(npp):
        bias = far if g < npp - 1 else jnp.where(st == nst - 1, last_ref[...], far)
        parts.append(_dot(qm[...], kbuf[slot, g].astype(BF16)) + bias)

    def page_values(h):
        return jnp.concatenate([vbuf[slot, g, pl.ds(h, PAGE, stride=H_B), :] for g in range(npp)],
                               axis=0).astype(BF16)

    update(jnp.concatenate(parts, axis=1), page_values)

    @pl.when(st == nst - 1)
    def _():
        o = acc_s[...] / l_s[...]
        lam = _lambda(lam_ref, lam_init)
        for h in range(H_B):
            r0 = h * hrows
            o_ref[0, :, h * DV_B:(h + 1) * DV_B] = _sub_norm(o[r0:r0 + SUBLANES], o[r0 + SUBLANES:r0 + hrows], lam,
                                                             sn_ref[...], lam_init)


def _attn_sample(page_table, q8, kn8, vn8, ck_t, cv_i, last_t, self_t, far_col, lam4, sn, t_new, lam_init):
    bsz, npg = page_table.shape
    npp = math.gcd(PAGES_PER_STEP, npg)
    nrow = H_B * 2 * SUBLANES
    tok = pl.BlockSpec((1, SUBLANES, QB), lambda b, s, pt: (b, 0, 0))
    hbm = pl.BlockSpec(memory_space=pl.ANY)
    full = lambda a: pl.BlockSpec(a.shape, lambda b, s, pt: (0,) * a.ndim)
    return pl.pallas_call(
        functools.partial(_attn_sample_kernel, npp, t_new, lam_init),
        grid_spec=pltpu.PrefetchScalarGridSpec(
            num_scalar_prefetch=1,
            grid=(bsz, npg // npp),
            in_specs=[tok, tok, tok, hbm, hbm, full(last_t), full(self_t), full(far_col), full(lam4), full(sn)],
            out_specs=tok,
            scratch_shapes=[pltpu.VMEM((nrow, QB), BF16), pltpu.VMEM((PAGE, QB), F32), pltpu.VMEM((PAGE, VB), F32),
                            pltpu.VMEM((nrow, 1), F32), pltpu.VMEM((nrow, 1), F32), pltpu.VMEM((nrow, DV_B), F32),
                            pltpu.VMEM((2, npp, QB, PAGE), F32), pltpu.VMEM((2, npp, PAGE * H_B, DV_B), F32),
                            pltpu.SemaphoreType.DMA((2,))],
        ),
        out_shape=jax.ShapeDtypeStruct((bsz, SUBLANES, VB), F32),
        compiler_params=_cparams(("arbitrary", "arbitrary")),
        name="diff_attn_sample",
    )(page_table, q8, kn8, vn8, ck_t, cv_i, last_t, self_t, far_col, lam4, sn)


def _post_kernel(prompt_tiles, *refs):
    wa_ref, wb_ref, wo_ref, n2_ref, wr_ref, br_ref, x1_ref, hm_ref, ti_ref, tg_ref = refs[10:]
    is_sample = pl.program_id(0) >= prompt_tiles
    x, oa, ob, sga, sgb = (jnp.where(is_sample, s_ref[...], p_ref[...]) for p_ref, s_ref in zip(refs[:5], refs[5:10]))
    ya = _dot(oa.astype(BF16), wa_ref[...])
    yb = _dot(ob.astype(BF16), wb_ref[...])
    merged = sga.astype(F32) * ya + sgb.astype(F32) * yb
    x1 = x + _dot(merged.astype(BF16), wo_ref[...])
    x1_ref[...] = x1
    hm = x1 * lax.rsqrt(jnp.mean(x1 * x1, axis=-1, keepdims=True) + EPS) * n2_ref[...]
    hm_ref[...] = hm
    logits = _dot(hm, wr_ref[...], HI) + br_ref[...]
    lane = lax.broadcasted_iota(jnp.int32, logits.shape, 1)
    lane_f = lane.astype(F32)
    work = jnp.where(lane < N_EXPERTS, logits, -jnp.inf)
    ti = jnp.zeros(logits.shape, F32)
    tg = jnp.zeros(logits.shape, F32)
    v0 = None
    den = None
    for k in range(TOP_K):
        vk = jnp.max(work, axis=-1, keepdims=True)
        ik = jnp.min(jnp.where(work == vk, lane_f, float(LANES)), axis=-1, keepdims=True)
        work = jnp.where(lane_f == ik, -jnp.inf, work)
        if k == 0:
            v0 = vk
        ek = jnp.exp(vk - v0)
        den = ek if k == 0 else den + ek
        ti = jnp.where(lane == k, ik, ti)
        tg = jnp.where(lane == k, ek, tg)
    ti_ref[...] = ti.astype(jnp.int32)
    tg_ref[...] = tg / den


def _post(prompt_in, sample_in, dense, n_prompt, n_sample, prompt_map):
    d = prompt_in[0].shape[1]
    tm = ROW_TILE
    tp = n_prompt // tm
    n_total = n_prompt + n_sample
    widths = (d, VA, VB, d, d)
    full = lambda a: pl.BlockSpec(a.shape, lambda i: (0,) * a.ndim)
    p_spec = lambda k, c: pl.BlockSpec((tm, c), lambda i: ((jnp.minimum(i, tp - 1) if k == 0
                                                             else prompt_map(jnp.minimum(i, tp - 1))), 0))
    s_spec = lambda c: pl.BlockSpec((tm, c), lambda i: (jnp.maximum(i - tp, 0), 0))
    rout = lambda c: pl.BlockSpec((tm, c), lambda i: (i, 0))
    return pl.pallas_call(
        functools.partial(_post_kernel, tp),
        grid=(n_total // tm,),
        in_specs=[p_spec(k, c) for k, c in enumerate(widths)] + [s_spec(c) for c in widths]
                 + [full(a) for a in dense],
        out_specs=[rout(d), rout(d), rout(LANES), rout(LANES)],
        out_shape=[jax.ShapeDtypeStruct((n_total, d), F32), jax.ShapeDtypeStruct((n_total, d), F32),
                   jax.ShapeDtypeStruct((n_total, LANES), jnp.int32), jax.ShapeDtypeStruct((n_total, LANES), F32)],
        compiler_params=_cparams(("parallel",)),
        name="merge_outproj_router",
    )(*prompt_in, *sample_in, *dense)


def _rank_kernel(ti_ref, rank_ref, cnt_ref, base):
    i = pl.program_id(0)
    tm = ti_ref.shape[0]

    @pl.when(i == 0)
    def _():
        base[...] = jnp.zeros(base.shape, F32)

    ti = ti_ref[...].astype(F32)
    lane = lax.broadcasted_iota(jnp.int32, ti.shape, 1)
    lane_f = lane.astype(F32)
    sel = [jnp.sum(jnp.where(lane == k, ti, 0.0), axis=-1, keepdims=True) for k in range(TOP_K)]
    onehot = jnp.zeros(ti.shape, F32)
    for k in range(TOP_K):
        onehot = onehot + (lane_f == sel[k]).astype(F32)
    ri = lax.broadcasted_iota(jnp.int32, (tm, tm), 0)
    ci = lax.broadcasted_iota(jnp.int32, (tm, tm), 1)
    before = _dot((ri > ci).astype(BF16), onehot.astype(BF16)) + base[...]
    rank = jnp.zeros(ti.shape, jnp.int32)
    for k in range(TOP_K):
        rk = jnp.sum(jnp.where(lane_f == sel[k], before, 0.0), axis=-1, keepdims=True)
        rank = jnp.where(lane == k, rk.astype(jnp.int32), rank)
    rank_ref[...] = rank
    base[...] = base[...] + jnp.sum(onehot, axis=0, keepdims=True)
    cnt_ref[...] = base[...]


def _rank(ti):
    n = ti.shape[0]
    tm = ROW_TILE
    return pl.pallas_call(
        _rank_kernel,
        grid=(n // tm,),
        in_specs=[pl.BlockSpec((tm, LANES), lambda i: (i, 0))],
        out_specs=[pl.BlockSpec((tm, LANES), lambda i: (i, 0)), pl.BlockSpec((1, LANES), lambda i: (0, 0))],
        out_shape=[jax.ShapeDtypeStruct((n, LANES), jnp.int32), jax.ShapeDtypeStruct((1, LANES), F32)],
        scratch_shapes=[pltpu.VMEM((1, LANES), F32)],
        compiler_params=_cparams(("arbitrary",)),
        name="moe_rank",
    )(ti)


def _dispatch_kernel(dest_ref, hm_ref, xs_in, xs_out, sem):
    del xs_in
    tm = hm_ref.shape[0]

    def row_copy(r, d):
        return pltpu.make_async_copy(hm_ref.at[pl.ds(r, 1)], xs_out.at[pl.ds(d, 1)], sem)

    def issue(r, c):
        for k in range(TOP_K):
            row_copy(r, dest_ref[0, 0, r * TOP_K + k]).start(priority=k % 2)
        return c

    def drain(r, c):
        for k in range(TOP_K):
            row_copy(r, dest_ref[0, 0, r * TOP_K + k]).wait()
        return c

    lax.fori_loop(0, tm, issue, 0)
    lax.fori_loop(0, tm, drain, 0)


def _dispatch(dest3, hm, xs_zero):
    n, d = hm.shape
    tm = ROW_TILE
    return pl.pallas_call(
        _dispatch_kernel,
        grid=(n // tm,),
        in_specs=[pl.BlockSpec((1, 1, tm * TOP_K), lambda i: (i, 0, 0), memory_space=pltpu.SMEM),
                  pl.BlockSpec((tm, d), lambda i: (i, 0)),
                  pl.BlockSpec(memory_space=pl.ANY)],
        out_specs=pl.BlockSpec(memory_space=pl.ANY),
        out_shape=jax.ShapeDtypeStruct(xs_zero.shape, xs_zero.dtype),
        scratch_shapes=[pltpu.SemaphoreType.DMA],
        input_output_aliases={2: 0},
        compiler_params=_cparams(("arbitrary",)),
        name="moe_dispatch",
    )(dest3, hm, xs_zero)


def _expert_kernel(be_ref, bv_ref, xs_ref, wgu_ref, bgu_ref, wd_ref, bd_ref, y_ref, wgu_bf, wd_bf):
    i = pl.program_id(0)
    d_ff = wd_ref.shape[1]

    @pl.when((i == 0) | (be_ref[i] != be_ref[jnp.maximum(i - 1, 0)]))
    def _():
        wgu_bf[...] = wgu_ref[0].astype(BF16)
        wd_bf[...] = wd_ref[0].astype(BF16)

    @pl.when(bv_ref[i] > 0)
    def _():
        x = xs_ref[...].astype(BF16)
        hgu = _dot(x, wgu_bf[...]) + bgu_ref[0]
        gate = jnp.minimum(hgu[:, :d_ff], SWIGLU_LIMIT)
        up = jnp.clip(hgu[:, d_ff:], -SWIGLU_LIMIT, SWIGLU_LIMIT)
        act = (up + 1.0) * gate * _sigmoid(SWIGLU_ALPHA * gate)
        y_ref[...] = _dot(act.astype(BF16), wd_bf[...]) + bd_ref[0]

    @pl.when(bv_ref[i] == 0)
    def _():
        y_ref[...] = jnp.zeros(y_ref.shape, y_ref.dtype)


def _experts(block_expert, block_valid, xs, w_gu, b_gu3, w_down, b_down3):
    n_slots, d = xs.shape
    n_blocks = n_slots // MOE_BLK
    d_ff = w_down.shape[1]
    return pl.pallas_call(
        _expert_kernel,
        grid_spec=pltpu.PrefetchScalarGridSpec(
            num_scalar_prefetch=2,
            grid=(n_blocks,),
            in_specs=[pl.BlockSpec((MOE_BLK, d), lambda i, be, bv: (i, 0)),
                      pl.BlockSpec((1, d, 2 * d_ff), lambda i, be, bv: (be[i], 0, 0)),
                      pl.BlockSpec((1, 1, 2 * d_ff), lambda i, be, bv: (be[i], 0, 0)),
                      pl.BlockSpec((1, d_ff, d), lambda i, be, bv: (be[i], 0, 0)),
                      pl.BlockSpec((1, 1, d), lambda i, be, bv: (be[i], 0, 0))],
            out_specs=pl.BlockSpec((MOE_BLK, d), lambda i, be, bv: (i, 0)),
            scratch_shapes=[pltpu.VMEM((d, 2 * d_ff), BF16), pltpu.VMEM((d_ff, d), BF16)],
        ),
        out_shape=jax.ShapeDtypeStruct((n_slots, d), F32),
        compiler_params=_cparams(("arbitrary",)),
        name="moe_experts",
    )(block_expert, block_valid, xs, w_gu, b_gu3, w_down, b_down3)


def _combine_kernel(dest_ref, yb_ref, tg_ref, x1_ref, y_ref, buf, sem):
    tm = x1_ref.shape[0]

    def row_copy(r, k, d):
        return pltpu.make_async_copy(yb_ref.at[pl.ds(d, 1)], buf.at[k, pl.ds(r, 1)], sem)

    def issue(r, c):
        for k in range(TOP_K):
            row_copy(r, k, dest_ref[0, 0, r * TOP_K + k]).start(priority=k % 2)
        return c

    def drain(r, c):
        for k in range(TOP_K):
            row_copy(r, k, dest_ref[0, 0, r * TOP_K + k]).wait()
        return c

    lax.fori_loop(0, tm, issue, 0)
    lax.fori_loop(0, tm, drain, 0)
    tg = tg_ref[...]
    lane = lax.broadcasted_iota(jnp.int32, tg.shape, 1)
    acc = jnp.zeros(x1_ref.shape, F32)
    for k in range(TOP_K):
        gk = jnp.sum(jnp.where(lane == k, tg, 0.0), axis=-1, keepdims=True)
        acc = acc + gk * buf[k]
    y_ref[...] = x1_ref[...] + acc


def _combine(dest3, yb, tg, x1):
    n, d = x1.shape
    tm = ROW_TILE
    return pl.pallas_call(
        _combine_kernel,
        grid=(n // tm,),
        in_specs=[pl.BlockSpec((1, 1, tm * TOP_K), lambda i: (i, 0, 0), memory_space=pltpu.SMEM),
                  pl.BlockSpec(memory_space=pl.ANY),
                  pl.BlockSpec((tm, LANES), lambda i: (i, 0)),
                  pl.BlockSpec((tm, d), lambda i: (i, 0))],
        out_specs=pl.BlockSpec((tm, d), lambda i: (i, 0)),
        out_shape=jax.ShapeDtypeStruct((n, d), F32),
        scratch_shapes=[pltpu.VMEM((TOP_K, tm, d), F32), pltpu.SemaphoreType.DMA],
        compiler_params=_cparams(("arbitrary",)),
        name="moe_combine",
    )(dest3, yb, tg, x1)


def _moe(hm, ti, tg, x1, w_gu, b_gu, w_down, b_down):
    n, d = hm.shape
    rank, cnt = _rank(ti)
    counts = cnt[0, :N_EXPERTS].astype(jnp.int32)
    padded = (counts + MOE_BLK - 1) // MOE_BLK * MOE_BLK
    pad_end = jnp.cumsum(padded)
    pad_start = pad_end - padded
    top_i = ti[:, :TOP_K]
    dest = (pad_start[top_i] + rank[:, :TOP_K]).astype(jnp.int32)
    dest3 = dest.reshape(n // ROW_TILE, 1, ROW_TILE * TOP_K)
    n_blocks = (n * TOP_K) // MOE_BLK + N_EXPERTS
    blk_start = jnp.arange(n_blocks, dtype=jnp.int32) * MOE_BLK
    block_expert = jnp.minimum(jnp.sum((pad_end[None, :] <= blk_start[:, None]).astype(jnp.int32), axis=1),
                               N_EXPERTS - 1)
    block_valid = (blk_start < pad_end[-1]).astype(jnp.int32)
    xs = _dispatch(dest3, hm, jnp.zeros((n_blocks * MOE_BLK, d), F32))
    yb = _experts(block_expert, block_valid, xs, w_gu, b_gu.reshape(N_EXPERTS, 1, -1), w_down,
                  b_down.reshape(N_EXPERTS, 1, -1))
    return _combine(dest3, yb, tg, x1)


def _pad_lanes(v, fill=0.0):
    v = v.reshape(1, -1).astype(F32)
    return jnp.pad(v, ((0, 0), (0, LANES - v.shape[1])), constant_values=fill)


def kernel(x_prompt, x_sample, cache_k, cache_v, state_ssm, state_conv, page_table, meta_tokens, rel_bias_table, norm1, w_in, conv_w, a_log, dt_bias, gdn_norm, q_norm, k_norm, lam_q1, lam_k1, lam_q2, lam_k2, sub_norm, w_br_a, w_br_b, w_out, norm2, w_router, b_router, w_gu, b_gu, w_down, b_down):
    bsz, seq, d = x_prompt.shape
    dbs, dseq, _ = x_sample.shape
    depth = w_in.shape[0]
    assert depth == 1 and dseq <= SUBLANES - (CONV_W - 1) and seq % ATT_BLK == 0
    lam_init = 0.8 - 0.6 * math.exp(-0.3 * 0)
    l = 0

    w = w_in[l]
    o_z = CONV_DIM + VA
    w_main = jnp.concatenate([w[:, :o_z], w[:, o_z + 2 * H_A:]], axis=1).astype(BF16)
    w_ba = jnp.pad(w[:, o_z:o_z + 2 * H_A], ((0, 0), (0, LANES - 2 * H_A))).astype(BF16)
    grp = np.arange(QB) // DK_B
    gmat = jnp.asarray((grp[:, None] == grp[None, :]).astype(np.float32) / DK_B, dtype=BF16)
    qn_t = jnp.tile(q_norm[l].astype(F32), QB // DK_B).reshape(1, QB)
    kn_t = jnp.tile(k_norm[l].astype(F32), QB // DK_B).reshape(1, QB)
    alog_p = jnp.pad(a_log[l].astype(F32), (H_A, LANES - 2 * H_A)).reshape(1, LANES)
    dtb_p = jnp.pad(dt_bias[l].astype(F32), (H_A, LANES - 2 * H_A)).reshape(1, LANES)
    n1 = norm1[l].reshape(1, d).astype(F32)
    proj = functools.partial(_inproj, n1=n1, w_main=w_main, w_ba=w_ba, gmat=gmat, qn_t=qn_t, kn_t=kn_t,
                             alog_p=alog_p, dtb_p=dtb_p)
    lam4 = jnp.stack([lam_q1[l], lam_k1[l], lam_q2[l], lam_k2[l]]).astype(F32)
    sn = sub_norm[l].reshape(1, DV_B).astype(F32)
    gn = gdn_norm[l].reshape(1, DV_A).astype(F32)
    cw = conv_w[l].astype(F32)

    ii = np.arange(ATT_BLK)[:, None]
    jj = np.arange(LANES)[None, :]
    bk_prompt = np.stack([_bucket_np(ii - jj), _bucket_np(ATT_BLK + ii - jj)])
    t8 = np.arange(SUBLANES)[:, None]
    bk_sample = np.stack([_bucket_np(PAGE + t8 - jj), _bucket_np(t8 - jj)])
    bt_prompt = _bias_tiles(rel_bias_table.astype(F32), bk_prompt)
    bt_prompt = bt_prompt.at[:, 0].set(jnp.where(jnp.asarray(ii >= jj), bt_prompt[:, 0], NEG))
    bt_sample = _bias_tiles(rel_bias_table.astype(F32), bk_sample)
    far_h = rel_bias_table[N_BUCKETS - 1].astype(F32)
    far_p = jnp.broadcast_to(far_h[:, None, None], (H_B, 1, LANES))
    nrow = H_B * 2 * SUBLANES
    rows_of = lambda t: jnp.broadcast_to(t[:, None], (H_B, 2, SUBLANES, LANES)).reshape(nrow, LANES)
    last_t, self_t = rows_of(bt_sample[:, 0]), rows_of(bt_sample[:, 1])
    far_col = jnp.broadcast_to(far_h[:, None, None], (H_B, 2 * SUBLANES, LANES)).reshape(nrow, LANES)

    lreal = N_META + seq
    lp = -(-lreal // ATT_KV) * ATT_KV
    fp = lp - lreal
    assert fp % SUBLANES == 0 and fp >= GDN_CHUNK and (fp + N_META) % ATT_BLK == 0 and lp % GDN_CHUNK == 0
    xp = jnp.concatenate([jnp.zeros((bsz, fp, d), F32),
                          jnp.broadcast_to(meta_tokens.astype(F32)[None], (bsz, N_META, d)), x_prompt], axis=1)
    conv_p, z_p, bg_p, qd_p, kd_p, vd_p, sga_p, sgb_p, kdb_p, vdb_p = proj(xp.reshape(bsz * lp, d), ATT_KV)
    r3 = lambda a, b_, r_: a.reshape(b_, r_, a.shape[-1])
    oa_p, ssm_p = _gdn(r3(conv_p, bsz, lp), r3(bg_p, bsz, lp), r3(z_p, bsz, lp),
                       jnp.zeros((bsz, H_A, DK_A, DV_A), F32), cw, gn, fp, GDN_CHUNK, 1, H_A)
    kbias = jnp.where(jnp.arange(lp) < fp, NEG, 0.0).astype(F32).reshape(lp // ATT_KV, ATT_KV)
    ob_p = _attn_prompt(r3(qd_p, bsz, lp), r3(kdb_p, bsz, lp), r3(vdb_p, bsz, lp), bt_prompt, far_p, kbias,
                        lam4, sn, fp // ATT_Q, lam_init)

    ns = dbs * dseq
    conv_s, z_s, bg_s, qd_s, kd_s, vd_s, sga_s, sgb_s, _, _ = proj(x_sample.reshape(ns, d), min(ns, 256))
    cs = SUBLANES
    rs = 2 * cs
    fs = rs - dseq

    def chunk_rows(a, head=None):
        a = a.reshape(dbs, dseq, a.shape[-1]).astype(F32)
        parts = [jnp.zeros((dbs, fs - (0 if head is None else head.shape[1]), a.shape[-1]), F32)]
        if head is not None:
            parts.append(head.astype(F32))
        return jnp.concatenate(parts + [a], axis=1)

    oa_s, ssm_s = _gdn(chunk_rows(conv_s, state_conv[l]), chunk_rows(bg_s), chunk_rows(z_s),
                       state_ssm[l].astype(F32), cw, gn, fs, cs, math.gcd(dbs, 4), H_A)
    pad8 = lambda a: jnp.pad(a.reshape(dbs, dseq, a.shape[-1]), ((0, 0), (0, SUBLANES - dseq), (0, 0)))
    n_pool = cache_k.shape[1]
    ck_t = jnp.transpose(cache_k[l], (0, 2, 3, 4, 1)).reshape(n_pool, QB, PAGE)
    cv_i = cache_v[l].reshape(n_pool, PAGE * H_B, DV_B)
    ob_s = _attn_sample(page_table, pad8(qd_s).astype(F32), pad8(kd_s), pad8(vd_s), ck_t, cv_i,
                        last_t, self_t, far_col, lam4, sn, dseq, lam_init)

    wa, wb, wo = w_br_a[l].astype(BF16), w_br_b[l].astype(BF16), w_out[l].astype(BF16)
    n2 = norm2[l].reshape(1, d).astype(F32)
    wr = jnp.pad(w_router[l].astype(F32), ((0, 0), (0, LANES - N_EXPERTS)))
    br = _pad_lanes(b_router[l])
    tiles_b = seq // ROW_TILE
    tiles_lp = lp // ROW_TILE
    skip = (fp + N_META) // ROW_TILE
    npt = bsz * seq
    assert ns % ROW_TILE == 0 and npt % ROW_TILE == 0
    oa_s2 = oa_s[:, fs:].reshape(ns, VA)
    ob_s2 = ob_s[:, :dseq].reshape(ns, VB)
    x1, hm, ti, tg = _post(
        (x_prompt.reshape(npt, d), oa_p.reshape(bsz * lp, VA), ob_p.reshape(bsz * lp, VB), sga_p, sgb_p),
        (x_sample.reshape(ns, d), oa_s2, ob_s2, sga_s, sgb_s), (wa, wb, wo, n2, wr, br), npt, ns,
        lambda i: (i // tiles_b) * tiles_lp + skip + i % tiles_b)

    y = _moe(hm, ti, tg, x1, w_gu[l], b_gu[l], w_down[l], b_down[l])
    y_prompt = y[:npt].reshape(bsz, seq, d)
    y_sample = y[npt:].reshape(dbs, dseq, d)

    k_prompt = r3(kd_p, bsz, lp)[:, fp:].reshape(1, bsz, lreal, H_B, 2, DK_B)
    v_prompt = r3(vd_p, bsz, lp)[:, fp:].reshape(1, bsz, lreal, H_B, DV_B)
    conv_prompt = r3(conv_p, bsz, lp)[:, lp - (CONV_W - 1):][None]
    xpad_s = jnp.concatenate([state_conv[l].astype(F32), conv_s.reshape(dbs, dseq, CONV_DIM)], axis=1)
    conv_sample = xpad_s[:, dseq:][None]
    return (y_prompt, y_sample, k_prompt, v_prompt, ssm_p[None], conv_prompt,
            kd_s.reshape(1, dbs, dseq, H_B, 2, DK_B), vd_s.reshape(1, dbs, dseq, H_B, DV_B), ssm_s[None], conv_sample)
```

```python
import functools
import math

import numpy as np
import jax
import jax.numpy as jnp
from jax import lax
from jax.experimental import pallas as pl
from jax.experimental.pallas import tpu as pltpu

F32 = jnp.float32
BF16 = jnp.bfloat16
HI = lax.Precision.HIGHEST

N_META = 16
H_A, DK_A, DV_A = 4, 128, 128
CONV_W = 4
H_B, DK_B = 4, 64
DV_B = 2 * DK_B
N_BUCKETS, MAX_DIST = 32, 128
N_EXPERTS, TOP_K = 32, 4
SWIGLU_LIMIT, SWIGLU_ALPHA = 7.0, 1.702
EPS = 1e-6
PAGE = 128
QA = H_A * DK_A
VA = H_A * DV_A
QB = H_B * 2 * DK_B
VB = H_B * DV_B
CONV_DIM = 2 * QA + VA

LANES = 128
SUBLANES = 8
VMEM_LIMIT = 56 * 1024 * 1024

GDN_CHUNK = 64
ATT_BLK = 128
ATT_KV = 2 * ATT_BLK
ATT_Q = 2 * ATT_BLK
PAGES_PER_STEP = 16
MOE_BLK = 256
ROW_TILE = 128
POST_TILE = 256
NEG = -1e30


def _dot(a, b, prec=None):
    return jnp.dot(a, b, preferred_element_type=F32, precision=prec)


def _dot_nt(a, b, prec=None):
    return lax.dot_general(a, b, (((1,), (1,)), ((), ())), preferred_element_type=F32, precision=prec)


def _dot_tn(a, b, prec=None):
    return lax.dot_general(a, b, (((0,), (0,)), ((), ())), preferred_element_type=F32, precision=prec)


def _split(a):
    hi = a.astype(BF16)
    return hi, (a - hi.astype(F32)).astype(BF16)


def _lhs3(a):
    hi, lo = _split(a)
    return jnp.concatenate([hi, lo, hi], axis=1)


def _rhs3(b):
    hi, lo = _split(b)
    return jnp.concatenate([hi, hi, lo], axis=0)


def _pack_halves(x):
    w = x.shape[1] // 2
    bits = lambda v: pltpu.bitcast(v.astype(BF16).astype(F32), jnp.uint32)
    return (bits(x[:, :w]) >> 16) | (bits(x[:, w:]) & jnp.uint32(0xFFFF0000))


def _unpack_halves(u):
    lo = pltpu.bitcast(u << 16, F32)
    hi = pltpu.bitcast(u & jnp.uint32(0xFFFF0000), F32)
    return jnp.concatenate([lo, hi], axis=1)


def _sigmoid(x):
    return 1.0 / (1.0 + jnp.exp(-x))


def _cparams(sem, flags=None):
    return pltpu.CompilerParams(dimension_semantics=sem, vmem_limit_bytes=VMEM_LIMIT, flags=flags)


def _bucket_np(d):
    d = np.maximum(d, 0)
    df = np.maximum(d, 1).astype(np.float32)
    max_exact = N_BUCKETS // 2
    large = max_exact + (np.log(df / np.float32(max_exact)) / np.float32(math.log(MAX_DIST / max_exact))
                         * np.float32(N_BUCKETS - max_exact)).astype(np.int32)
    return np.where(d < max_exact, d, np.minimum(large, N_BUCKETS - 1)).astype(np.int32)


def _bias_kernel(tab_ref, bk_ref, o_ref):
    for t in range(bk_ref.shape[0]):
        bk = bk_ref[t]
        for h in range(H_B):
            acc = jnp.zeros(bk.shape, F32)
            for b in range(N_BUCKETS):
                acc = jnp.where(bk == b, tab_ref[b, h], acc)
            o_ref[h, t] = acc


def _bias_tiles(table, buckets):
    t, r, _ = buckets.shape
    return pl.pallas_call(
        _bias_kernel,
        out_shape=jax.ShapeDtypeStruct((H_B, t, r, LANES), F32),
        in_specs=[pl.BlockSpec(memory_space=pltpu.SMEM), pl.BlockSpec(memory_space=pltpu.VMEM)],
        out_specs=pl.BlockSpec(memory_space=pltpu.VMEM),
        name="rel_bias_tiles",
    )(table, jnp.asarray(buckets))


def _inproj_kernel(x_ref, n1_ref, w_ref, wba_ref, gm_ref, qn_ref, kn_ref, alog_ref, dtb_ref,
                   conv_ref, z_ref, bg_ref, qd_ref, kd_ref, vd_ref, sga_ref, sgb_ref, kdb_ref, vdb_ref):
    x = x_ref[...]
    ms = jnp.mean(x * x, axis=-1, keepdims=True)
    h = (x * lax.rsqrt(ms + EPS) * n1_ref[...]).astype(BF16)
    o = 0
    conv_ref[...] = _dot(h, w_ref[:, o:o + CONV_DIM]); o += CONV_DIM
    z_ref[...] = _dot(h, w_ref[:, o:o + VA]).astype(z_ref.dtype); o += VA

    def group_norm(y, g):
        sq = y * y
        hi = sq.astype(BF16)
        lo = (sq - hi.astype(F32)).astype(BF16)
        msq = _dot(hi, gm_ref[...]) + _dot(lo, gm_ref[...])
        return y * lax.rsqrt(msq + EPS) * g

    qb = _dot(h, w_ref[:, o:o + QB]); o += QB
    qd_ref[...] = (group_norm(qb, qn_ref[...]) * (DK_B ** -0.5)).astype(qd_ref.dtype)
    kb = _dot(h, w_ref[:, o:o + QB]); o += QB
    kd = group_norm(kb, kn_ref[...])
    kd_ref[...] = kd
    kdb_ref[...] = kd.astype(kdb_ref.dtype)
    vd = _dot(h, w_ref[:, o:o + VB]); o += VB
    vd_ref[...] = vd
    vdb_ref[...] = vd.astype(vdb_ref.dtype)
    d_model = x.shape[1]
    sga_ref[...] = _sigmoid(_dot(h, w_ref[:, o:o + d_model])).astype(sga_ref.dtype); o += d_model
    sgb_ref[...] = _sigmoid(_dot(h, w_ref[:, o:o + d_model])).astype(sgb_ref.dtype)
    t = _dot(h, wba_ref[...])
    lane = lax.broadcasted_iota(jnp.int32, t.shape, 1)
    ta = t + dtb_ref[...]
    sp = jnp.maximum(ta, 0.0) + jnp.log(1.0 + jnp.exp(-jnp.abs(ta)))
    bg_ref[...] = jnp.where(lane < H_A, _sigmoid(t), -jnp.exp(alog_ref[...]) * sp)


def _inproj(x2d, tm, n1, w_main, w_ba, gmat, qn_t, kn_t, alog_p, dtb_p):
    n, d = x2d.shape
    assert n % tm == 0
    row = lambda c: pl.BlockSpec((tm, c), lambda i: (i, 0))
    full = lambda a: pl.BlockSpec(a.shape, lambda i: (0,) * a.ndim)
    outs = [(CONV_DIM, F32), (VA, BF16), (LANES, F32), (QB, BF16), (QB, F32), (VB, F32), (d, BF16), (d, BF16),
            (QB, BF16), (VB, BF16)]
    return pl.pallas_call(
        _inproj_kernel,
        grid=(n // tm,),
        in_specs=[row(d)] + [full(a) for a in (n1, w_main, w_ba, gmat, qn_t, kn_t, alog_p, dtb_p)],
        out_specs=[row(c) for c, _ in outs],
        out_shape=[jax.ShapeDtypeStruct((n, c), dt) for c, dt in outs],
        compiler_params=_cparams(("parallel",)),
        name="in_proj",
    )(x2d, n1, w_main, w_ba, gmat, qn_t, kn_t, alog_p, dtb_p)


def _gdn_kernel(fv, c, q_ref, k_ref, v_ref, cwq_ref, cwk_ref, cwv_ref, bg_ref, z_ref, s0_ref, gn_ref,
                o_ref, st_ref):
    nb, r, wd = q_ref.shape
    hg = wd // LANES
    head0 = pl.program_id(1) * hg
    chains = [(b, hh) for b in range(nb) for hh in range(hg)]

    ri = lax.broadcasted_iota(jnp.int32, (c, c), 0)
    ci = lax.broadcasted_iota(jnp.int32, (c, c), 1)
    incl = ri >= ci
    strict = ri > ci
    eye = (ri == ci).astype(F32)
    lane = lax.broadcasted_iota(jnp.int32, (c, LANES), 1)
    sub_t = lax.broadcasted_iota(jnp.int32, (LANES, c), 0)
    rowid = lax.broadcasted_iota(jnp.int32, (c, 1), 0)
    colid = lax.broadcasted_iota(jnp.int32, (1, c), 1)

    def conv(x_ref, cw_ref, b, cols, r0):
        w = x_ref[b, pl.ds(r0 - SUBLANES, c + SUBLANES), cols]
        acc = w[SUBLANES - 3:SUBLANES - 3 + c] * cw_ref[0:1, cols]
        for j in range(1, CONV_W):
            acc = acc + w[SUBLANES - 3 + j:SUBLANES - 3 + j + c] * cw_ref[j:j + 1, cols]
        return acc * _sigmoid(acc)

    def elementwise(b, hh, r0):
        cols = slice(hh * LANES, (hh + 1) * LANES)
        head = head0 + hh
        vcol = ((r0 + rowid) >= fv).astype(F32)
        vrow = ((r0 + colid) >= fv).astype(F32)
        qv = conv(q_ref, cwq_ref, b, cols, r0)
        kv = conv(k_ref, cwk_ref, b, cols, r0)
        v = conv(v_ref, cwv_ref, b, cols, r0) * vcol
        q = qv * lax.rsqrt(jnp.sum(qv * qv, axis=-1, keepdims=True) + EPS) * (DK_A ** -0.5) * vcol
        k = kv * lax.rsqrt(jnp.sum(kv * kv, axis=-1, keepdims=True) + EPS) * vcol
        bgc = bg_ref[b, pl.ds(r0, c), :]
        beta = jnp.sum(jnp.where(lane == head, bgc, 0.0), axis=-1, keepdims=True) * vcol
        g_col = jnp.sum(jnp.where(lane == H_A + head, bgc, 0.0), axis=-1, keepdims=True) * vcol
        g_row = jnp.sum(jnp.where(sub_t == H_A + head, bgc.T, 0.0), axis=0, keepdims=True) * vrow
        gc = jnp.sum(jnp.where(incl, g_row, 0.0), axis=-1, keepdims=True)
        gr = jnp.sum(jnp.where(ri <= ci, g_col, 0.0), axis=0, keepdims=True)
        decay = jnp.where(incl, jnp.exp(jnp.where(incl, gc - gr, 0.0)), 0.0)
        kb = k * beta
        eg = jnp.exp(gc)
        g_last = jnp.sum(jnp.where(rowid == c - 1, gc, 0.0), axis=0, keepdims=True)
        k_dt = (k * jnp.exp(g_last - gc)).T
        lhs = jnp.concatenate([kb, q], axis=0).astype(BF16)
        rhs = jnp.concatenate([v * beta, kb * eg], axis=-1)
        return lhs, k.astype(BF16), decay, rhs, q * eg, k_dt, jnp.exp(g_last)

    def prepare_a(j):
        r0 = pl.multiple_of(j * c, c)
        vec = [elementwise(b, hh, r0) for b, hh in chains]
        kk = [_dot_nt(v_[0], v_[1]) for v_ in vec]
        return vec, kk

    def prepare_b(vec, kk):
        low = [jnp.where(strict, kk_[:c] * v_[2], 0.0) for v_, kk_ in zip(vec, kk)]
        intra = [kk_[c:] * v_[2] for v_, kk_ in zip(vec, kk)]
        inv = [eye - lw for lw in low]
        levels = int(math.log2(c)) - 1
        pw = [_dot(_lhs3(lw), _rhs3(lw)) for lw in low]
        for lev in range(levels):
            pw_r = [_rhs3(p) for p in pw]
            if lev + 1 < levels:
                pw = [_dot(_lhs3(p), r_) for p, r_ in zip(pw, pw_r)]
            inv = [iv + _dot(_lhs3(iv), r_) for iv, r_ in zip(inv, pw_r)]
        sol = [_dot(_lhs3(iv), _rhs3(v_[3])) for iv, v_ in zip(inv, vec)]
        out = []
        for v_, sl, it in zip(vec, sol, intra):
            on_state = jnp.concatenate([sl[:, DV_A:], v_[4]], axis=0).astype(BF16)
            on_u = jnp.concatenate([it, v_[5]], axis=0).astype(BF16)
            out.append((sl[:, :DV_A], on_state, on_u, v_[6]))
        return tuple(out)

    def apply_a(prepared, states):
        return [_dot(p[1], s.astype(BF16)) for p, s in zip(prepared, states)]

    def apply_b(j, prepared, states, ps):
        r0 = pl.multiple_of(j * c, c)
        u = [p[0] - ps_[:c] for p, ps_ in zip(prepared, ps)]
        pu = [_dot(p[2], u_.astype(BF16)) for p, u_ in zip(prepared, u)]
        new_states = []
        for (b, hh), p, s, ps_, pu_ in zip(chains, prepared, states, ps, pu):
            cols = slice(hh * LANES, (hh + 1) * LANES)
            o = ps_[c:] + pu_[:c]
            on = o * lax.rsqrt(jnp.mean(o * o, axis=-1, keepdims=True) + EPS) * gn_ref[...]
            zc = z_ref[b, pl.ds(r0, c), cols].astype(F32)
            o_ref[b, pl.ds(r0, c), cols] = (on * (zc * _sigmoid(zc))).astype(o_ref.dtype)
            new_states.append(s * p[3] + pu_[c:])
        return tuple(new_states)

    def chunk(j, carry):
        prepared, states = carry
        ps = apply_a(prepared, states)
        vec, kk = prepare_a(j + 1)
        states = apply_b(j, prepared, states, ps)
        return prepare_b(vec, kk), states

    j0 = fv // c
    last = r // c - 1
    o_ref[:, 0:j0 * c, :] = jnp.zeros((nb, j0 * c, wd), o_ref.dtype)
    init = (prepare_b(*prepare_a(j0)), tuple(s0_ref[b, hh] for b, hh in chains))
    prepared, states = lax.fori_loop(j0, last, chunk, init)
    for (b, hh), s in zip(chains, apply_b(last, prepared, states, apply_a(prepared, states))):
        st_ref[b, hh] = s


def _gdn(conv3, bg3, z3, s0, conv_w, gn, fv, c, nb, hg):
    bsz, r, _ = conv3.shape
    assert r % c == 0 and fv // c >= 1 and c >= SUBLANES and bsz % nb == 0 and H_A % hg == 0
    ng = H_A // hg
    wd = hg * LANES
    blk = lambda off: pl.BlockSpec((nb, r, wd), lambda i, g: (i, 0, off * ng + g))
    cw = lambda off: pl.BlockSpec((CONV_W, wd), lambda i, g: (0, off * ng + g))
    st = pl.BlockSpec((nb, hg, DK_A, DV_A), lambda i, g: (i, g, 0, 0))
    return pl.pallas_call(
        functools.partial(_gdn_kernel, fv, c),
        grid=(bsz // nb, ng),
        in_specs=[blk(0), blk(1), blk(2), cw(0), cw(1), cw(2),
                  pl.BlockSpec((nb, r, LANES), lambda i, g: (i, 0, 0)), blk(0), st,
                  pl.BlockSpec((1, LANES), lambda i, g: (0, 0))],
        out_specs=[blk(0), st],
        out_shape=[jax.ShapeDtypeStruct((bsz, r, VA), F32), jax.ShapeDtypeStruct((bsz, H_A, DK_A, DV_A), F32)],
        compiler_params=_cparams(("parallel", "parallel")),
        name="gdn",
    )(conv3, conv3, conv3, conv_w, conv_w, conv_w, bg3, z3, s0, gn)


def _lambda(lam_ref, lam_init):
    l1 = jnp.sum(lam_ref[0:1, :] * lam_ref[1:2, :], axis=-1, keepdims=True)
    l2 = jnp.sum(lam_ref[2:3, :] * lam_ref[3:4, :], axis=-1, keepdims=True)
    return jnp.exp(l1) - jnp.exp(l2) + lam_init


def _sub_norm(o0, o1, lam, sn, lam_init):
    ob = o0 - lam * o1
    return ob * lax.rsqrt(jnp.mean(ob * ob, axis=-1, keepdims=True) + EPS) * sn * (1.0 - lam_init)


def _attn_prompt_kernel(first_q, lam_init, q_ref, k_ref, v_ref, bt_ref, far_ref, kb_ref, lam_ref, sn_ref, o_ref):
    qi = pl.program_id(2)

    @pl.when(qi < first_q)
    def _():
        o_ref[0] = jnp.zeros(o_ref.shape[1:], o_ref.dtype)

    @pl.when(qi >= first_q)
    def _():
        q = q_ref[0]
        lane = lax.broadcasted_iota(jnp.int32, q.shape, 1)
        zero = jnp.zeros_like(q)
        qs = jnp.concatenate([jnp.where(lane < DK_B, q, zero), jnp.where(lane >= DK_B, q, zero)], axis=0)

        def scores(j):
            k0 = pl.multiple_of(j * ATT_KV, ATT_KV)
            return _dot_nt(qs, k_ref[0, pl.ds(k0, ATT_KV), :])

        def step(j, carry, bias, last=False):
            m, l, acc, s_raw = carry
            s_next = s_raw if last else scores(j + 1)
            k0 = pl.multiple_of(j * ATT_KV, ATT_KV)
            s = s_raw + bias
            m_new = jnp.maximum(m, jnp.max(s, axis=-1, keepdims=True))
            alpha = jnp.exp(m - m_new)
            p = jnp.exp(s - m_new)
            l = alpha * l + jnp.sum(p, axis=-1, keepdims=True)
            acc = alpha * acc + _dot(p.astype(BF16), v_ref[0, pl.ds(k0, ATT_KV), :])
            return m_new, l, acc, s_next

        def tile_bias(j, which):
            t = bt_ref[0, which] + kb_ref[pl.ds(j, 1), :]
            return jnp.concatenate([t, t], axis=0)

        carry = (jnp.full((2 * ATT_Q, 1), NEG, F32), jnp.zeros((2 * ATT_Q, 1), F32),
                 jnp.zeros((2 * ATT_Q, DV_B), F32), scores(0))
        carry = lax.fori_loop(0, qi - 1, lambda j, c: step(j, c, far_ref[0] + kb_ref[pl.ds(j, 1), :]), carry)
        carry = lax.cond(qi >= 1, lambda c: step(qi - 1, c, tile_bias(qi - 1, 1)), lambda c: c, carry)
        _, l, acc, _ = step(qi, carry, tile_bias(qi, 0), last=True)
        o = acc / l
        o_ref[0] = _sub_norm(o[:ATT_Q], o[ATT_Q:], _lambda(lam_ref, lam_init), sn_ref[...],
                             lam_init).astype(o_ref.dtype)


def _attn_prompt(qd3, kd3, vd3, btiles, far, kbias, lam4, sn, first_q, lam_init):
    bsz, lp, _ = qd3.shape
    assert lp % ATT_KV == 0 and ATT_Q == ATT_KV
    kv = pl.BlockSpec((1, lp, LANES), lambda b, h, i: (b, 0, h))
    qo = pl.BlockSpec((1, ATT_Q, LANES), lambda b, h, i: (b, i, h))
    full = lambda a: pl.BlockSpec(a.shape, lambda b, h, i: (0,) * a.ndim)
    return pl.pallas_call(
        functools.partial(_attn_prompt_kernel, first_q, lam_init),
        grid=(bsz, H_B, lp // ATT_Q),
        in_specs=[qo, kv, kv,
                  pl.BlockSpec((1, 2, ATT_Q, ATT_KV), lambda b, h, i: (h, 0, 0, 0)),
                  pl.BlockSpec((1, 1, ATT_KV), lambda b, h, i: (h, 0, 0)),
                  full(kbias), full(lam4), full(sn)],
        out_specs=qo,
        out_shape=jax.ShapeDtypeStruct((bsz, lp, VB), F32),
        compiler_params=_cparams(("parallel", "parallel", "parallel")),
        name="diff_attn_prompt",
    )(qd3, kd3, vd3, btiles, far, kbias, lam4, sn)


def _attn_sample_kernel(npp, t_new, lam_init, pt_ref, q_ref, kn_ref, vn_ref, ck_ref, cv_ref, last_ref, self_ref,
                        far_ref, lam_ref, sn_ref, o_ref, qm, kself, vself, m_s, l_s, acc_s, kbuf, vbuf, sem):
    b = pl.program_id(0)
    st = pl.program_id(1)
    nb = pl.num_programs(0)
    nst = pl.num_programs(1)
    nrow = H_B * 2 * SUBLANES
    hrows = 2 * SUBLANES
    t = b * nst + st
    slot = t % 2

    def page_copies(bb, ss, sl):
        out = []
        for g in range(npp):
            page = pt_ref[bb, ss * npp + g]
            out.append(pltpu.make_async_copy(ck_ref.at[page], kbuf.at[sl, g], sem.at[sl]))
            out.append(pltpu.make_async_copy(cv_ref.at[page], vbuf.at[sl, g], sem.at[sl]))
        return out

    @pl.when(t == 0)
    def _():
        for cp in page_copies(0, 0, 0):
            cp.start()

    @pl.when(t + 1 < nb * nst)
    def _():
        wrap = st + 1 == nst
        for cp in page_copies(jnp.where(wrap, b + 1, b), jnp.where(wrap, 0, st + 1), 1 - slot):
            cp.start()

    for cp in page_copies(b, st, slot):
        cp.wait()

    def update(s, values_of_head):
        m_old = m_s[...]
        m_new = jnp.maximum(m_old, jnp.max(s, axis=-1, keepdims=True))
        alpha = jnp.exp(m_old - m_new)
        pr = jnp.exp(s - m_new)
        l_s[...] = alpha * l_s[...] + jnp.sum(pr, axis=-1, keepdims=True)
        prb = pr.astype(BF16)
        pv = [_dot(prb[h * hrows:(h + 1) * hrows], values_of_head(h)) for h in range(H_B)]
        acc_s[...] = alpha * acc_s[...] + jnp.concatenate(pv, axis=0)
        m_s[...] = m_new

    @pl.when(st == 0)
    def _():
        q8 = q_ref[0]
        lane = lax.broadcasted_iota(jnp.int32, q8.shape, 1)
        qm[...] = jnp.concatenate([jnp.where(lane // DK_B == g, q8, 0.0) for g in range(2 * H_B)],
                                  axis=0).astype(qm.dtype)
        kself[...] = jnp.zeros(kself.shape, F32)
        vself[...] = jnp.zeros(vself.shape, F32)
        kself[0:SUBLANES, :] = kn_ref[0]
        vself[0:SUBLANES, :] = vn_ref[0]
        m_s[...] = jnp.full(m_s.shape, NEG, F32)
        l_s[...] = jnp.zeros(l_s.shape, F32)
        acc_s[...] = jnp.zeros(acc_s.shape, F32)
        rows = lax.broadcasted_iota(jnp.int32, (nrow, PAGE), 0) % SUBLANES
        cols = lax.broadcasted_iota(jnp.int32, (nrow, PAGE), 1)
        ok = (cols <= rows) & (cols < t_new)
        s = jnp.where(ok, _dot_nt(qm[...], kself[...].astype(BF16)) + self_ref[...], NEG)
        update(s, lambda h: vself[:, h * DV_B:(h + 1) * DV_B].astype(BF16))

    far = far_ref[:, 0:1]
    parts = []
    for g in range(npp):
        bias = far if g < npp - 1 else jnp.where(st == nst - 1, last_ref[...], far)
        parts.append(_dot(qm[...], kbuf[slot, g].astype(BF16)) + bias)

    def page_values(h):
        return jnp.concatenate([vbuf[slot, g, pl.ds(h, PAGE, stride=H_B), :] for g in range(npp)],
                               axis=0).astype(BF16)

    update(jnp.concatenate(parts, axis=1), page_values)

    @pl.when(st == nst - 1)
    def _():
        o = acc_s[...] / l_s[...]
        lam = _lambda(lam_ref, lam_init)
        for h in range(H_B):
            r0 = h * hrows
            o_ref[0, :, h * DV_B:(h + 1) * DV_B] = _sub_norm(o[r0:r0 + SUBLANES], o[r0 + SUBLANES:r0 + hrows], lam,
                                                             sn_ref[...], lam_init)


def _attn_sample(page_table, q8, kn8, vn8, ck_t, cv_i, last_t, self_t, far_col, lam4, sn, t_new, lam_init):
    bsz, npg = page_table.shape
    npp = math.gcd(PAGES_PER_STEP, npg)
    nrow = H_B * 2 * SUBLANES
    tok = pl.BlockSpec((1, SUBLANES, QB), lambda b, s, pt: (b, 0, 0))
    hbm = pl.BlockSpec(memory_space=pl.ANY)
    full = lambda a: pl.BlockSpec(a.shape, lambda b, s, pt: (0,) * a.ndim)
    return pl.pallas_call(
        functools.partial(_attn_sample_kernel, npp, t_new, lam_init),
        grid_spec=pltpu.PrefetchScalarGridSpec(
            num_scalar_prefetch=1,
            grid=(bsz, npg // npp),
            in_specs=[tok, tok, tok, hbm, hbm, full(last_t), full(self_t), full(far_col), full(lam4), full(sn)],
            out_specs=tok,
            scratch_shapes=[pltpu.VMEM((nrow, QB), BF16), pltpu.VMEM((PAGE, QB), F32), pltpu.VMEM((PAGE, VB), F32),
                            pltpu.VMEM((nrow, 1), F32), pltpu.VMEM((nrow, 1), F32), pltpu.VMEM((nrow, DV_B), F32),
                            pltpu.VMEM((2, npp, QB, PAGE), F32), pltpu.VMEM((2, npp, PAGE * H_B, DV_B), F32),
                            pltpu.SemaphoreType.DMA((2,))],
        ),
        out_shape=jax.ShapeDtypeStruct((bsz, SUBLANES, VB), F32),
        compiler_params=_cparams(("arbitrary", "arbitrary")),
        name="diff_attn_sample",
    )(page_table, q8, kn8, vn8, ck_t, cv_i, last_t, self_t, far_col, lam4, sn)


def _post_kernel(prompt_tiles, *refs):
    wa_ref, wb_ref, wo_ref, n2_ref, wr_ref, br_ref, x1_ref, hm_ref, ti_ref, tg_ref = refs[10:]
    is_sample = pl.program_id(0) >= prompt_tiles
    x, oa, ob, sga, sgb = (jnp.where(is_sample, s_ref[...], p_ref[...]) for p_ref, s_ref in zip(refs[:5], refs[5:10]))
    ya = _dot(oa.astype(BF16), wa_ref[...])
    yb = _dot(ob.astype(BF16), wb_ref[...])
    merged = sga.astype(F32) * ya + sgb.astype(F32) * yb
    x1 = x + _dot(merged.astype(BF16), wo_ref[...])
    x1_ref[...] = x1
    hm = x1 * lax.rsqrt(jnp.mean(x1 * x1, axis=-1, keepdims=True) + EPS) * n2_ref[...]
    hm_ref[...] = _pack_halves(hm)
    logits = _dot(hm.astype(BF16), wr_ref[...]) + br_ref[...]
    lane = lax.broadcasted_iota(jnp.int32, logits.shape, 1)
    lane_f = lane.astype(F32)
    work = jnp.where(lane < N_EXPERTS, logits, -jnp.inf)
    ti = jnp.zeros(logits.shape, F32)
    tg = jnp.zeros(logits.shape, F32)
    v0 = None
    den = None
    for k in range(TOP_K):
        vk = jnp.max(work, axis=-1, keepdims=True)
        ik = jnp.min(jnp.where(work == vk, lane_f, float(LANES)), axis=-1, keepdims=True)
        work = jnp.where(lane_f == ik, -jnp.inf, work)
        if k == 0:
            v0 = vk
        ek = jnp.exp(vk - v0)
        den = ek if k == 0 else den + ek
        ti = jnp.where(lane == k, ik, ti)
        tg = jnp.where(lane == k, ek, tg)
    ti_ref[...] = ti.astype(jnp.int32)
    tg_ref[...] = tg / den


def _post(prompt_in, sample_in, dense, n_prompt, n_sample, prompt_map):
    d = prompt_in[0].shape[1]
    tm = POST_TILE
    tp = n_prompt // tm
    n_total = n_prompt + n_sample
    widths = (d, VA, VB, d, d)
    full = lambda a: pl.BlockSpec(a.shape, lambda i: (0,) * a.ndim)
    p_spec = lambda k, c: pl.BlockSpec((tm, c), lambda i: ((jnp.minimum(i, tp - 1) if k == 0
                                                             else prompt_map(jnp.minimum(i, tp - 1))), 0))
    s_spec = lambda c: pl.BlockSpec((tm, c), lambda i: (jnp.maximum(i - tp, 0), 0))
    rout = lambda c: pl.BlockSpec((tm, c), lambda i: (i, 0))
    return pl.pallas_call(
        functools.partial(_post_kernel, tp),
        grid=(n_total // tm,),
        in_specs=[p_spec(k, c) for k, c in enumerate(widths)] + [s_spec(c) for c in widths]
                 + [full(a) for a in dense],
        out_specs=[rout(d), rout(d // 2), rout(LANES), rout(LANES)],
        out_shape=[jax.ShapeDtypeStruct((n_total, d), F32), jax.ShapeDtypeStruct((n_total, d // 2), jnp.uint32),
                   jax.ShapeDtypeStruct((n_total, LANES), jnp.int32), jax.ShapeDtypeStruct((n_total, LANES), F32)],
        compiler_params=_cparams(("parallel",)),
        name="merge_outproj_router",
    )(*prompt_in, *sample_in, *dense)


def _rank_kernel(ti_ref, rank_ref, cnt_ref, base):
    i = pl.program_id(0)
    tm = ti_ref.shape[0]

    @pl.when(i == 0)
    def _():
        base[...] = jnp.zeros(base.shape, F32)

    ti = ti_ref[...].astype(F32)
    lane = lax.broadcasted_iota(jnp.int32, ti.shape, 1)
    lane_f = lane.astype(F32)
    sel = [jnp.sum(jnp.where(lane == k, ti, 0.0), axis=-1, keepdims=True) for k in range(TOP_K)]
    onehot = jnp.zeros(ti.shape, F32)
    for k in range(TOP_K):
        onehot = onehot + (lane_f == sel[k]).astype(F32)
    ri = lax.broadcasted_iota(jnp.int32, (tm, tm), 0)
    ci = lax.broadcasted_iota(jnp.int32, (tm, tm), 1)
    before = _dot((ri > ci).astype(BF16), onehot.astype(BF16)) + base[...]
    rank = jnp.zeros(ti.shape, jnp.int32)
    for k in range(TOP_K):
        rk = jnp.sum(jnp.where(lane_f == sel[k], before, 0.0), axis=-1, keepdims=True)
        rank = jnp.where(lane == k, rk.astype(jnp.int32), rank)
    rank_ref[...] = rank
    base[...] = base[...] + jnp.sum(onehot, axis=0, keepdims=True)
    cnt_ref[...] = base[...]


def _rank(ti):
    n = ti.shape[0]
    tm = ROW_TILE
    return pl.pallas_call(
        _rank_kernel,
        grid=(n // tm,),
        in_specs=[pl.BlockSpec((tm, LANES), lambda i: (i, 0))],
        out_specs=[pl.BlockSpec((tm, LANES), lambda i: (i, 0)), pl.BlockSpec((1, LANES), lambda i: (0, 0))],
        out_shape=[jax.ShapeDtypeStruct((n, LANES), jnp.int32), jax.ShapeDtypeStruct((1, LANES), F32)],
        scratch_shapes=[pltpu.VMEM((1, LANES), F32)],
        compiler_params=_cparams(("arbitrary",)),
        name="moe_rank",
    )(ti)


def _dispatch_kernel(dest_ref, hm_ref, xs_in, xs_out, sem):
    del xs_in
    tm = hm_ref.shape[0]

    def row_copy(r, d):
        return pltpu.make_async_copy(hm_ref.at[pl.ds(r, 1)], xs_out.at[pl.ds(d, 1)], sem)

    def issue(r, c):
        for k in range(TOP_K):
            row_copy(r, dest_ref[0, 0, r * TOP_K + k]).start(priority=k % 2)
        return c

    def drain(r, c):
        for k in range(TOP_K):
            row_copy(r, dest_ref[0, 0, r * TOP_K + k]).wait()
        return c

    lax.fori_loop(0, tm, issue, 0)
    lax.fori_loop(0, tm, drain, 0)


def _dispatch(dest3, hm, xs_zero):
    n, d = hm.shape
    tm = ROW_TILE
    return pl.pallas_call(
        _dispatch_kernel,
        grid=(n // tm,),
        in_specs=[pl.BlockSpec((1, 1, tm * TOP_K), lambda i: (i, 0, 0), memory_space=pltpu.SMEM),
                  pl.BlockSpec((tm, d), lambda i: (i, 0)),
                  pl.BlockSpec(memory_space=pl.ANY)],
        out_specs=pl.BlockSpec(memory_space=pl.ANY),
        out_shape=jax.ShapeDtypeStruct(xs_zero.shape, xs_zero.dtype),
        scratch_shapes=[pltpu.SemaphoreType.DMA],
        input_output_aliases={2: 0},
        compiler_params=_cparams(("arbitrary",)),
        name="moe_dispatch",
    )(dest3, hm, xs_zero)


def _expert_kernel(be_ref, bv_ref, xs_ref, wgu_ref, bgu_ref, wd_ref, bd_ref, y_ref, wgu_bf, wd_bf):
    i = pl.program_id(0)
    d_ff = wd_ref.shape[1]

    @pl.when((i == 0) | (be_ref[i] != be_ref[jnp.maximum(i - 1, 0)]))
    def _():
        wgu_bf[...] = wgu_ref[0].astype(BF16)
        wd_bf[...] = wd_ref[0].astype(BF16)

    @pl.when(bv_ref[i] > 0)
    def _():
        x = _unpack_halves(xs_ref[...]).astype(BF16)
        hgu = _dot(x, wgu_bf[...]) + bgu_ref[0]
        gate = jnp.minimum(hgu[:, :d_ff], SWIGLU_LIMIT)
        up = jnp.clip(hgu[:, d_ff:], -SWIGLU_LIMIT, SWIGLU_LIMIT)
        act = (up + 1.0) * gate * _sigmoid(SWIGLU_ALPHA * gate)
        y_ref[...] = _pack_halves(_dot(act.astype(BF16), wd_bf[...]) + bd_ref[0])

    @pl.when(bv_ref[i] == 0)
    def _():
        y_ref[...] = jnp.zeros(y_ref.shape, y_ref.dtype)


def _experts(block_expert, block_valid, xs, w_gu, b_gu3, w_down, b_down3):
    n_slots, dp = xs.shape
    n_blocks = n_slots // MOE_BLK
    d_ff, d = w_down.shape[1:]
    assert dp * 2 == d
    return pl.pallas_call(
        _expert_kernel,
        grid_spec=pltpu.PrefetchScalarGridSpec(
            num_scalar_prefetch=2,
            grid=(n_blocks,),
            in_specs=[pl.BlockSpec((MOE_BLK, dp), lambda i, be, bv: (i, 0)),
                      pl.BlockSpec((1, d, 2 * d_ff), lambda i, be, bv: (be[i], 0, 0)),
                      pl.BlockSpec((1, 1, 2 * d_ff), lambda i, be, bv: (be[i], 0, 0)),
                      pl.BlockSpec((1, d_ff, d), lambda i, be, bv: (be[i], 0, 0)),
                      pl.BlockSpec((1, 1, d), lambda i, be, bv: (be[i], 0, 0))],
            out_specs=pl.BlockSpec((MOE_BLK, dp), lambda i, be, bv: (i, 0)),
            scratch_shapes=[pltpu.VMEM((d, 2 * d_ff), BF16), pltpu.VMEM((d_ff, d), BF16)],
        ),
        out_shape=jax.ShapeDtypeStruct((n_slots, dp), jnp.uint32),
        compiler_params=_cparams(("arbitrary",)),
        name="moe_experts",
    )(block_expert, block_valid, xs, w_gu, b_gu3, w_down, b_down3)


def _combine_kernel(dest_ref, yb_ref, tg_ref, x1_ref, y_ref, buf, sem):
    tm = x1_ref.shape[0]

    def row_copy(r, k, d):
        return pltpu.make_async_copy(yb_ref.at[pl.ds(d, 1)], buf.at[k, pl.ds(r, 1)], sem)

    def issue(r, c):
        for k in range(TOP_K):
            row_copy(r, k, dest_ref[0, 0, r * TOP_K + k]).start(priority=k % 2)
        return c

    def drain(r, c):
        for k in range(TOP_K):
            row_copy(r, k, dest_ref[0, 0, r * TOP_K + k]).wait()
        return c

    lax.fori_loop(0, tm, issue, 0)
    lax.fori_loop(0, tm, drain, 0)
    tg = tg_ref[...]
    lane = lax.broadcasted_iota(jnp.int32, tg.shape, 1)
    acc = jnp.zeros(x1_ref.shape, F32)
    for k in range(TOP_K):
        gk = jnp.sum(jnp.where(lane == k, tg, 0.0), axis=-1, keepdims=True)
        acc = acc + gk * _unpack_halves(buf[k])
    y_ref[...] = x1_ref[...] + acc


def _combine(dest3, yb, tg, x1):
    n, d = x1.shape
    dp = yb.shape[1]
    tm = ROW_TILE
    return pl.pallas_call(
        _combine_kernel,
        grid=(n // tm,),
        in_specs=[pl.BlockSpec((1, 1, tm * TOP_K), lambda i: (i, 0, 0), memory_space=pltpu.SMEM),
                  pl.BlockSpec(memory_space=pl.ANY),
                  pl.BlockSpec((tm, LANES), lambda i: (i, 0)),
                  pl.BlockSpec((tm, d), lambda i: (i, 0))],
        out_specs=pl.BlockSpec((tm, d), lambda i: (i, 0)),
        out_shape=jax.ShapeDtypeStruct((n, d), F32),
        scratch_shapes=[pltpu.VMEM((TOP_K, tm, dp), jnp.uint32), pltpu.SemaphoreType.DMA],
        compiler_params=_cparams(("arbitrary",)),
        name="moe_combine",
    )(dest3, yb, tg, x1)


def _moe(hm, ti, tg, x1, w_gu, b_gu, w_down, b_down):
    n, d = hm.shape
    rank, cnt = _rank(ti)
    counts = cnt[0, :N_EXPERTS].astype(jnp.int32)
    padded = (counts + MOE_BLK - 1) // MOE_BLK * MOE_BLK
    pad_end = jnp.cumsum(padded)
    pad_start = pad_end - padded
    top_i = ti[:, :TOP_K]
    dest = (pad_start[top_i] + rank[:, :TOP_K]).astype(jnp.int32)
    dest3 = dest.reshape(n // ROW_TILE, 1, ROW_TILE * TOP_K)
    n_blocks = (n * TOP_K) // MOE_BLK + N_EXPERTS
    blk_start = jnp.arange(n_blocks, dtype=jnp.int32) * MOE_BLK
    block_expert = jnp.minimum(jnp.sum((pad_end[None, :] <= blk_start[:, None]).astype(jnp.int32), axis=1),
                               N_EXPERTS - 1)
    block_valid = (blk_start < pad_end[-1]).astype(jnp.int32)
    xs = _dispatch(dest3, hm, jnp.zeros((n_blocks * MOE_BLK, d), hm.dtype))
    yb = _experts(block_expert, block_valid, xs, w_gu, b_gu.reshape(N_EXPERTS, 1, -1), w_down,
                  b_down.reshape(N_EXPERTS, 1, -1))
    return _combine(dest3, yb, tg, x1)


def _pad_lanes(v, fill=0.0):
    v = v.reshape(1, -1).astype(F32)
    return jnp.pad(v, ((0, 0), (0, LANES - v.shape[1])), constant_values=fill)


def kernel(x_prompt, x_sample, cache_k, cache_v, state_ssm, state_conv, page_table, meta_tokens, rel_bias_table, norm1, w_in, conv_w, a_log, dt_bias, gdn_norm, q_norm, k_norm, lam_q1, lam_k1, lam_q2, lam_k2, sub_norm, w_br_a, w_br_b, w_out, norm2, w_router, b_router, w_gu, b_gu, w_down, b_down):
    bsz, seq, d = x_prompt.shape
    dbs, dseq, _ = x_sample.shape
    depth = w_in.shape[0]
    assert depth == 1 and dseq <= SUBLANES - (CONV_W - 1) and seq % ATT_BLK == 0
    lam_init = 0.8 - 0.6 * math.exp(-0.3 * 0)
    l = 0

    w = w_in[l]
    o_z = CONV_DIM + VA
    w_main = jnp.concatenate([w[:, :o_z], w[:, o_z + 2 * H_A:]], axis=1).astype(BF16)
    w_ba = jnp.pad(w[:, o_z:o_z + 2 * H_A], ((0, 0), (0, LANES - 2 * H_A))).astype(BF16)
    grp = np.arange(QB) // DK_B
    gmat = jnp.asarray((grp[:, None] == grp[None, :]).astype(np.float32) / DK_B, dtype=BF16)
    qn_t = jnp.tile(q_norm[l].astype(F32), QB // DK_B).reshape(1, QB)
    kn_t = jnp.tile(k_norm[l].astype(F32), QB // DK_B).reshape(1, QB)
    alog_p = jnp.pad(a_log[l].astype(F32), (H_A, LANES - 2 * H_A)).reshape(1, LANES)
    dtb_p = jnp.pad(dt_bias[l].astype(F32), (H_A, LANES - 2 * H_A)).reshape(1, LANES)
    n1 = norm1[l].reshape(1, d).astype(F32)
    proj = functools.partial(_inproj, n1=n1, w_main=w_main, w_ba=w_ba, gmat=gmat, qn_t=qn_t, kn_t=kn_t,
                             alog_p=alog_p, dtb_p=dtb_p)
    lam4 = jnp.stack([lam_q1[l], lam_k1[l], lam_q2[l], lam_k2[l]]).astype(F32)
    sn = sub_norm[l].reshape(1, DV_B).astype(F32)
    gn = gdn_norm[l].reshape(1, DV_A).astype(F32)
    cw = conv_w[l].astype(F32)

    ii = np.arange(ATT_BLK)[:, None]
    jj = np.arange(LANES)[None, :]
    bk_prompt = np.stack([_bucket_np(ii - jj), _bucket_np(ATT_BLK + ii - jj)])
    t8 = np.arange(SUBLANES)[:, None]
    bk_sample = np.stack([_bucket_np(PAGE + t8 - jj), _bucket_np(t8 - jj)])
    bt_prompt = _bias_tiles(rel_bias_table.astype(F32), bk_prompt)
    bt_sample = _bias_tiles(rel_bias_table.astype(F32), bk_sample)
    far_h = rel_bias_table[N_BUCKETS - 1].astype(F32)
    far_p = jnp.broadcast_to(far_h[:, None, None], (H_B, 1, ATT_KV))
    diag_t = jnp.where(jnp.asarray(ii >= jj), bt_prompt[:, 0], NEG)
    near_t = bt_prompt[:, 1]
    far_t = jnp.broadcast_to(far_h[:, None, None], near_t.shape)
    neg_t = jnp.full(near_t.shape, NEG, F32)
    tile2 = lambda a, b_, c_, d_: jnp.concatenate([jnp.concatenate([a, b_], axis=2),
                                                   jnp.concatenate([c_, d_], axis=2)], axis=1)
    bt_prompt = jnp.stack([tile2(diag_t, neg_t, near_t, diag_t), tile2(far_t, near_t, far_t, far_t)], axis=1)
    nrow = H_B * 2 * SUBLANES
    rows_of = lambda t: jnp.broadcast_to(t[:, None], (H_B, 2, SUBLANES, LANES)).reshape(nrow, LANES)
    last_t, self_t = rows_of(bt_sample[:, 0]), rows_of(bt_sample[:, 1])
    far_col = jnp.broadcast_to(far_h[:, None, None], (H_B, 2 * SUBLANES, LANES)).reshape(nrow, LANES)

    lreal = N_META + seq
    lp = -(-lreal // ATT_KV) * ATT_KV
    fp = lp - lreal
    assert fp % SUBLANES == 0 and fp >= GDN_CHUNK and (fp + N_META) % ATT_BLK == 0 and lp % GDN_CHUNK == 0
    xp = jnp.concatenate([jnp.zeros((bsz, fp, d), F32),
                          jnp.broadcast_to(meta_tokens.astype(F32)[None], (bsz, N_META, d)), x_prompt], axis=1)
    conv_p, z_p, bg_p, qd_p, kd_p, vd_p, sga_p, sgb_p, kdb_p, vdb_p = proj(xp.reshape(bsz * lp, d), ATT_KV)
    r3 = lambda a, b_, r_: a.reshape(b_, r_, a.shape[-1])
    oa_p, ssm_p = _gdn(r3(conv_p, bsz, lp), r3(bg_p, bsz, lp), r3(z_p, bsz, lp),
                       jnp.zeros((bsz, H_A, DK_A, DV_A), F32), cw, gn, fp, GDN_CHUNK, 1, H_A)
    kbias = jnp.where(jnp.arange(lp) < fp, NEG, 0.0).astype(F32).reshape(lp // ATT_KV, ATT_KV)
    ob_p = _attn_prompt(r3(qd_p, bsz, lp), r3(kdb_p, bsz, lp), r3(vdb_p, bsz, lp), bt_prompt, far_p, kbias,
                        lam4, sn, fp // ATT_Q, lam_init)

    ns = dbs * dseq
    conv_s, z_s, bg_s, qd_s, kd_s, vd_s, sga_s, sgb_s, _, _ = proj(x_sample.reshape(ns, d), min(ns, 256))
    cs = SUBLANES
    rs = 2 * cs
    fs = rs - dseq

    def chunk_rows(a, head=None):
        a = a.reshape(dbs, dseq, a.shape[-1]).astype(F32)
        parts = [jnp.zeros((dbs, fs - (0 if head is None else head.shape[1]), a.shape[-1]), F32)]
        if head is not None:
            parts.append(head.astype(F32))
        return jnp.concatenate(parts + [a], axis=1)

    oa_s, ssm_s = _gdn(chunk_rows(conv_s, state_conv[l]), chunk_rows(bg_s), chunk_rows(z_s),
                       state_ssm[l].astype(F32), cw, gn, fs, cs, math.gcd(dbs, 4), H_A)
    pad8 = lambda a: jnp.pad(a.reshape(dbs, dseq, a.shape[-1]), ((0, 0), (0, SUBLANES - dseq), (0, 0)))
    n_pool = cache_k.shape[1]
    ck_t = jnp.transpose(cache_k[l], (0, 2, 3, 4, 1)).reshape(n_pool, QB, PAGE)
    cv_i = cache_v[l].reshape(n_pool, PAGE * H_B, DV_B)
    ob_s = _attn_sample(page_table, pad8(qd_s).astype(F32), pad8(kd_s), pad8(vd_s), ck_t, cv_i,
                        last_t, self_t, far_col, lam4, sn, dseq, lam_init)

    wa, wb, wo = w_br_a[l].astype(BF16), w_br_b[l].astype(BF16), w_out[l].astype(BF16)
    n2 = norm2[l].reshape(1, d).astype(F32)
    wr = jnp.pad(w_router[l], ((0, 0), (0, LANES - N_EXPERTS))).astype(BF16)
    br = _pad_lanes(b_router[l])
    tiles_b = seq // POST_TILE
    tiles_lp = lp // POST_TILE
    skip = (fp + N_META) // POST_TILE
    npt = bsz * seq
    assert ns % POST_TILE == 0 and seq % POST_TILE == 0 and (fp + N_META) % POST_TILE == 0
    oa_s2 = oa_s[:, fs:].reshape(ns, VA)
    ob_s2 = ob_s[:, :dseq].reshape(ns, VB)
    x1, hm, ti, tg = _post(
        (x_prompt.reshape(npt, d), oa_p.reshape(bsz * lp, VA), ob_p.reshape(bsz * lp, VB), sga_p, sgb_p),
        (x_sample.reshape(ns, d), oa_s2, ob_s2, sga_s, sgb_s), (wa, wb, wo, n2, wr, br), npt, ns,
        lambda i: (i // tiles_b) * tiles_lp + skip + i % tiles_b)

    y = _moe(hm, ti, tg, x1, w_gu[l], b_gu[l], w_down[l], b_down[l])
    y_prompt = y[:npt].reshape(bsz, seq, d)
    y_sample = y[npt:].reshape(dbs, dseq, d)

    k_prompt = r3(kd_p, bsz, lp)[:, fp:].reshape(1, bsz, lreal, H_B, 2, DK_B)
    v_prompt = r3(vd_p, bsz, lp)[:, fp:].reshape(1, bsz, lreal, H_B, DV_B)
    conv_prompt = r3(conv_p, bsz, lp)[:, lp - (CONV_W - 1):][None]
    xpad_s = jnp.concatenate([state_conv[l].astype(F32), conv_s.reshape(dbs, dseq, CONV_DIM)], axis=1)
    conv_sample = xpad_s[:, dseq:][None]
    return (y_prompt, y_sample, k_prompt, v_prompt, ssm_p[None], conv_prompt,
            kd_s.reshape(1, dbs, dseq, H_B, 2, DK_B), vd_s.reshape(1, dbs, dseq, H_B, DV_B), ssm_s[None], conv_sample)
```

```python
import functools
import math

import numpy as np
import jax
import jax.numpy as jnp
from jax import lax
from jax.experimental import pallas as pl
from jax.experimental.pallas import tpu as pltpu
from jax.experimental.pallas import tpu_sc as plsc

F32 = jnp.float32
BF16 = jnp.bfloat16
HI = lax.Precision.HIGHEST

N_META = 16
H_A, DK_A, DV_A = 4, 128, 128
CONV_W = 4
H_B, DK_B = 4, 64
DV_B = 2 * DK_B
N_BUCKETS, MAX_DIST = 32, 128
N_EXPERTS, TOP_K = 32, 4
SWIGLU_LIMIT, SWIGLU_ALPHA = 7.0, 1.702
EPS = 1e-6
PAGE = 128
QA = H_A * DK_A
VA = H_A * DV_A
QB = H_B * 2 * DK_B
VB = H_B * DV_B
CONV_DIM = 2 * QA + VA

LANES = 128
SUBLANES = 8
VMEM_LIMIT = 56 * 1024 * 1024
SC_CORES, SC_SUBCORES, SC_WINDOW = 2, 16, 64

GDN_CHUNK = 64
ATT_BLK = 128
ATT_KV = 2 * ATT_BLK
ATT_Q = 2 * ATT_BLK
PAGES_PER_STEP = 16
MOE_BLK = 256
ROW_TILE = 128
POST_TILE = 256
NEG = -1e30


def _dot(a, b, prec=None):
    return jnp.dot(a, b, preferred_element_type=F32, precision=prec)


def _dot_nt(a, b, prec=None):
    return lax.dot_general(a, b, (((1,), (1,)), ((), ())), preferred_element_type=F32, precision=prec)


def _dot_tn(a, b, prec=None):
    return lax.dot_general(a, b, (((0,), (0,)), ((), ())), preferred_element_type=F32, precision=prec)


def _split(a):
    hi = a.astype(BF16)
    return hi, (a - hi.astype(F32)).astype(BF16)


def _lhs3(a):
    hi, lo = _split(a)
    return jnp.concatenate([hi, lo, hi], axis=1)


def _rhs3(b):
    hi, lo = _split(b)
    return jnp.concatenate([hi, hi, lo], axis=0)


def _pack_halves(x):
    w = x.shape[1] // 2
    bits = lambda v: pltpu.bitcast(v.astype(BF16).astype(F32), jnp.uint32)
    return (bits(x[:, :w]) >> 16) | (bits(x[:, w:]) & jnp.uint32(0xFFFF0000))


def _unpack_halves(u):
    lo = pltpu.bitcast(u << 16, F32)
    hi = pltpu.bitcast(u & jnp.uint32(0xFFFF0000), F32)
    return jnp.concatenate([lo, hi], axis=1)


def _sigmoid(x):
    return 1.0 / (1.0 + jnp.exp(-x))


def _cparams(sem, flags=None):
    return pltpu.CompilerParams(dimension_semantics=sem, vmem_limit_bytes=VMEM_LIMIT, flags=flags)


def _bucket_np(d):
    d = np.maximum(d, 0)
    df = np.maximum(d, 1).astype(np.float32)
    max_exact = N_BUCKETS // 2
    large = max_exact + (np.log(df / np.float32(max_exact)) / np.float32(math.log(MAX_DIST / max_exact))
                         * np.float32(N_BUCKETS - max_exact)).astype(np.int32)
    return np.where(d < max_exact, d, np.minimum(large, N_BUCKETS - 1)).astype(np.int32)


def _bias_kernel(tab_ref, bk_ref, o_ref):
    for t in range(bk_ref.shape[0]):
        bk = bk_ref[t]
        for h in range(H_B):
            acc = jnp.zeros(bk.shape, F32)
            for b in range(N_BUCKETS):
                acc = jnp.where(bk == b, tab_ref[b, h], acc)
            o_ref[h, t] = acc


def _bias_tiles(table, buckets):
    t, r, _ = buckets.shape
    return pl.pallas_call(
        _bias_kernel,
        out_shape=jax.ShapeDtypeStruct((H_B, t, r, LANES), F32),
        in_specs=[pl.BlockSpec(memory_space=pltpu.SMEM), pl.BlockSpec(memory_space=pltpu.VMEM)],
        out_specs=pl.BlockSpec(memory_space=pltpu.VMEM),
        name="rel_bias_tiles",
    )(table, jnp.asarray(buckets))


def _inproj_kernel(x_ref, n1_ref, w_ref, wba_ref, gm_ref, qn_ref, kn_ref, alog_ref, dtb_ref,
                   conv_ref, z_ref, bg_ref, qd_ref, kd_ref, vd_ref, sga_ref, sgb_ref, kdb_ref, vdb_ref):
    x = x_ref[...]
    ms = jnp.mean(x * x, axis=-1, keepdims=True)
    h = (x * lax.rsqrt(ms + EPS) * n1_ref[...]).astype(BF16)
    o = 0
    conv_ref[...] = _dot(h, w_ref[:, o:o + CONV_DIM]); o += CONV_DIM
    z_ref[...] = _dot(h, w_ref[:, o:o + VA]).astype(z_ref.dtype); o += VA

    def group_norm(y, g):
        sq = y * y
        hi = sq.astype(BF16)
        lo = (sq - hi.astype(F32)).astype(BF16)
        msq = _dot(hi, gm_ref[...]) + _dot(lo, gm_ref[...])
        return y * lax.rsqrt(msq + EPS) * g

    qb = _dot(h, w_ref[:, o:o + QB]); o += QB
    qd_ref[...] = (group_norm(qb, qn_ref[...]) * (DK_B ** -0.5)).astype(qd_ref.dtype)
    kb = _dot(h, w_ref[:, o:o + QB]); o += QB
    kd = group_norm(kb, kn_ref[...])
    kd_ref[...] = kd
    kdb_ref[...] = kd.astype(kdb_ref.dtype)
    vd = _dot(h, w_ref[:, o:o + VB]); o += VB
    vd_ref[...] = vd
    vdb_ref[...] = vd.astype(vdb_ref.dtype)
    d_model = x.shape[1]
    sga_ref[...] = _sigmoid(_dot(h, w_ref[:, o:o + d_model])).astype(sga_ref.dtype); o += d_model
    sgb_ref[...] = _sigmoid(_dot(h, w_ref[:, o:o + d_model])).astype(sgb_ref.dtype)
    t = _dot(h, wba_ref[...])
    lane = lax.broadcasted_iota(jnp.int32, t.shape, 1)
    ta = t + dtb_ref[...]
    sp = jnp.maximum(ta, 0.0) + jnp.log(1.0 + jnp.exp(-jnp.abs(ta)))
    bg_ref[...] = jnp.where(lane < H_A, _sigmoid(t), -jnp.exp(alog_ref[...]) * sp)


def _inproj(x2d, tm, n1, w_main, w_ba, gmat, qn_t, kn_t, alog_p, dtb_p):
    n, d = x2d.shape
    assert n % tm == 0
    row = lambda c: pl.BlockSpec((tm, c), lambda i: (i, 0))
    full = lambda a: pl.BlockSpec(a.shape, lambda i: (0,) * a.ndim)
    outs = [(CONV_DIM, F32), (VA, BF16), (LANES, F32), (QB, BF16), (QB, F32), (VB, F32), (d, BF16), (d, BF16),
            (QB, BF16), (VB, BF16)]
    return pl.pallas_call(
        _inproj_kernel,
        grid=(n // tm,),
        in_specs=[row(d)] + [full(a) for a in (n1, w_main, w_ba, gmat, qn_t, kn_t, alog_p, dtb_p)],
        out_specs=[row(c) for c, _ in outs],
        out_shape=[jax.ShapeDtypeStruct((n, c), dt) for c, dt in outs],
        compiler_params=_cparams(("parallel",)),
        name="in_proj",
    )(x2d, n1, w_main, w_ba, gmat, qn_t, kn_t, alog_p, dtb_p)


def _gdn_kernel(fv, c, q_ref, k_ref, v_ref, cwq_ref, cwk_ref, cwv_ref, bg_ref, z_ref, s0_ref, gn_ref,
                o_ref, st_ref):
    nb, r, wd = q_ref.shape
    hg = wd // LANES
    head0 = pl.program_id(1) * hg
    chains = [(b, hh) for b in range(nb) for hh in range(hg)]

    ri = lax.broadcasted_iota(jnp.int32, (c, c), 0)
    ci = lax.broadcasted_iota(jnp.int32, (c, c), 1)
    incl = ri >= ci
    strict = ri > ci
    eye = (ri == ci).astype(F32)
    lane = lax.broadcasted_iota(jnp.int32, (c, LANES), 1)
    sub_t = lax.broadcasted_iota(jnp.int32, (LANES, c), 0)
    rowid = lax.broadcasted_iota(jnp.int32, (c, 1), 0)
    colid = lax.broadcasted_iota(jnp.int32, (1, c), 1)

    def conv(x_ref, cw_ref, b, cols, r0):
        w = x_ref[b, pl.ds(r0 - SUBLANES, c + SUBLANES), cols]
        acc = w[SUBLANES - 3:SUBLANES - 3 + c] * cw_ref[0:1, cols]
        for j in range(1, CONV_W):
            acc = acc + w[SUBLANES - 3 + j:SUBLANES - 3 + j + c] * cw_ref[j:j + 1, cols]
        return acc * _sigmoid(acc)

    def elementwise(b, hh, r0):
        cols = slice(hh * LANES, (hh + 1) * LANES)
        head = head0 + hh
        vcol = ((r0 + rowid) >= fv).astype(F32)
        vrow = ((r0 + colid) >= fv).astype(F32)
        qv = conv(q_ref, cwq_ref, b, cols, r0)
        kv = conv(k_ref, cwk_ref, b, cols, r0)
        v = conv(v_ref, cwv_ref, b, cols, r0) * vcol
        q = qv * lax.rsqrt(jnp.sum(qv * qv, axis=-1, keepdims=True) + EPS) * (DK_A ** -0.5) * vcol
        k = kv * lax.rsqrt(jnp.sum(kv * kv, axis=-1, keepdims=True) + EPS) * vcol
        bgc = bg_ref[b, pl.ds(r0, c), :]
        beta = jnp.sum(jnp.where(lane == head, bgc, 0.0), axis=-1, keepdims=True) * vcol
        g_col = jnp.sum(jnp.where(lane == H_A + head, bgc, 0.0), axis=-1, keepdims=True) * vcol
        g_row = jnp.sum(jnp.where(sub_t == H_A + head, bgc.T, 0.0), axis=0, keepdims=True) * vrow
        gc = jnp.sum(jnp.where(incl, g_row, 0.0), axis=-1, keepdims=True)
        gr = jnp.sum(jnp.where(ri <= ci, g_col, 0.0), axis=0, keepdims=True)
        decay = jnp.where(incl, jnp.exp(jnp.where(incl, gc - gr, 0.0)), 0.0)
        kb = k * beta
        eg = jnp.exp(gc)
        g_last = jnp.sum(jnp.where(rowid == c - 1, gc, 0.0), axis=0, keepdims=True)
        k_dt = (k * jnp.exp(g_last - gc)).T
        lhs = jnp.concatenate([kb, q], axis=0).astype(BF16)
        rhs = jnp.concatenate([v * beta, kb * eg], axis=-1)
        return lhs, k.astype(BF16), decay, rhs, q * eg, k_dt, jnp.exp(g_last)

    def prepare_a(j):
        r0 = pl.multiple_of(j * c, c)
        vec = [elementwise(b, hh, r0) for b, hh in chains]
        kk = [_dot_nt(v_[0], v_[1]) for v_ in vec]
        return vec, kk

    def prepare_b(vec, kk):
        low = [jnp.where(strict, kk_[:c] * v_[2], 0.0) for v_, kk_ in zip(vec, kk)]
        intra = [kk_[c:] * v_[2] for v_, kk_ in zip(vec, kk)]
        inv = [eye - lw for lw in low]
        levels = int(math.log2(c)) - 1
        pw = [_dot(_lhs3(lw), _rhs3(lw)) for lw in low]
        for lev in range(levels):
            pw_r = [_rhs3(p) for p in pw]
            if lev + 1 < levels:
                pw = [_dot(_lhs3(p), r_) for p, r_ in zip(pw, pw_r)]
            inv = [iv + _dot(_lhs3(iv), r_) for iv, r_ in zip(inv, pw_r)]
        sol = [_dot(_lhs3(iv), _rhs3(v_[3])) for iv, v_ in zip(inv, vec)]
        out = []
        for v_, sl, it in zip(vec, sol, intra):
            on_state = jnp.concatenate([sl[:, DV_A:], v_[4]], axis=0).astype(BF16)
            on_u = jnp.concatenate([it, v_[5]], axis=0).astype(BF16)
            out.append((sl[:, :DV_A], on_state, on_u, v_[6]))
        return tuple(out)

    def apply_a(prepared, states):
        return [_dot(p[1], s.astype(BF16)) for p, s in zip(prepared, states)]

    def apply_b(j, prepared, states, ps):
        r0 = pl.multiple_of(j * c, c)
        u = [p[0] - ps_[:c] for p, ps_ in zip(prepared, ps)]
        pu = [_dot(p[2], u_.astype(BF16)) for p, u_ in zip(prepared, u)]
        new_states = []
        for (b, hh), p, s, ps_, pu_ in zip(chains, prepared, states, ps, pu):
            cols = slice(hh * LANES, (hh + 1) * LANES)
            o = ps_[c:] + pu_[:c]
            on = o * lax.rsqrt(jnp.mean(o * o, axis=-1, keepdims=True) + EPS) * gn_ref[...]
            zc = z_ref[b, pl.ds(r0, c), cols].astype(F32)
            o_ref[b, pl.ds(r0, c), cols] = (on * (zc * _sigmoid(zc))).astype(o_ref.dtype)
            new_states.append(s * p[3] + pu_[c:])
        return tuple(new_states)

    def chunk(j, carry):
        prepared, states = carry
        ps = apply_a(prepared, states)
        vec, kk = prepare_a(j + 1)
        states = apply_b(j, prepared, states, ps)
        return prepare_b(vec, kk), states

    j0 = fv // c
    last = r // c - 1
    o_ref[:, 0:j0 * c, :] = jnp.zeros((nb, j0 * c, wd), o_ref.dtype)
    init = (prepare_b(*prepare_a(j0)), tuple(s0_ref[b, hh] for b, hh in chains))
    prepared, states = lax.fori_loop(j0, last, chunk, init)
    for (b, hh), s in zip(chains, apply_b(last, prepared, states, apply_a(prepared, states))):
        st_ref[b, hh] = s


def _gdn(conv3, bg3, z3, s0, conv_w, gn, fv, c, nb, hg):
    bsz, r, _ = conv3.shape
    assert r % c == 0 and fv // c >= 1 and c >= SUBLANES and bsz % nb == 0 and H_A % hg == 0
    ng = H_A // hg
    wd = hg * LANES
    blk = lambda off: pl.BlockSpec((nb, r, wd), lambda i, g: (i, 0, off * ng + g))
    cw = lambda off: pl.BlockSpec((CONV_W, wd), lambda i, g: (0, off * ng + g))
    st = pl.BlockSpec((nb, hg, DK_A, DV_A), lambda i, g: (i, g, 0, 0))
    return pl.pallas_call(
        functools.partial(_gdn_kernel, fv, c),
        grid=(bsz // nb, ng),
        in_specs=[blk(0), blk(1), blk(2), cw(0), cw(1), cw(2),
                  pl.BlockSpec((nb, r, LANES), lambda i, g: (i, 0, 0)), blk(0), st,
                  pl.BlockSpec((1, LANES), lambda i, g: (0, 0))],
        out_specs=[blk(0), st],
        out_shape=[jax.ShapeDtypeStruct((bsz, r, VA), F32), jax.ShapeDtypeStruct((bsz, H_A, DK_A, DV_A), F32)],
        compiler_params=_cparams(("parallel", "parallel")),
        name="gdn",
    )(conv3, conv3, conv3, conv_w, conv_w, conv_w, bg3, z3, s0, gn)


def _lambda(lam_ref, lam_init):
    l1 = jnp.sum(lam_ref[0:1, :] * lam_ref[1:2, :], axis=-1, keepdims=True)
    l2 = jnp.sum(lam_ref[2:3, :] * lam_ref[3:4, :], axis=-1, keepdims=True)
    return jnp.exp(l1) - jnp.exp(l2) + lam_init


def _sub_norm(o0, o1, lam, sn, lam_init):
    ob = o0 - lam * o1
    return ob * lax.rsqrt(jnp.mean(ob * ob, axis=-1, keepdims=True) + EPS) * sn * (1.0 - lam_init)


def _attn_prompt_kernel(first_q, lam_init, q_ref, k_ref, v_ref, bt_ref, far_ref, kb_ref, lam_ref, sn_ref, o_ref):
    qi = pl.program_id(2)

    @pl.when(qi < first_q)
    def _():
        o_ref[0] = jnp.zeros(o_ref.shape[1:], o_ref.dtype)

    @pl.when(qi >= first_q)
    def _():
        q = q_ref[0]
        lane = lax.broadcasted_iota(jnp.int32, q.shape, 1)
        zero = jnp.zeros_like(q)
        qs = jnp.concatenate([jnp.where(lane < DK_B, q, zero), jnp.where(lane >= DK_B, q, zero)], axis=0)

        def scores(j):
            k0 = pl.multiple_of(j * ATT_KV, ATT_KV)
            return _dot_nt(qs, k_ref[0, pl.ds(k0, ATT_KV), :])

        def step(j, carry, bias, last=False):
            m, l, acc, s_raw = carry
            s_next = s_raw if last else scores(j + 1)
            k0 = pl.multiple_of(j * ATT_KV, ATT_KV)
            s = s_raw + bias
            m_new = jnp.maximum(m, jnp.max(s, axis=-1, keepdims=True))
            alpha = jnp.exp(m - m_new)
            p = jnp.exp(s - m_new)
            l = alpha * l + jnp.sum(p, axis=-1, keepdims=True)
            acc = alpha * acc + _dot(p.astype(BF16), v_ref[0, pl.ds(k0, ATT_KV), :])
            return m_new, l, acc, s_next

        def tile_bias(j, which):
            t = bt_ref[0, which] + kb_ref[pl.ds(j, 1), :]
            return jnp.concatenate([t, t], axis=0)

        carry = (jnp.full((2 * ATT_Q, 1), NEG, F32), jnp.zeros((2 * ATT_Q, 1), F32),
                 jnp.zeros((2 * ATT_Q, DV_B), F32), scores(0))
        carry = lax.fori_loop(0, qi - 1, lambda j, c: step(j, c, far_ref[0] + kb_ref[pl.ds(j, 1), :]), carry)
        carry = lax.cond(qi >= 1, lambda c: step(qi - 1, c, tile_bias(qi - 1, 1)), lambda c: c, carry)
        _, l, acc, _ = step(qi, carry, tile_bias(qi, 0), last=True)
        o = acc / l
        o_ref[0] = _sub_norm(o[:ATT_Q], o[ATT_Q:], _lambda(lam_ref, lam_init), sn_ref[...],
                             lam_init).astype(o_ref.dtype)


def _attn_prompt(qd3, kd3, vd3, btiles, far, kbias, lam4, sn, first_q, lam_init):
    bsz, lp, _ = qd3.shape
    assert lp % ATT_KV == 0 and ATT_Q == ATT_KV
    kv = pl.BlockSpec((1, lp, LANES), lambda b, h, i: (b, 0, h))
    qo = pl.BlockSpec((1, ATT_Q, LANES), lambda b, h, i: (b, i, h))
    full = lambda a: pl.BlockSpec(a.shape, lambda b, h, i: (0,) * a.ndim)
    return pl.pallas_call(
        functools.partial(_attn_prompt_kernel, first_q, lam_init),
        grid=(bsz, H_B, lp // ATT_Q),
        in_specs=[qo, kv, kv,
                  pl.BlockSpec((1, 2, ATT_Q, ATT_KV), lambda b, h, i: (h, 0, 0, 0)),
                  pl.BlockSpec((1, 1, ATT_KV), lambda b, h, i: (h, 0, 0)),
                  full(kbias), full(lam4), full(sn)],
        out_specs=qo,
        out_shape=jax.ShapeDtypeStruct((bsz, lp, VB), F32),
        compiler_params=_cparams(("parallel", "parallel", "parallel")),
        name="diff_attn_prompt",
    )(qd3, kd3, vd3, btiles, far, kbias, lam4, sn)


def _attn_sample_kernel(npp, t_new, lam_init, pt_ref, q_ref, kn_ref, vn_ref, ck_ref, cv_ref, last_ref, self_ref,
                        far_ref, lam_ref, sn_ref, o_ref, qm, kself, vself, m_s, l_s, acc_s, kbuf, vbuf, sem):
    b = pl.program_id(0)
    st = pl.program_id(1)
    nb = pl.num_programs(0)
    nst = pl.num_programs(1)
    nrow = H_B * 2 * SUBLANES
    hrows = 2 * SUBLANES
    t = b * nst + st
    slot = t % 2

    def page_copies(bb, ss, sl):
        out = []
        for g in range(npp):
            page = pt_ref[bb, ss * npp + g]
            out.append(pltpu.make_async_copy(ck_ref.at[page], kbuf.at[sl, g], sem.at[sl]))
            out.append(pltpu.make_async_copy(cv_ref.at[page], vbuf.at[sl, g], sem.at[sl]))
        return out

    @pl.when(t == 0)
    def _():
        for cp in page_copies(0, 0, 0):
            cp.start()

    @pl.when(t + 1 < nb * nst)
    def _():
        wrap = st + 1 == nst
        for cp in page_copies(jnp.where(wrap, b + 1, b), jnp.where(wrap, 0, st + 1), 1 - slot):
            cp.start()

    for cp in page_copies(b, st, slot):
        cp.wait()

    def update(s, values_of_head):
        m_old = m_s[...]
        m_new = jnp.maximum(m_old, jnp.max(s, axis=-1, keepdims=True))
        alpha = jnp.exp(m_old - m_new)
        pr = jnp.exp(s - m_new)
        l_s[...] = alpha * l_s[...] + jnp.sum(pr, axis=-1, keepdims=True)
        prb = pr.astype(BF16)
        pv = [_dot(prb[h * hrows:(h + 1) * hrows], values_of_head(h)) for h in range(H_B)]
        acc_s[...] = alpha * acc_s[...] + jnp.concatenate(pv, axis=0)
        m_s[...] = m_new

    @pl.when(st == 0)
    def _():
        q8 = q_ref[0]
        lane = lax.broadcasted_iota(jnp.int32, q8.shape, 1)
        qm[...] = jnp.concatenate([jnp.where(lane // DK_B == g, q8, 0.0) for g in range(2 * H_B)],
                                  axis=0).astype(qm.dtype)
        kself[...] = jnp.zeros(kself.shape, F32)
        vself[...] = jnp.zeros(vself.shape, F32)
        kself[0:SUBLANES, :] = kn_ref[0]
        vself[0:SUBLANES, :] = vn_ref[0]
        m_s[...] = jnp.full(m_s.shape, NEG, F32)
        l_s[...] = jnp.zeros(l_s.shape, F32)
        acc_s[...] = jnp.zeros(acc_s.shape, F32)
        rows = lax.broadcasted_iota(jnp.int32, (nrow, PAGE), 0) % SUBLANES
        cols = lax.broadcasted_iota(jnp.int32, (nrow, PAGE), 1)
        ok = (cols <= rows) & (cols < t_new)
        s = jnp.where(ok, _dot_nt(qm[...], kself[...].astype(BF16)) + self_ref[...], NEG)
        update(s, lambda h: vself[:, h * DV_B:(h + 1) * DV_B].astype(BF16))

    far = far_ref[:, 0:1]
    parts = []
    for g in range(npp):
        bias = far if g < npp - 1 else jnp.where(st == nst - 1, last_ref[...], far)
        parts.append(_dot(qm[...], kbuf[slot, g].astype(BF16)) + bias)

    def page_values(h):
        return jnp.concatenate([vbuf[slot, g, pl.ds(h, PAGE, stride=H_B), :] for g in range(npp)],
                               axis=0).astype(BF16)

    update(jnp.concatenate(parts, axis=1), page_values)

    @pl.when(st == nst - 1)
    def _():
        o = acc_s[...] / l_s[...]
        lam = _lambda(lam_ref, lam_init)
        for h in range(H_B):
            r0 = h * hrows
            o_ref[0, :, h * DV_B:(h + 1) * DV_B] = _sub_norm(o[r0:r0 + SUBLANES], o[r0 + SUBLANES:r0 + hrows], lam,
                                                             sn_ref[...], lam_init)


def _attn_sample(page_table, q8, kn8, vn8, ck_t, cv_i, last_t, self_t, far_col, lam4, sn, t_new, lam_init):
    bsz, npg = page_table.shape
    npp = math.gcd(PAGES_PER_STEP, npg)
    nrow = H_B * 2 * SUBLANES
    tok = pl.BlockSpec((1, SUBLANES, QB), lambda b, s, pt: (b, 0, 0))
    hbm = pl.BlockSpec(memory_space=pl.ANY)
    full = lambda a: pl.BlockSpec(a.shape, lambda b, s, pt: (0,) * a.ndim)
    return pl.pallas_call(
        functools.partial(_attn_sample_kernel, npp, t_new, lam_init),
        grid_spec=pltpu.PrefetchScalarGridSpec(
            num_scalar_prefetch=1,
            grid=(bsz, npg // npp),
            in_specs=[tok, tok, tok, hbm, hbm, full(last_t), full(self_t), full(far_col), full(lam4), full(sn)],
            out_specs=tok,
            scratch_shapes=[pltpu.VMEM((nrow, QB), BF16), pltpu.VMEM((PAGE, QB), F32), pltpu.VMEM((PAGE, VB), F32),
                            pltpu.VMEM((nrow, 1), F32), pltpu.VMEM((nrow, 1), F32), pltpu.VMEM((nrow, DV_B), F32),
                            pltpu.VMEM((2, npp, QB, PAGE), F32), pltpu.VMEM((2, npp, PAGE * H_B, DV_B), F32),
                            pltpu.SemaphoreType.DMA((2,))],
        ),
        out_shape=jax.ShapeDtypeStruct((bsz, SUBLANES, VB), F32),
        compiler_params=_cparams(("arbitrary", "arbitrary")),
        name="diff_attn_sample",
    )(page_table, q8, kn8, vn8, ck_t, cv_i, last_t, self_t, far_col, lam4, sn)


def _post_kernel(prompt_tiles, *refs):
    wa_ref, wb_ref, wo_ref, n2_ref, wr_ref, br_ref, x1_ref, hm_ref, ti_ref, tg_ref = refs[10:]
    is_sample = pl.program_id(0) >= prompt_tiles
    x, oa, ob, sga, sgb = (jnp.where(is_sample, s_ref[...], p_ref[...]) for p_ref, s_ref in zip(refs[:5], refs[5:10]))
    ya = _dot(oa.astype(BF16), wa_ref[...])
    yb = _dot(ob.astype(BF16), wb_ref[...])
    merged = sga.astype(F32) * ya + sgb.astype(F32) * yb
    x1 = x + _dot(merged.astype(BF16), wo_ref[...])
    x1_ref[...] = x1
    hm = x1 * lax.rsqrt(jnp.mean(x1 * x1, axis=-1, keepdims=True) + EPS) * n2_ref[...]
    hm_ref[...] = _pack_halves(hm)
    logits = _dot(hm.astype(BF16), wr_ref[...]) + br_ref[...]
    lane = lax.broadcasted_iota(jnp.int32, logits.shape, 1)
    lane_f = lane.astype(F32)
    work = jnp.where(lane < N_EXPERTS, logits, -jnp.inf)
    ti = jnp.zeros(logits.shape, F32)
    tg = jnp.zeros(logits.shape, F32)
    v0 = None
    den = None
    for k in range(TOP_K):
        vk = jnp.max(work, axis=-1, keepdims=True)
        ik = jnp.min(jnp.where(work == vk, lane_f, float(LANES)), axis=-1, keepdims=True)
        work = jnp.where(lane_f == ik, -jnp.inf, work)
        if k == 0:
            v0 = vk
        ek = jnp.exp(vk - v0)
        den = ek if k == 0 else den + ek
        ti = jnp.where(lane == k, ik, ti)
        tg = jnp.where(lane == k, ek, tg)
    ti_ref[...] = ti.astype(jnp.int32)
    tg_ref[...] = tg / den


def _post(prompt_in, sample_in, dense, n_prompt, n_sample, prompt_map):
    d = prompt_in[0].shape[1]
    tm = POST_TILE
    tp = n_prompt // tm
    n_total = n_prompt + n_sample
    widths = (d, VA, VB, d, d)
    full = lambda a: pl.BlockSpec(a.shape, lambda i: (0,) * a.ndim)
    p_spec = lambda k, c: pl.BlockSpec((tm, c), lambda i: ((jnp.minimum(i, tp - 1) if k == 0
                                                             else prompt_map(jnp.minimum(i, tp - 1))), 0))
    s_spec = lambda c: pl.BlockSpec((tm, c), lambda i: (jnp.maximum(i - tp, 0), 0))
    rout = lambda c: pl.BlockSpec((tm, c), lambda i: (i, 0))
    return pl.pallas_call(
        functools.partial(_post_kernel, tp),
        grid=(n_total // tm,),
        in_specs=[p_spec(k, c) for k, c in enumerate(widths)] + [s_spec(c) for c in widths]
                 + [full(a) for a in dense],
        out_specs=[rout(d), rout(d // 2), rout(LANES), rout(LANES)],
        out_shape=[jax.ShapeDtypeStruct((n_total, d), F32), jax.ShapeDtypeStruct((n_total, d // 2), jnp.uint32),
                   jax.ShapeDtypeStruct((n_total, LANES), jnp.int32), jax.ShapeDtypeStruct((n_total, LANES), F32)],
        compiler_params=_cparams(("parallel",)),
        name="merge_outproj_router",
    )(*prompt_in, *sample_in, *dense)


def _rank_kernel(ti_ref, rank_ref, cnt_ref, base):
    i = pl.program_id(0)
    tm = ti_ref.shape[0]

    @pl.when(i == 0)
    def _():
        base[...] = jnp.zeros(base.shape, F32)

    ti = ti_ref[...].astype(F32)
    lane = lax.broadcasted_iota(jnp.int32, ti.shape, 1)
    lane_f = lane.astype(F32)
    sel = [jnp.sum(jnp.where(lane == k, ti, 0.0), axis=-1, keepdims=True) for k in range(TOP_K)]
    onehot = jnp.zeros(ti.shape, F32)
    for k in range(TOP_K):
        onehot = onehot + (lane_f == sel[k]).astype(F32)
    ri = lax.broadcasted_iota(jnp.int32, (tm, tm), 0)
    ci = lax.broadcasted_iota(jnp.int32, (tm, tm), 1)
    before = _dot((ri > ci).astype(BF16), onehot.astype(BF16)) + base[...]
    rank = jnp.zeros(ti.shape, jnp.int32)
    for k in range(TOP_K):
        rk = jnp.sum(jnp.where(lane_f == sel[k], before, 0.0), axis=-1, keepdims=True)
        rank = jnp.where(lane == k, rk.astype(jnp.int32), rank)
    rank_ref[...] = rank
    base[...] = base[...] + jnp.sum(onehot, axis=0, keepdims=True)
    cnt_ref[...] = base[...]


def _rank(ti):
    n = ti.shape[0]
    tm = ROW_TILE
    return pl.pallas_call(
        _rank_kernel,
        grid=(n // tm,),
        in_specs=[pl.BlockSpec((tm, LANES), lambda i: (i, 0))],
        out_specs=[pl.BlockSpec((tm, LANES), lambda i: (i, 0)), pl.BlockSpec((1, LANES), lambda i: (0, 0))],
        out_shape=[jax.ShapeDtypeStruct((n, LANES), jnp.int32), jax.ShapeDtypeStruct((1, LANES), F32)],
        scratch_shapes=[pltpu.VMEM((1, LANES), F32)],
        compiler_params=_cparams(("arbitrary",)),
        name="moe_rank",
    )(ti)


def _row_gather(table, idx):
    n = idx.shape[0]
    d = table.shape[1]
    workers = SC_CORES * SC_SUBCORES
    per_worker = n // workers
    assert n % workers == 0 and per_worker % SC_WINDOW == 0
    mesh = plsc.VectorSubcoreMesh(core_axis_name="core", subcore_axis_name="subcore",
                                  num_cores=SC_CORES, num_subcores=SC_SUBCORES)

    @pl.kernel(out_type=jax.ShapeDtypeStruct((n, d), table.dtype), mesh=mesh,
               scratch_types=[pltpu.VMEM((SC_WINDOW,), jnp.int32), pltpu.VMEM((SC_WINDOW, d), table.dtype),
                              pltpu.SemaphoreType.DMA],
               name="moe_row_gather")
    def gather(table_hbm, idx_hbm, out_hbm, idx_v, rows_v, sem):
        worker = lax.axis_index("subcore") * SC_CORES + lax.axis_index("core")

        @pl.loop(0, per_worker // SC_WINDOW)
        def _(c):
            base = pl.multiple_of(worker * per_worker + c * SC_WINDOW, SC_WINDOW)
            pltpu.sync_copy(idx_hbm.at[pl.ds(base, SC_WINDOW)], idx_v)
            pltpu.async_copy(table_hbm.at[idx_v], rows_v, sem).wait()
            pltpu.sync_copy(rows_v, out_hbm.at[pl.ds(base, SC_WINDOW)])

    return gather(table, idx)


def _expert_kernel(be_ref, bv_ref, xs_ref, wgu_ref, bgu_ref, wd_ref, bd_ref, y_ref, wgu_bf, wd_bf):
    i = pl.program_id(0)
    d_ff = wd_ref.shape[1]

    @pl.when((i == 0) | (be_ref[i] != be_ref[jnp.maximum(i - 1, 0)]))
    def _():
        wgu_bf[...] = wgu_ref[0].astype(BF16)
        wd_bf[...] = wd_ref[0].astype(BF16)

    @pl.when(bv_ref[i] > 0)
    def _():
        x = _unpack_halves(xs_ref[...]).astype(BF16)
        hgu = _dot(x, wgu_bf[...]) + bgu_ref[0]
        gate = jnp.minimum(hgu[:, :d_ff], SWIGLU_LIMIT)
        up = jnp.clip(hgu[:, d_ff:], -SWIGLU_LIMIT, SWIGLU_LIMIT)
        act = (up + 1.0) * gate * _sigmoid(SWIGLU_ALPHA * gate)
        y_ref[...] = _pack_halves(_dot(act.astype(BF16), wd_bf[...]) + bd_ref[0])

    @pl.when(bv_ref[i] == 0)
    def _():
        y_ref[...] = jnp.zeros(y_ref.shape, y_ref.dtype)


def _experts(block_expert, block_valid, xs, w_gu, b_gu3, w_down, b_down3):
    n_slots, dp = xs.shape
    n_blocks = n_slots // MOE_BLK
    d_ff, d = w_down.shape[1:]
    assert dp * 2 == d
    return pl.pallas_call(
        _expert_kernel,
        grid_spec=pltpu.PrefetchScalarGridSpec(
            num_scalar_prefetch=2,
            grid=(n_blocks,),
            in_specs=[pl.BlockSpec((MOE_BLK, dp), lambda i, be, bv: (i, 0)),
                      pl.BlockSpec((1, d, 2 * d_ff), lambda i, be, bv: (be[i], 0, 0)),
                      pl.BlockSpec((1, 1, 2 * d_ff), lambda i, be, bv: (be[i], 0, 0)),
                      pl.BlockSpec((1, d_ff, d), lambda i, be, bv: (be[i], 0, 0)),
                      pl.BlockSpec((1, 1, d), lambda i, be, bv: (be[i], 0, 0))],
            out_specs=pl.BlockSpec((MOE_BLK, dp), lambda i, be, bv: (i, 0)),
            scratch_shapes=[pltpu.VMEM((d, 2 * d_ff), BF16), pltpu.VMEM((d_ff, d), BF16)],
        ),
        out_shape=jax.ShapeDtypeStruct((n_slots, dp), jnp.uint32),
        compiler_params=_cparams(("arbitrary",)),
        name="moe_experts",
    )(block_expert, block_valid, xs, w_gu, b_gu3, w_down, b_down3)


def _combine_kernel(yk_ref, tg_ref, x1_ref, y_ref):
    dp = yk_ref.shape[1] // TOP_K
    tg = tg_ref[...]
    lane = lax.broadcasted_iota(jnp.int32, tg.shape, 1)
    acc = x1_ref[...]
    for k in range(TOP_K):
        gk = jnp.sum(jnp.where(lane == k, tg, 0.0), axis=-1, keepdims=True)
        acc = acc + gk * _unpack_halves(yk_ref[:, k * dp:(k + 1) * dp])
    y_ref[...] = acc


def _combine(yk, tg, x1):
    n, d = x1.shape
    tm = POST_TILE
    return pl.pallas_call(
        _combine_kernel,
        grid=(n // tm,),
        in_specs=[pl.BlockSpec((tm, yk.shape[1]), lambda i: (i, 0)),
                  pl.BlockSpec((tm, LANES), lambda i: (i, 0)),
                  pl.BlockSpec((tm, d), lambda i: (i, 0))],
        out_specs=pl.BlockSpec((tm, d), lambda i: (i, 0)),
        out_shape=jax.ShapeDtypeStruct((n, d), F32),
        compiler_params=_cparams(("parallel",)),
        name="moe_combine",
    )(yk, tg, x1)


def _moe(hm, ti, tg, x1, w_gu, b_gu, w_down, b_down):
    n, d = hm.shape
    rank, cnt = _rank(ti)
    counts = cnt[0, :N_EXPERTS].astype(jnp.int32)
    padded = (counts + MOE_BLK - 1) // MOE_BLK * MOE_BLK
    pad_end = jnp.cumsum(padded)
    pad_start = pad_end - padded
    top_i = ti[:, :TOP_K]
    dest = (pad_start[top_i] + rank[:, :TOP_K]).astype(jnp.int32).reshape(n * TOP_K)
    n_blocks = (n * TOP_K) // MOE_BLK + N_EXPERTS
    blk_start = jnp.arange(n_blocks, dtype=jnp.int32) * MOE_BLK
    block_expert = jnp.minimum(jnp.sum((pad_end[None, :] <= blk_start[:, None]).astype(jnp.int32), axis=1),
                               N_EXPERTS - 1)
    block_valid = (blk_start < pad_end[-1]).astype(jnp.int32)
    slot_tok = jnp.zeros((n_blocks * MOE_BLK,), jnp.int32).at[dest].set(
        jnp.arange(n * TOP_K, dtype=jnp.int32) // TOP_K)
    xs = _row_gather(hm, slot_tok)
    yb = _experts(block_expert, block_valid, xs, w_gu, b_gu.reshape(N_EXPERTS, 1, -1), w_down,
                  b_down.reshape(N_EXPERTS, 1, -1))
    yk = _row_gather(yb, dest).reshape(n, TOP_K * d)
    return _combine(yk, tg, x1)


def _pad_lanes(v, fill=0.0):
    v = v.reshape(1, -1).astype(F32)
    return jnp.pad(v, ((0, 0), (0, LANES - v.shape[1])), constant_values=fill)


def kernel(x_prompt, x_sample, cache_k, cache_v, state_ssm, state_conv, page_table, meta_tokens, rel_bias_table, norm1, w_in, conv_w, a_log, dt_bias, gdn_norm, q_norm, k_norm, lam_q1, lam_k1, lam_q2, lam_k2, sub_norm, w_br_a, w_br_b, w_out, norm2, w_router, b_router, w_gu, b_gu, w_down, b_down):
    bsz, seq, d = x_prompt.shape
    dbs, dseq, _ = x_sample.shape
    depth = w_in.shape[0]
    assert depth == 1 and dseq <= SUBLANES - (CONV_W - 1) and seq % ATT_BLK == 0
    lam_init = 0.8 - 0.6 * math.exp(-0.3 * 0)
    l = 0

    w = w_in[l]
    o_z = CONV_DIM + VA
    w_main = jnp.concatenate([w[:, :o_z], w[:, o_z + 2 * H_A:]], axis=1).astype(BF16)
    w_ba = jnp.pad(w[:, o_z:o_z + 2 * H_A], ((0, 0), (0, LANES - 2 * H_A))).astype(BF16)
    grp = np.arange(QB) // DK_B
    gmat = jnp.asarray((grp[:, None] == grp[None, :]).astype(np.float32) / DK_B, dtype=BF16)
    qn_t = jnp.tile(q_norm[l].astype(F32), QB // DK_B).reshape(1, QB)
    kn_t = jnp.tile(k_norm[l].astype(F32), QB // DK_B).reshape(1, QB)
    alog_p = jnp.pad(a_log[l].astype(F32), (H_A, LANES - 2 * H_A)).reshape(1, LANES)
    dtb_p = jnp.pad(dt_bias[l].astype(F32), (H_A, LANES - 2 * H_A)).reshape(1, LANES)
    n1 = norm1[l].reshape(1, d).astype(F32)
    proj = functools.partial(_inproj, n1=n1, w_main=w_main, w_ba=w_ba, gmat=gmat, qn_t=qn_t, kn_t=kn_t,
                             alog_p=alog_p, dtb_p=dtb_p)
    lam4 = jnp.stack([lam_q1[l], lam_k1[l], lam_q2[l], lam_k2[l]]).astype(F32)
    sn = sub_norm[l].reshape(1, DV_B).astype(F32)
    gn = gdn_norm[l].reshape(1, DV_A).astype(F32)
    cw = conv_w[l].astype(F32)

    ii = np.arange(ATT_BLK)[:, None]
    jj = np.arange(LANES)[None, :]
    bk_prompt = np.stack([_bucket_np(ii - jj), _bucket_np(ATT_BLK + ii - jj)])
    t8 = np.arange(SUBLANES)[:, None]
    bk_sample = np.stack([_bucket_np(PAGE + t8 - jj), _bucket_np(t8 - jj)])
    bt_prompt = _bias_tiles(rel_bias_table.astype(F32), bk_prompt)
    bt_sample = _bias_tiles(rel_bias_table.astype(F32), bk_sample)
    far_h = rel_bias_table[N_BUCKETS - 1].astype(F32)
    far_p = jnp.broadcast_to(far_h[:, None, None], (H_B, 1, ATT_KV))
    diag_t = jnp.where(jnp.asarray(ii >= jj), bt_prompt[:, 0], NEG)
    near_t = bt_prompt[:, 1]
    far_t = jnp.broadcast_to(far_h[:, None, None], near_t.shape)
    neg_t = jnp.full(near_t.shape, NEG, F32)
    tile2 = lambda a, b_, c_, d_: jnp.concatenate([jnp.concatenate([a, b_], axis=2),
                                                   jnp.concatenate([c_, d_], axis=2)], axis=1)
    bt_prompt = jnp.stack([tile2(diag_t, neg_t, near_t, diag_t), tile2(far_t, near_t, far_t, far_t)], axis=1)
    nrow = H_B * 2 * SUBLANES
    rows_of = lambda t: jnp.broadcast_to(t[:, None], (H_B, 2, SUBLANES, LANES)).reshape(nrow, LANES)
    last_t, self_t = rows_of(bt_sample[:, 0]), rows_of(bt_sample[:, 1])
    far_col = jnp.broadcast_to(far_h[:, None, None], (H_B, 2 * SUBLANES, LANES)).reshape(nrow, LANES)

    lreal = N_META + seq
    lp = -(-lreal // ATT_KV) * ATT_KV
    fp = lp - lreal
    assert fp % SUBLANES == 0 and fp >= GDN_CHUNK and (fp + N_META) % ATT_BLK == 0 and lp % GDN_CHUNK == 0
    xp = jnp.concatenate([jnp.zeros((bsz, fp, d), F32),
                          jnp.broadcast_to(meta_tokens.astype(F32)[None], (bsz, N_META, d)), x_prompt], axis=1)
    conv_p, z_p, bg_p, qd_p, kd_p, vd_p, sga_p, sgb_p, kdb_p, vdb_p = proj(xp.reshape(bsz * lp, d), ATT_KV)
    r3 = lambda a, b_, r_: a.reshape(b_, r_, a.shape[-1])
    oa_p, ssm_p = _gdn(r3(conv_p, bsz, lp), r3(bg_p, bsz, lp), r3(z_p, bsz, lp),
                       jnp.zeros((bsz, H_A, DK_A, DV_A), F32), cw, gn, fp, GDN_CHUNK, 1, H_A)
    kbias = jnp.where(jnp.arange(lp) < fp, NEG, 0.0).astype(F32).reshape(lp // ATT_KV, ATT_KV)
    ob_p = _attn_prompt(r3(qd_p, bsz, lp), r3(kdb_p, bsz, lp), r3(vdb_p, bsz, lp), bt_prompt, far_p, kbias,
                        lam4, sn, fp // ATT_Q, lam_init)

    ns = dbs * dseq
    conv_s, z_s, bg_s, qd_s, kd_s, vd_s, sga_s, sgb_s, _, _ = proj(x_sample.reshape(ns, d), min(ns, 256))
    cs = SUBLANES
    rs = 2 * cs
    fs = rs - dseq

    def chunk_rows(a, head=None):
        a = a.reshape(dbs, dseq, a.shape[-1]).astype(F32)
        parts = [jnp.zeros((dbs, fs - (0 if head is None else head.shape[1]), a.shape[-1]), F32)]
        if head is not None:
            parts.append(head.astype(F32))
        return jnp.concatenate(parts + [a], axis=1)

    oa_s, ssm_s = _gdn(chunk_rows(conv_s, state_conv[l]), chunk_rows(bg_s), chunk_rows(z_s),
                       state_ssm[l].astype(F32), cw, gn, fs, cs, math.gcd(dbs, 4), H_A)
    pad8 = lambda a: jnp.pad(a.reshape(dbs, dseq, a.shape[-1]), ((0, 0), (0, SUBLANES - dseq), (0, 0)))
    n_pool = cache_k.shape[1]
    ck_t = jnp.transpose(cache_k[l], (0, 2, 3, 4, 1)).reshape(n_pool, QB, PAGE)
    cv_i = cache_v[l].reshape(n_pool, PAGE * H_B, DV_B)
    ob_s = _attn_sample(page_table, pad8(qd_s).astype(F32), pad8(kd_s), pad8(vd_s), ck_t, cv_i,
                        last_t, self_t, far_col, lam4, sn, dseq, lam_init)

    wa, wb, wo = w_br_a[l].astype(BF16), w_br_b[l].astype(BF16), w_out[l].astype(BF16)
    n2 = norm2[l].reshape(1, d).astype(F32)
    wr = jnp.pad(w_router[l], ((0, 0), (0, LANES - N_EXPERTS))).astype(BF16)
    br = _pad_lanes(b_router[l])
    tiles_b = seq // POST_TILE
    tiles_lp = lp // POST_TILE
    skip = (fp + N_META) // POST_TILE
    npt = bsz * seq
    assert ns % POST_TILE == 0 and seq % POST_TILE == 0 and (fp + N_META) % POST_TILE == 0
    oa_s2 = oa_s[:, fs:].reshape(ns, VA)
    ob_s2 = ob_s[:, :dseq].reshape(ns, VB)
    x1, hm, ti, tg = _post(
        (x_prompt.reshape(npt, d), oa_p.reshape(bsz * lp, VA), ob_p.reshape(bsz * lp, VB), sga_p, sgb_p),
        (x_sample.reshape(ns, d), oa_s2, ob_s2, sga_s, sgb_s), (wa, wb, wo, n2, wr, br), npt, ns,
        lambda i: (i // tiles_b) * tiles_lp + skip + i % tiles_b)

    y = _moe(hm, ti, tg, x1, w_gu[l], b_gu[l], w_down[l], b_down[l])
    y_prompt = y[:npt].reshape(bsz, seq, d)
    y_sample = y[npt:].reshape(dbs, dseq, d)

    k_prompt = r3(kd_p, bsz, lp)[:, fp:].reshape(1, bsz, lreal, H_B, 2, DK_B)
    v_prompt = r3(vd_p, bsz, lp)[:, fp:].reshape(1, bsz, lreal, H_B, DV_B)
    conv_prompt = r3(conv_p, bsz, lp)[:, lp - (CONV_W - 1):][None]
    xpad_s = jnp.concatenate([state_conv[l].astype(F32), conv_s.reshape(dbs, dseq, CONV_DIM)], axis=1)
    conv_sample = xpad_s[:, dseq:][None]
    return (y_prompt, y_sample, k_prompt, v_prompt, ssm_p[None], conv_prompt,
            kd_s.reshape(1, dbs, dseq, H_B, 2, DK_B), vd_s.reshape(1, dbs, dseq, H_B, DV_B), ssm_s[None], conv_sample)
```

```python
import functools
import math

import numpy as np
import jax
import jax.numpy as jnp
from jax import lax
from jax.experimental import pallas as pl
from jax.experimental.pallas import tpu as pltpu
from jax.experimental.pallas import tpu_sc as plsc

F32 = jnp.float32
BF16 = jnp.bfloat16
HI = lax.Precision.HIGHEST

N_META = 16
H_A, DK_A, DV_A = 4, 128, 128
CONV_W = 4
H_B, DK_B = 4, 64
DV_B = 2 * DK_B
N_BUCKETS, MAX_DIST = 32, 128
N_EXPERTS, TOP_K = 32, 4
SWIGLU_LIMIT, SWIGLU_ALPHA = 7.0, 1.702
EPS = 1e-6
PAGE = 128
QA = H_A * DK_A
VA = H_A * DV_A
QB = H_B * 2 * DK_B
VB = H_B * DV_B
CONV_DIM = 2 * QA + VA

LANES = 128
SUBLANES = 8
VMEM_LIMIT = 56 * 1024 * 1024
SC_CORES, SC_SUBCORES, SC_WINDOW = 2, 16, 64

GDN_CHUNK = 64
ATT_BLK = 128
ATT_KV = 2 * ATT_BLK
ATT_Q = 2 * ATT_BLK
PAGES_PER_STEP = 16
MOE_BLK = 256
ROW_TILE = 128
POST_TILE = 256
NEG = -1e30


def _dot(a, b, prec=None):
    return jnp.dot(a, b, preferred_element_type=F32, precision=prec)


def _dot_nt(a, b, prec=None):
    return lax.dot_general(a, b, (((1,), (1,)), ((), ())), preferred_element_type=F32, precision=prec)


def _dot_tn(a, b, prec=None):
    return lax.dot_general(a, b, (((0,), (0,)), ((), ())), preferred_element_type=F32, precision=prec)


def _split(a):
    hi = a.astype(BF16)
    return hi, (a - hi.astype(F32)).astype(BF16)


def _lhs3(a):
    hi, lo = _split(a)
    return jnp.concatenate([hi, lo, hi], axis=1)


def _rhs3(b):
    hi, lo = _split(b)
    return jnp.concatenate([hi, hi, lo], axis=0)


def _pack_halves(x):
    w = x.shape[1] // 2
    bits = lambda v: pltpu.bitcast(v.astype(BF16).astype(F32), jnp.uint32)
    return (bits(x[:, :w]) >> 16) | (bits(x[:, w:]) & jnp.uint32(0xFFFF0000))


def _unpack_halves(u):
    lo = pltpu.bitcast(u << 16, F32)
    hi = pltpu.bitcast(u & jnp.uint32(0xFFFF0000), F32)
    return jnp.concatenate([lo, hi], axis=1)


def _sigmoid(x):
    return 1.0 / (1.0 + jnp.exp(-x))


def _cparams(sem, flags=None):
    return pltpu.CompilerParams(dimension_semantics=sem, vmem_limit_bytes=VMEM_LIMIT, flags=flags)


def _bucket_np(d):
    d = np.maximum(d, 0)
    df = np.maximum(d, 1).astype(np.float32)
    max_exact = N_BUCKETS // 2
    large = max_exact + (np.log(df / np.float32(max_exact)) / np.float32(math.log(MAX_DIST / max_exact))
                         * np.float32(N_BUCKETS - max_exact)).astype(np.int32)
    return np.where(d < max_exact, d, np.minimum(large, N_BUCKETS - 1)).astype(np.int32)


def _bias_kernel(tab_ref, bk_ref, o_ref):
    for t in range(bk_ref.shape[0]):
        bk = bk_ref[t]
        for h in range(H_B):
            acc = jnp.zeros(bk.shape, F32)
            for b in range(N_BUCKETS):
                acc = jnp.where(bk == b, tab_ref[b, h], acc)
            o_ref[h, t] = acc


def _bias_tiles(table, buckets):
    t, r, _ = buckets.shape
    return pl.pallas_call(
        _bias_kernel,
        out_shape=jax.ShapeDtypeStruct((H_B, t, r, LANES), F32),
        in_specs=[pl.BlockSpec(memory_space=pltpu.SMEM), pl.BlockSpec(memory_space=pltpu.VMEM)],
        out_specs=pl.BlockSpec(memory_space=pltpu.VMEM),
        name="rel_bias_tiles",
    )(table, jnp.asarray(buckets))


def _inproj_kernel(x_ref, n1_ref, w_ref, wba_ref, gm_ref, qn_ref, kn_ref, alog_ref, dtb_ref,
                   conv_ref, z_ref, bg_ref, qd_ref, kd_ref, vd_ref, sga_ref, sgb_ref, kdb_ref, vdb_ref):
    x = x_ref[...]
    ms = jnp.mean(x * x, axis=-1, keepdims=True)
    h = (x * lax.rsqrt(ms + EPS) * n1_ref[...]).astype(BF16)
    o = 0
    conv_ref[...] = _dot(h, w_ref[:, o:o + CONV_DIM]); o += CONV_DIM
    z_ref[...] = _dot(h, w_ref[:, o:o + VA]).astype(z_ref.dtype); o += VA

    def group_norm(y, g):
        sq = y * y
        hi = sq.astype(BF16)
        lo = (sq - hi.astype(F32)).astype(BF16)
        msq = _dot(hi, gm_ref[...]) + _dot(lo, gm_ref[...])
        return y * lax.rsqrt(msq + EPS) * g

    qb = _dot(h, w_ref[:, o:o + QB]); o += QB
    qd_ref[...] = (group_norm(qb, qn_ref[...]) * (DK_B ** -0.5)).astype(qd_ref.dtype)
    kb = _dot(h, w_ref[:, o:o + QB]); o += QB
    kd = group_norm(kb, kn_ref[...])
    kd_ref[...] = kd
    kdb_ref[...] = kd.astype(kdb_ref.dtype)
    vd = _dot(h, w_ref[:, o:o + VB]); o += VB
    vd_ref[...] = vd
    vdb_ref[...] = vd.astype(vdb_ref.dtype)
    d_model = x.shape[1]
    sga_ref[...] = _sigmoid(_dot(h, w_ref[:, o:o + d_model])).astype(sga_ref.dtype); o += d_model
    sgb_ref[...] = _sigmoid(_dot(h, w_ref[:, o:o + d_model])).astype(sgb_ref.dtype)
    t = _dot(h, wba_ref[...])
    lane = lax.broadcasted_iota(jnp.int32, t.shape, 1)
    ta = t + dtb_ref[...]
    sp = jnp.maximum(ta, 0.0) + jnp.log(1.0 + jnp.exp(-jnp.abs(ta)))
    bg_ref[...] = jnp.where(lane < H_A, _sigmoid(t), -jnp.exp(alog_ref[...]) * sp)


def _inproj(x2d, tm, n1, w_main, w_ba, gmat, qn_t, kn_t, alog_p, dtb_p):
    n, d = x2d.shape
    assert n % tm == 0
    row = lambda c: pl.BlockSpec((tm, c), lambda i: (i, 0))
    full = lambda a: pl.BlockSpec(a.shape, lambda i: (0,) * a.ndim)
    outs = [(CONV_DIM, F32), (VA, BF16), (LANES, F32), (QB, BF16), (QB, F32), (VB, F32), (d, BF16), (d, BF16),
            (QB, BF16), (VB, BF16)]
    return pl.pallas_call(
        _inproj_kernel,
        grid=(n // tm,),
        in_specs=[row(d)] + [full(a) for a in (n1, w_main, w_ba, gmat, qn_t, kn_t, alog_p, dtb_p)],
        out_specs=[row(c) for c, _ in outs],
        out_shape=[jax.ShapeDtypeStruct((n, c), dt) for c, dt in outs],
        compiler_params=_cparams(("parallel",)),
        name="in_proj",
    )(x2d, n1, w_main, w_ba, gmat, qn_t, kn_t, alog_p, dtb_p)


def _gdn_kernel(fv, c, q_ref, k_ref, v_ref, cwq_ref, cwk_ref, cwv_ref, bg_ref, z_ref, s0_ref, gn_ref,
                o_ref, st_ref):
    nb, r, wd = q_ref.shape
    hg = wd // LANES
    head0 = pl.program_id(1) * hg
    chains = [(b, hh) for b in range(nb) for hh in range(hg)]

    ri = lax.broadcasted_iota(jnp.int32, (c, c), 0)
    ci = lax.broadcasted_iota(jnp.int32, (c, c), 1)
    incl = ri >= ci
    strict = ri > ci
    eye = (ri == ci).astype(F32)
    lane = lax.broadcasted_iota(jnp.int32, (c, LANES), 1)
    sub_t = lax.broadcasted_iota(jnp.int32, (LANES, c), 0)
    rowid = lax.broadcasted_iota(jnp.int32, (c, 1), 0)
    colid = lax.broadcasted_iota(jnp.int32, (1, c), 1)

    def conv(x_ref, cw_ref, b, cols, r0):
        w = x_ref[b, pl.ds(r0 - SUBLANES, c + SUBLANES), cols]
        acc = w[SUBLANES - 3:SUBLANES - 3 + c] * cw_ref[0:1, cols]
        for j in range(1, CONV_W):
            acc = acc + w[SUBLANES - 3 + j:SUBLANES - 3 + j + c] * cw_ref[j:j + 1, cols]
        return acc * _sigmoid(acc)

    def elementwise(b, hh, r0):
        cols = slice(hh * LANES, (hh + 1) * LANES)
        head = head0 + hh
        vcol = ((r0 + rowid) >= fv).astype(F32)
        vrow = ((r0 + colid) >= fv).astype(F32)
        qv = conv(q_ref, cwq_ref, b, cols, r0)
        kv = conv(k_ref, cwk_ref, b, cols, r0)
        v = conv(v_ref, cwv_ref, b, cols, r0) * vcol
        q = qv * lax.rsqrt(jnp.sum(qv * qv, axis=-1, keepdims=True) + EPS) * (DK_A ** -0.5) * vcol
        k = kv * lax.rsqrt(jnp.sum(kv * kv, axis=-1, keepdims=True) + EPS) * vcol
        bgc = bg_ref[b, pl.ds(r0, c), :]
        beta = jnp.sum(jnp.where(lane == head, bgc, 0.0), axis=-1, keepdims=True) * vcol
        g_col = jnp.sum(jnp.where(lane == H_A + head, bgc, 0.0), axis=-1, keepdims=True) * vcol
        g_row = jnp.sum(jnp.where(sub_t == H_A + head, bgc.T, 0.0), axis=0, keepdims=True) * vrow
        gc = jnp.sum(jnp.where(incl, g_row, 0.0), axis=-1, keepdims=True)
        gr = jnp.sum(jnp.where(ri <= ci, g_col, 0.0), axis=0, keepdims=True)
        decay = jnp.where(incl, jnp.exp(jnp.where(incl, gc - gr, 0.0)), 0.0)
        kb = k * beta
        eg = jnp.exp(gc)
        g_last = jnp.sum(jnp.where(rowid == c - 1, gc, 0.0), axis=0, keepdims=True)
        k_dt = (k * jnp.exp(g_last - gc)).T
        lhs = jnp.concatenate([kb, q], axis=0).astype(BF16)
        rhs = jnp.concatenate([v * beta, kb * eg], axis=-1)
        return lhs, k.astype(BF16), decay, rhs, q * eg, k_dt, jnp.exp(g_last)

    def prepare_a(j):
        r0 = pl.multiple_of(j * c, c)
        vec = [elementwise(b, hh, r0) for b, hh in chains]
        kk = [_dot_nt(v_[0], v_[1]) for v_ in vec]
        return vec, kk

    def prepare_b(vec, kk):
        low = [jnp.where(strict, kk_[:c] * v_[2], 0.0) for v_, kk_ in zip(vec, kk)]
        intra = [kk_[c:] * v_[2] for v_, kk_ in zip(vec, kk)]
        inv = [eye - lw for lw in low]
        levels = int(math.log2(c)) - 1
        pw = [_dot(_lhs3(lw), _rhs3(lw)) for lw in low]
        for lev in range(levels):
            pw_r = [_rhs3(p) for p in pw]
            if lev + 1 < levels:
                pw = [_dot(_lhs3(p), r_) for p, r_ in zip(pw, pw_r)]
            inv = [iv + _dot(_lhs3(iv), r_) for iv, r_ in zip(inv, pw_r)]
        sol = [_dot(_lhs3(iv), _rhs3(v_[3])) for iv, v_ in zip(inv, vec)]
        out = []
        for v_, sl, it in zip(vec, sol, intra):
            on_state = jnp.concatenate([sl[:, DV_A:], v_[4]], axis=0).astype(BF16)
            on_u = jnp.concatenate([it, v_[5]], axis=0).astype(BF16)
            out.append((sl[:, :DV_A], on_state, on_u, v_[6]))
        return tuple(out)

    def apply_a(prepared, states):
        return [_dot(p[1], s.astype(BF16)) for p, s in zip(prepared, states)]

    def apply_b(j, prepared, states, ps):
        r0 = pl.multiple_of(j * c, c)
        u = [p[0] - ps_[:c] for p, ps_ in zip(prepared, ps)]
        pu = [_dot(p[2], u_.astype(BF16)) for p, u_ in zip(prepared, u)]
        new_states = []
        for (b, hh), p, s, ps_, pu_ in zip(chains, prepared, states, ps, pu):
            cols = slice(hh * LANES, (hh + 1) * LANES)
            o = ps_[c:] + pu_[:c]
            on = o * lax.rsqrt(jnp.mean(o * o, axis=-1, keepdims=True) + EPS) * gn_ref[...]
            zc = z_ref[b, pl.ds(r0, c), cols].astype(F32)
            o_ref[b, pl.ds(r0, c), cols] = (on * (zc * _sigmoid(zc))).astype(o_ref.dtype)
            new_states.append(s * p[3] + pu_[c:])
        return tuple(new_states)

    def chunk(j, carry):
        prepared, states = carry
        ps = apply_a(prepared, states)
        vec, kk = prepare_a(j + 1)
        states = apply_b(j, prepared, states, ps)
        return prepare_b(vec, kk), states

    j0 = fv // c
    last = r // c - 1
    o_ref[:, 0:j0 * c, :] = jnp.zeros((nb, j0 * c, wd), o_ref.dtype)
    init = (prepare_b(*prepare_a(j0)), tuple(s0_ref[b, hh] for b, hh in chains))
    prepared, states = lax.fori_loop(j0, last, chunk, init)
    for (b, hh), s in zip(chains, apply_b(last, prepared, states, apply_a(prepared, states))):
        st_ref[b, hh] = s


def _gdn(conv3, bg3, z3, s0, conv_w, gn, fv, c, nb, hg):
    bsz, r, _ = conv3.shape
    assert r % c == 0 and fv // c >= 1 and c >= SUBLANES and bsz % nb == 0 and H_A % hg == 0
    ng = H_A // hg
    wd = hg * LANES
    blk = lambda off: pl.BlockSpec((nb, r, wd), lambda i, g: (i, 0, off * ng + g))
    cw = lambda off: pl.BlockSpec((CONV_W, wd), lambda i, g: (0, off * ng + g))
    st = pl.BlockSpec((nb, hg, DK_A, DV_A), lambda i, g: (i, g, 0, 0))
    return pl.pallas_call(
        functools.partial(_gdn_kernel, fv, c),
        grid=(bsz // nb, ng),
        in_specs=[blk(0), blk(1), blk(2), cw(0), cw(1), cw(2),
                  pl.BlockSpec((nb, r, LANES), lambda i, g: (i, 0, 0)), blk(0), st,
                  pl.BlockSpec((1, LANES), lambda i, g: (0, 0))],
        out_specs=[blk(0), st],
        out_shape=[jax.ShapeDtypeStruct((bsz, r, VA), F32), jax.ShapeDtypeStruct((bsz, H_A, DK_A, DV_A), F32)],
        compiler_params=_cparams(("parallel", "parallel")),
        name="gdn",
    )(conv3, conv3, conv3, conv_w, conv_w, conv_w, bg3, z3, s0, gn)


def _lambda(lam_ref, lam_init):
    l1 = jnp.sum(lam_ref[0:1, :] * lam_ref[1:2, :], axis=-1, keepdims=True)
    l2 = jnp.sum(lam_ref[2:3, :] * lam_ref[3:4, :], axis=-1, keepdims=True)
    return jnp.exp(l1) - jnp.exp(l2) + lam_init


def _sub_norm(o0, o1, lam, sn, lam_init):
    ob = o0 - lam * o1
    return ob * lax.rsqrt(jnp.mean(ob * ob, axis=-1, keepdims=True) + EPS) * sn * (1.0 - lam_init)


def _attn_prompt_kernel(first_q, lam_init, q_ref, k_ref, v_ref, bt_ref, far_ref, kb_ref, lam_ref, sn_ref, o_ref):
    qi = pl.program_id(2)

    @pl.when(qi < first_q)
    def _():
        o_ref[0] = jnp.zeros(o_ref.shape[1:], o_ref.dtype)

    @pl.when(qi >= first_q)
    def _():
        q = q_ref[0]
        lane = lax.broadcasted_iota(jnp.int32, q.shape, 1)
        zero = jnp.zeros_like(q)
        qs = jnp.concatenate([jnp.where(lane < DK_B, q, zero), jnp.where(lane >= DK_B, q, zero)], axis=0)

        def scores(j):
            k0 = pl.multiple_of(j * ATT_KV, ATT_KV)
            return _dot_nt(qs, k_ref[0, pl.ds(k0, ATT_KV), :])

        def step(j, carry, bias, last=False):
            m, l, acc, s_raw = carry
            s_next = s_raw if last else scores(j + 1)
            k0 = pl.multiple_of(j * ATT_KV, ATT_KV)
            s = s_raw + bias
            m_new = jnp.maximum(m, jnp.max(s, axis=-1, keepdims=True))
            alpha = jnp.exp(m - m_new)
            p = jnp.exp(s - m_new)
            l = alpha * l + jnp.sum(p, axis=-1, keepdims=True)
            acc = alpha * acc + _dot(p.astype(BF16), v_ref[0, pl.ds(k0, ATT_KV), :])
            return m_new, l, acc, s_next

        def tile_bias(j, which):
            t = bt_ref[0, which] + kb_ref[pl.ds(j, 1), :]
            return jnp.concatenate([t, t], axis=0)

        carry = (jnp.full((2 * ATT_Q, 1), NEG, F32), jnp.zeros((2 * ATT_Q, 1), F32),
                 jnp.zeros((2 * ATT_Q, DV_B), F32), scores(0))
        carry = lax.fori_loop(0, qi - 1, lambda j, c: step(j, c, far_ref[0] + kb_ref[pl.ds(j, 1), :]), carry)
        carry = lax.cond(qi >= 1, lambda c: step(qi - 1, c, tile_bias(qi - 1, 1)), lambda c: c, carry)
        _, l, acc, _ = step(qi, carry, tile_bias(qi, 0), last=True)
        o = acc / l
        o_ref[0] = _sub_norm(o[:ATT_Q], o[ATT_Q:], _lambda(lam_ref, lam_init), sn_ref[...],
                             lam_init).astype(o_ref.dtype)


def _attn_prompt(qd3, kd3, vd3, btiles, far, kbias, lam4, sn, first_q, lam_init):
    bsz, lp, _ = qd3.shape
    assert lp % ATT_KV == 0 and ATT_Q == ATT_KV
    kv = pl.BlockSpec((1, lp, LANES), lambda b, h, i: (b, 0, h))
    qo = pl.BlockSpec((1, ATT_Q, LANES), lambda b, h, i: (b, i, h))
    full = lambda a: pl.BlockSpec(a.shape, lambda b, h, i: (0,) * a.ndim)
    return pl.pallas_call(
        functools.partial(_attn_prompt_kernel, first_q, lam_init),
        grid=(bsz, H_B, lp // ATT_Q),
        in_specs=[qo, kv, kv,
                  pl.BlockSpec((1, 2, ATT_Q, ATT_KV), lambda b, h, i: (h, 0, 0, 0)),
                  pl.BlockSpec((1, 1, ATT_KV), lambda b, h, i: (h, 0, 0)),
                  full(kbias), full(lam4), full(sn)],
        out_specs=qo,
        out_shape=jax.ShapeDtypeStruct((bsz, lp, VB), F32),
        compiler_params=_cparams(("parallel", "parallel", "parallel")),
        name="diff_attn_prompt",
    )(qd3, kd3, vd3, btiles, far, kbias, lam4, sn)


def _attn_sample_kernel(npp, t_new, lam_init, pt_ref, q_ref, kn_ref, vn_ref, ck_ref, cv_ref, last_ref, self_ref,
                        far_ref, lam_ref, sn_ref, o_ref, qm, kself, vself, m_s, l_s, acc_s, kbuf, vbuf, sem):
    b = pl.program_id(0)
    st = pl.program_id(1)
    nb = pl.num_programs(0)
    nst = pl.num_programs(1)
    nrow = H_B * 2 * SUBLANES
    hrows = 2 * SUBLANES
    t = b * nst + st
    slot = t % 2

    def page_copies(bb, ss, sl):
        out = []
        for g in range(npp):
            page = pt_ref[bb, ss * npp + g]
            out.append(pltpu.make_async_copy(ck_ref.at[page], kbuf.at[sl, g], sem.at[sl]))
            out.append(pltpu.make_async_copy(cv_ref.at[page], vbuf.at[sl, g], sem.at[sl]))
        return out

    @pl.when(t == 0)
    def _():
        for cp in page_copies(0, 0, 0):
            cp.start()

    @pl.when(t + 1 < nb * nst)
    def _():
        wrap = st + 1 == nst
        for cp in page_copies(jnp.where(wrap, b + 1, b), jnp.where(wrap, 0, st + 1), 1 - slot):
            cp.start()

    for cp in page_copies(b, st, slot):
        cp.wait()

    def update(s, values_of_head):
        m_old = m_s[...]
        m_new = jnp.maximum(m_old, jnp.max(s, axis=-1, keepdims=True))
        alpha = jnp.exp(m_old - m_new)
        pr = jnp.exp(s - m_new)
        l_s[...] = alpha * l_s[...] + jnp.sum(pr, axis=-1, keepdims=True)
        prb = pr.astype(BF16)
        pv = [_dot(prb[h * hrows:(h + 1) * hrows], values_of_head(h)) for h in range(H_B)]
        acc_s[...] = alpha * acc_s[...] + jnp.concatenate(pv, axis=0)
        m_s[...] = m_new

    @pl.when(st == 0)
    def _():
        q8 = q_ref[0]
        lane = lax.broadcasted_iota(jnp.int32, q8.shape, 1)
        qm[...] = jnp.concatenate([jnp.where(lane // DK_B == g, q8, 0.0) for g in range(2 * H_B)],
                                  axis=0).astype(qm.dtype)
        kself[...] = jnp.zeros(kself.shape, F32)
        vself[...] = jnp.zeros(vself.shape, F32)
        kself[0:SUBLANES, :] = kn_ref[0]
        vself[0:SUBLANES, :] = vn_ref[0]
        m_s[...] = jnp.full(m_s.shape, NEG, F32)
        l_s[...] = jnp.zeros(l_s.shape, F32)
        acc_s[...] = jnp.zeros(acc_s.shape, F32)
        rows = lax.broadcasted_iota(jnp.int32, (nrow, PAGE), 0) % SUBLANES
        cols = lax.broadcasted_iota(jnp.int32, (nrow, PAGE), 1)
        ok = (cols <= rows) & (cols < t_new)
        s = jnp.where(ok, _dot_nt(qm[...], kself[...].astype(BF16)) + self_ref[...], NEG)
        update(s, lambda h: vself[:, h * DV_B:(h + 1) * DV_B].astype(BF16))

    far = far_ref[:, 0:1]
    parts = []
    for g in range(npp):
        bias = far if g < npp - 1 else jnp.where(st == nst - 1, last_ref[...], far)
        parts.append(_dot(qm[...], kbuf[slot, g].astype(BF16)) + bias)

    def page_values(h):
        return jnp.concatenate([vbuf[slot, g, pl.ds(h, PAGE, stride=H_B), :] for g in range(npp)],
                               axis=0).astype(BF16)

    update(jnp.concatenate(parts, axis=1), page_values)

    @pl.when(st == nst - 1)
    def _():
        o = acc_s[...] / l_s[...]
        lam = _lambda(lam_ref, lam_init)
        for h in range(H_B):
            r0 = h * hrows
            o_ref[0, :, h * DV_B:(h + 1) * DV_B] = _sub_norm(o[r0:r0 + SUBLANES], o[r0 + SUBLANES:r0 + hrows], lam,
                                                             sn_ref[...], lam_init)


def _attn_sample(page_table, q8, kn8, vn8, ck_t, cv_i, last_t, self_t, far_col, lam4, sn, t_new, lam_init):
    bsz, npg = page_table.shape
    npp = math.gcd(PAGES_PER_STEP, npg)
    nrow = H_B * 2 * SUBLANES
    tok = pl.BlockSpec((1, SUBLANES, QB), lambda b, s, pt: (b, 0, 0))
    hbm = pl.BlockSpec(memory_space=pl.ANY)
    full = lambda a: pl.BlockSpec(a.shape, lambda b, s, pt: (0,) * a.ndim)
    return pl.pallas_call(
        functools.partial(_attn_sample_kernel, npp, t_new, lam_init),
        grid_spec=pltpu.PrefetchScalarGridSpec(
            num_scalar_prefetch=1,
            grid=(bsz, npg // npp),
            in_specs=[tok, tok, tok, hbm, hbm, full(last_t), full(self_t), full(far_col), full(lam4), full(sn)],
            out_specs=tok,
            scratch_shapes=[pltpu.VMEM((nrow, QB), BF16), pltpu.VMEM((PAGE, QB), F32), pltpu.VMEM((PAGE, VB), F32),
                            pltpu.VMEM((nrow, 1), F32), pltpu.VMEM((nrow, 1), F32), pltpu.VMEM((nrow, DV_B), F32),
                            pltpu.VMEM((2, npp, QB, PAGE), F32), pltpu.VMEM((2, npp, PAGE * H_B, DV_B), F32),
                            pltpu.SemaphoreType.DMA((2,))],
        ),
        out_shape=jax.ShapeDtypeStruct((bsz, SUBLANES, VB), F32),
        compiler_params=_cparams(("arbitrary", "arbitrary")),
        name="diff_attn_sample",
    )(page_table, q8, kn8, vn8, ck_t, cv_i, last_t, self_t, far_col, lam4, sn)


def _post_kernel(prompt_tiles, *refs):
    wa_ref, wb_ref, wo_ref, n2_ref, wr_ref, br_ref, x1_ref, hm_ref, ti_ref, tg_ref = refs[10:]
    is_sample = pl.program_id(0) >= prompt_tiles
    x, oa, ob, sga, sgb = (jnp.where(is_sample, s_ref[...], p_ref[...]) for p_ref, s_ref in zip(refs[:5], refs[5:10]))
    ya = _dot(oa.astype(BF16), wa_ref[...])
    yb = _dot(ob.astype(BF16), wb_ref[...])
    merged = sga.astype(F32) * ya + sgb.astype(F32) * yb
    x1 = x + _dot(merged.astype(BF16), wo_ref[...])
    x1_ref[...] = x1
    hm = x1 * lax.rsqrt(jnp.mean(x1 * x1, axis=-1, keepdims=True) + EPS) * n2_ref[...]
    hm_ref[...] = _pack_halves(hm)
    logits = _dot(hm.astype(BF16), wr_ref[...]) + br_ref[...]
    lane = lax.broadcasted_iota(jnp.int32, logits.shape, 1)
    lane_f = lane.astype(F32)
    work = jnp.where(lane < N_EXPERTS, logits, -jnp.inf)
    ti = jnp.zeros(logits.shape, F32)
    tg = jnp.zeros(logits.shape, F32)
    v0 = None
    den = None
    for k in range(TOP_K):
        vk = jnp.max(work, axis=-1, keepdims=True)
        ik = jnp.min(jnp.where(work == vk, lane_f, float(LANES)), axis=-1, keepdims=True)
        work = jnp.where(lane_f == ik, -jnp.inf, work)
        if k == 0:
            v0 = vk
        ek = jnp.exp(vk - v0)
        den = ek if k == 0 else den + ek
        ti = jnp.where(lane == k, ik, ti)
        tg = jnp.where(lane == k, ek, tg)
    ti_ref[...] = ti.astype(jnp.int32)
    tg_ref[...] = tg / den


def _post(prompt_in, sample_in, dense, n_prompt, n_sample, prompt_map):
    d = prompt_in[0].shape[1]
    tm = POST_TILE
    tp = n_prompt // tm
    n_total = n_prompt + n_sample
    widths = (d, VA, VB, d, d)
    full = lambda a: pl.BlockSpec(a.shape, lambda i: (0,) * a.ndim)
    p_spec = lambda k, c: pl.BlockSpec((tm, c), lambda i: ((jnp.minimum(i, tp - 1) if k == 0
                                                             else prompt_map(jnp.minimum(i, tp - 1))), 0))
    s_spec = lambda c: pl.BlockSpec((tm, c), lambda i: (jnp.maximum(i - tp, 0), 0))
    rout = lambda c: pl.BlockSpec((tm, c), lambda i: (i, 0))
    return pl.pallas_call(
        functools.partial(_post_kernel, tp),
        grid=(n_total // tm,),
        in_specs=[p_spec(k, c) for k, c in enumerate(widths)] + [s_spec(c) for c in widths]
                 + [full(a) for a in dense],
        out_specs=[rout(d), rout(d // 2), rout(LANES), rout(LANES)],
        out_shape=[jax.ShapeDtypeStruct((n_total, d), F32), jax.ShapeDtypeStruct((n_total, d // 2), jnp.uint32),
                   jax.ShapeDtypeStruct((n_total, LANES), jnp.int32), jax.ShapeDtypeStruct((n_total, LANES), F32)],
        compiler_params=_cparams(("parallel",)),
        name="merge_outproj_router",
    )(*prompt_in, *sample_in, *dense)


def _rank_kernel(ti_ref, rank_ref, cnt_ref, base):
    i = pl.program_id(0)
    tm = ti_ref.shape[0]

    @pl.when(i == 0)
    def _():
        base[...] = jnp.zeros(base.shape, F32)

    ti = ti_ref[...].astype(F32)
    lane = lax.broadcasted_iota(jnp.int32, ti.shape, 1)
    lane_f = lane.astype(F32)
    sel = [jnp.sum(jnp.where(lane == k, ti, 0.0), axis=-1, keepdims=True) for k in range(TOP_K)]
    onehot = jnp.zeros(ti.shape, F32)
    for k in range(TOP_K):
        onehot = onehot + (lane_f == sel[k]).astype(F32)
    ri = lax.broadcasted_iota(jnp.int32, (tm, tm), 0)
    ci = lax.broadcasted_iota(jnp.int32, (tm, tm), 1)
    before = _dot((ri > ci).astype(BF16), onehot.astype(BF16)) + base[...]
    rank = jnp.zeros(ti.shape, jnp.int32)
    for k in range(TOP_K):
        rk = jnp.sum(jnp.where(lane_f == sel[k], before, 0.0), axis=-1, keepdims=True)
        rank = jnp.where(lane == k, rk.astype(jnp.int32), rank)
    rank_ref[...] = rank
    base[...] = base[...] + jnp.sum(onehot, axis=0, keepdims=True)
    cnt_ref[...] = base[...]


def _rank(ti):
    n = ti.shape[0]
    tm = ROW_TILE
    return pl.pallas_call(
        _rank_kernel,
        grid=(n // tm,),
        in_specs=[pl.BlockSpec((tm, LANES), lambda i: (i, 0))],
        out_specs=[pl.BlockSpec((tm, LANES), lambda i: (i, 0)), pl.BlockSpec((1, LANES), lambda i: (0, 0))],
        out_shape=[jax.ShapeDtypeStruct((n, LANES), jnp.int32), jax.ShapeDtypeStruct((1, LANES), F32)],
        scratch_shapes=[pltpu.VMEM((1, LANES), F32)],
        compiler_params=_cparams(("arbitrary",)),
        name="moe_rank",
    )(ti)


def _dispatch_kernel(dest_ref, hm_ref, xs_in, xs_out, sem):
    del xs_in
    tm = hm_ref.shape[0]

    def row_copy(r, d):
        return pltpu.make_async_copy(hm_ref.at[pl.ds(r, 1)], xs_out.at[pl.ds(d, 1)], sem)

    def issue(r, c):
        for k in range(TOP_K):
            row_copy(r, dest_ref[0, 0, r * TOP_K + k]).start(priority=k % 2)
        return c

    def drain(r, c):
        for k in range(TOP_K):
            row_copy(r, dest_ref[0, 0, r * TOP_K + k]).wait()
        return c

    lax.fori_loop(0, tm, issue, 0)
    lax.fori_loop(0, tm, drain, 0)


def _dispatch(dest3, hm, xs_zero):
    n, d = hm.shape
    tm = ROW_TILE
    return pl.pallas_call(
        _dispatch_kernel,
        grid=(n // tm,),
        in_specs=[pl.BlockSpec((1, 1, tm * TOP_K), lambda i: (i, 0, 0), memory_space=pltpu.SMEM),
                  pl.BlockSpec((tm, d), lambda i: (i, 0)),
                  pl.BlockSpec(memory_space=pl.ANY)],
        out_specs=pl.BlockSpec(memory_space=pl.ANY),
        out_shape=jax.ShapeDtypeStruct(xs_zero.shape, xs_zero.dtype),
        scratch_shapes=[pltpu.SemaphoreType.DMA],
        input_output_aliases={2: 0},
        compiler_params=_cparams(("arbitrary",)),
        name="moe_dispatch",
    )(dest3, hm, xs_zero)


def _row_gather(table, idx):
    n = idx.shape[0]
    d = table.shape[1]
    workers = SC_CORES * SC_SUBCORES
    per_worker = n // workers
    chunks = per_worker // SC_WINDOW
    assert n % workers == 0 and per_worker % SC_WINDOW == 0 and chunks >= 2
    mesh = plsc.VectorSubcoreMesh(core_axis_name="core", subcore_axis_name="subcore",
                                  num_cores=SC_CORES, num_subcores=SC_SUBCORES)

    @pl.kernel(out_type=jax.ShapeDtypeStruct((n, d), table.dtype), mesh=mesh,
               scratch_types=[pltpu.VMEM((2, SC_WINDOW), jnp.int32), pltpu.VMEM((2, SC_WINDOW, d), table.dtype),
                              pltpu.SemaphoreType.DMA((2,))],
               name="moe_row_gather")
    def gather(table_hbm, idx_hbm, out_hbm, idx_v, rows_v, sem):
        worker = lax.axis_index("subcore") * SC_CORES + lax.axis_index("core")

        def rows_of(c):
            return pl.ds(pl.multiple_of(worker * per_worker + c * SC_WINDOW, SC_WINDOW), SC_WINDOW)

        def fetch(c, b):
            return pltpu.make_async_copy(table_hbm.at[idx_v.at[b]], rows_v.at[b], sem.at[b])

        def start(c, b):
            pltpu.sync_copy(idx_hbm.at[rows_of(c)], idx_v.at[b])
            fetch(c, b).start()

        for b in range(2):
            start(b, b)

        @pl.loop(0, chunks, step=2)
        def _(c0):
            for b in range(2):
                c = c0 + b

                @pl.when(c < chunks)
                def _():
                    fetch(c, b).wait()
                    pltpu.sync_copy(rows_v.at[b], out_hbm.at[rows_of(c)])

                    @pl.when(c + 2 < chunks)
                    def _():
                        start(c + 2, b)

    return gather(table, idx)


def _expert_kernel(be_ref, bv_ref, xs_ref, wgu_ref, bgu_ref, wd_ref, bd_ref, y_ref, wgu_bf, wd_bf):
    i = pl.program_id(0)
    d_ff = wd_ref.shape[1]

    @pl.when((i == 0) | (be_ref[i] != be_ref[jnp.maximum(i - 1, 0)]))
    def _():
        wgu_bf[...] = wgu_ref[0].astype(BF16)
        wd_bf[...] = wd_ref[0].astype(BF16)

    @pl.when(bv_ref[i] > 0)
    def _():
        x = _unpack_halves(xs_ref[...]).astype(BF16)
        hgu = _dot(x, wgu_bf[...]) + bgu_ref[0]
        gate = jnp.minimum(hgu[:, :d_ff], SWIGLU_LIMIT)
        up = jnp.clip(hgu[:, d_ff:], -SWIGLU_LIMIT, SWIGLU_LIMIT)
        act = (up + 1.0) * gate * _sigmoid(SWIGLU_ALPHA * gate)
        y_ref[...] = _pack_halves(_dot(act.astype(BF16), wd_bf[...]) + bd_ref[0])

    @pl.when(bv_ref[i] == 0)
    def _():
        y_ref[...] = jnp.zeros(y_ref.shape, y_ref.dtype)


def _experts(block_expert, block_valid, xs, w_gu, b_gu3, w_down, b_down3):
    n_slots, dp = xs.shape
    n_blocks = n_slots // MOE_BLK
    d_ff, d = w_down.shape[1:]
    assert dp * 2 == d
    return pl.pallas_call(
        _expert_kernel,
        grid_spec=pltpu.PrefetchScalarGridSpec(
            num_scalar_prefetch=2,
            grid=(n_blocks,),
            in_specs=[pl.BlockSpec((MOE_BLK, dp), lambda i, be, bv: (i, 0)),
                      pl.BlockSpec((1, d, 2 * d_ff), lambda i, be, bv: (be[i], 0, 0)),
                      pl.BlockSpec((1, 1, 2 * d_ff), lambda i, be, bv: (be[i], 0, 0)),
                      pl.BlockSpec((1, d_ff, d), lambda i, be, bv: (be[i], 0, 0)),
                      pl.BlockSpec((1, 1, d), lambda i, be, bv: (be[i], 0, 0))],
            out_specs=pl.BlockSpec((MOE_BLK, dp), lambda i, be, bv: (i, 0)),
            scratch_shapes=[pltpu.VMEM((d, 2 * d_ff), BF16), pltpu.VMEM((d_ff, d), BF16)],
        ),
        out_shape=jax.ShapeDtypeStruct((n_slots, dp), jnp.uint32),
        compiler_params=_cparams(("arbitrary",)),
        name="moe_experts",
    )(block_expert, block_valid, xs, w_gu, b_gu3, w_down, b_down3)


def _combine_kernel(yk_ref, tg_ref, x1_ref, y_ref):
    tg = tg_ref[...]
    lane = lax.broadcasted_iota(jnp.int32, tg.shape, 1)
    acc = x1_ref[...]
    for k in range(TOP_K):
        gk = jnp.sum(jnp.where(lane == k, tg, 0.0), axis=-1, keepdims=True)
        acc = acc + gk * _unpack_halves(yk_ref[k])
    y_ref[...] = acc


def _combine(yk, tg, x1):
    n, d = x1.shape
    tm = POST_TILE
    return pl.pallas_call(
        _combine_kernel,
        grid=(n // tm,),
        in_specs=[pl.BlockSpec((TOP_K, tm, yk.shape[2]), lambda i: (0, i, 0)),
                  pl.BlockSpec((tm, LANES), lambda i: (i, 0)),
                  pl.BlockSpec((tm, d), lambda i: (i, 0))],
        out_specs=pl.BlockSpec((tm, d), lambda i: (i, 0)),
        out_shape=jax.ShapeDtypeStruct((n, d), F32),
        compiler_params=_cparams(("parallel",)),
        name="moe_combine",
    )(yk, tg, x1)


def _moe(hm, ti, tg, x1, w_gu, b_gu, w_down, b_down):
    n, d = hm.shape
    rank, cnt = _rank(ti)
    counts = cnt[0, :N_EXPERTS].astype(jnp.int32)
    padded = (counts + MOE_BLK - 1) // MOE_BLK * MOE_BLK
    pad_end = jnp.cumsum(padded)
    pad_start = pad_end - padded
    top_i = ti[:, :TOP_K]
    dest = (pad_start[top_i] + rank[:, :TOP_K]).astype(jnp.int32)
    dest3 = dest.reshape(n // ROW_TILE, 1, ROW_TILE * TOP_K)
    n_blocks = (n * TOP_K) // MOE_BLK + N_EXPERTS
    blk_start = jnp.arange(n_blocks, dtype=jnp.int32) * MOE_BLK
    block_expert = jnp.minimum(jnp.sum((pad_end[None, :] <= blk_start[:, None]).astype(jnp.int32), axis=1),
                               N_EXPERTS - 1)
    block_valid = (blk_start < pad_end[-1]).astype(jnp.int32)
    xs = _dispatch(dest3, hm, jnp.zeros((n_blocks * MOE_BLK, d), hm.dtype))
    yb = _experts(block_expert, block_valid, xs, w_gu, b_gu.reshape(N_EXPERTS, 1, -1), w_down,
                  b_down.reshape(N_EXPERTS, 1, -1))
    yk = _row_gather(yb, dest.T.reshape(TOP_K * n)).reshape(TOP_K, n, d)
    return _combine(yk, tg, x1)


def _pad_lanes(v, fill=0.0):
    v = v.reshape(1, -1).astype(F32)
    return jnp.pad(v, ((0, 0), (0, LANES - v.shape[1])), constant_values=fill)


def kernel(x_prompt, x_sample, cache_k, cache_v, state_ssm, state_conv, page_table, meta_tokens, rel_bias_table, norm1, w_in, conv_w, a_log, dt_bias, gdn_norm, q_norm, k_norm, lam_q1, lam_k1, lam_q2, lam_k2, sub_norm, w_br_a, w_br_b, w_out, norm2, w_router, b_router, w_gu, b_gu, w_down, b_down):
    bsz, seq, d = x_prompt.shape
    dbs, dseq, _ = x_sample.shape
    depth = w_in.shape[0]
    assert depth == 1 and dseq <= SUBLANES - (CONV_W - 1) and seq % ATT_BLK == 0
    lam_init = 0.8 - 0.6 * math.exp(-0.3 * 0)
    l = 0

    w = w_in[l]
    o_z = CONV_DIM + VA
    w_main = jnp.concatenate([w[:, :o_z], w[:, o_z + 2 * H_A:]], axis=1).astype(BF16)
    w_ba = jnp.pad(w[:, o_z:o_z + 2 * H_A], ((0, 0), (0, LANES - 2 * H_A))).astype(BF16)
    grp = np.arange(QB) // DK_B
    gmat = jnp.asarray((grp[:, None] == grp[None, :]).astype(np.float32) / DK_B, dtype=BF16)
    qn_t = jnp.tile(q_norm[l].astype(F32), QB // DK_B).reshape(1, QB)
    kn_t = jnp.tile(k_norm[l].astype(F32), QB // DK_B).reshape(1, QB)
    alog_p = jnp.pad(a_log[l].astype(F32), (H_A, LANES - 2 * H_A)).reshape(1, LANES)
    dtb_p = jnp.pad(dt_bias[l].astype(F32), (H_A, LANES - 2 * H_A)).reshape(1, LANES)
    n1 = norm1[l].reshape(1, d).astype(F32)
    proj = functools.partial(_inproj, n1=n1, w_main=w_main, w_ba=w_ba, gmat=gmat, qn_t=qn_t, kn_t=kn_t,
                             alog_p=alog_p, dtb_p=dtb_p)
    lam4 = jnp.stack([lam_q1[l], lam_k1[l], lam_q2[l], lam_k2[l]]).astype(F32)
    sn = sub_norm[l].reshape(1, DV_B).astype(F32)
    gn = gdn_norm[l].reshape(1, DV_A).astype(F32)
    cw = conv_w[l].astype(F32)

    ii = np.arange(ATT_BLK)[:, None]
    jj = np.arange(LANES)[None, :]
    bk_prompt = np.stack([_bucket_np(ii - jj), _bucket_np(ATT_BLK + ii - jj)])
    t8 = np.arange(SUBLANES)[:, None]
    bk_sample = np.stack([_bucket_np(PAGE + t8 - jj), _bucket_np(t8 - jj)])
    bt_prompt = _bias_tiles(rel_bias_table.astype(F32), bk_prompt)
    bt_sample = _bias_tiles(rel_bias_table.astype(F32), bk_sample)
    far_h = rel_bias_table[N_BUCKETS - 1].astype(F32)
    far_p = jnp.broadcast_to(far_h[:, None, None], (H_B, 1, ATT_KV))
    diag_t = jnp.where(jnp.asarray(ii >= jj), bt_prompt[:, 0], NEG)
    near_t = bt_prompt[:, 1]
    far_t = jnp.broadcast_to(far_h[:, None, None], near_t.shape)
    neg_t = jnp.full(near_t.shape, NEG, F32)
    tile2 = lambda a, b_, c_, d_: jnp.concatenate([jnp.concatenate([a, b_], axis=2),
                                                   jnp.concatenate([c_, d_], axis=2)], axis=1)
    bt_prompt = jnp.stack([tile2(diag_t, neg_t, near_t, diag_t), tile2(far_t, near_t, far_t, far_t)], axis=1)
    nrow = H_B * 2 * SUBLANES
    rows_of = lambda t: jnp.broadcast_to(t[:, None], (H_B, 2, SUBLANES, LANES)).reshape(nrow, LANES)
    last_t, self_t = rows_of(bt_sample[:, 0]), rows_of(bt_sample[:, 1])
    far_col = jnp.broadcast_to(far_h[:, None, None], (H_B, 2 * SUBLANES, LANES)).reshape(nrow, LANES)

    lreal = N_META + seq
    lp = -(-lreal // ATT_KV) * ATT_KV
    fp = lp - lreal
    assert fp % SUBLANES == 0 and fp >= GDN_CHUNK and (fp + N_META) % ATT_BLK == 0 and lp % GDN_CHUNK == 0
    xp = jnp.concatenate([jnp.zeros((bsz, fp, d), F32),
                          jnp.broadcast_to(meta_tokens.astype(F32)[None], (bsz, N_META, d)), x_prompt], axis=1)
    conv_p, z_p, bg_p, qd_p, kd_p, vd_p, sga_p, sgb_p, kdb_p, vdb_p = proj(xp.reshape(bsz * lp, d), ATT_KV)
    r3 = lambda a, b_, r_: a.reshape(b_, r_, a.shape[-1])
    oa_p, ssm_p = _gdn(r3(conv_p, bsz, lp), r3(bg_p, bsz, lp), r3(z_p, bsz, lp),
                       jnp.zeros((bsz, H_A, DK_A, DV_A), F32), cw, gn, fp, GDN_CHUNK, 1, H_A)
    kbias = jnp.where(jnp.arange(lp) < fp, NEG, 0.0).astype(F32).reshape(lp // ATT_KV, ATT_KV)
    ob_p = _attn_prompt(r3(qd_p, bsz, lp), r3(kdb_p, bsz, lp), r3(vdb_p, bsz, lp), bt_prompt, far_p, kbias,
                        lam4, sn, fp // ATT_Q, lam_init)

    ns = dbs * dseq
    conv_s, z_s, bg_s, qd_s, kd_s, vd_s, sga_s, sgb_s, _, _ = proj(x_sample.reshape(ns, d), min(ns, 256))
    cs = SUBLANES
    rs = 2 * cs
    fs = rs - dseq

    def chunk_rows(a, head=None):
        a = a.reshape(dbs, dseq, a.shape[-1]).astype(F32)
        parts = [jnp.zeros((dbs, fs - (0 if head is None else head.shape[1]), a.shape[-1]), F32)]
        if head is not None:
            parts.append(head.astype(F32))
        return jnp.concatenate(parts + [a], axis=1)

    oa_s, ssm_s = _gdn(chunk_rows(conv_s, state_conv[l]), chunk_rows(bg_s), chunk_rows(z_s),
                       state_ssm[l].astype(F32), cw, gn, fs, cs, math.gcd(dbs, 4), H_A)
    pad8 = lambda a: jnp.pad(a.reshape(dbs, dseq, a.shape[-1]), ((0, 0), (0, SUBLANES - dseq), (0, 0)))
    n_pool = cache_k.shape[1]
    ck_t = jnp.transpose(cache_k[l], (0, 2, 3, 4, 1)).reshape(n_pool, QB, PAGE)
    cv_i = cache_v[l].reshape(n_pool, PAGE * H_B, DV_B)
    ob_s = _attn_sample(page_table, pad8(qd_s).astype(F32), pad8(kd_s), pad8(vd_s), ck_t, cv_i,
                        last_t, self_t, far_col, lam4, sn, dseq, lam_init)

    wa, wb, wo = w_br_a[l].astype(BF16), w_br_b[l].astype(BF16), w_out[l].astype(BF16)
    n2 = norm2[l].reshape(1, d).astype(F32)
    wr = jnp.pad(w_router[l], ((0, 0), (0, LANES - N_EXPERTS))).astype(BF16)
    br = _pad_lanes(b_router[l])
    tiles_b = seq // POST_TILE
    tiles_lp = lp // POST_TILE
    skip = (fp + N_META) // POST_TILE
    npt = bsz * seq
    assert ns % POST_TILE == 0 and seq % POST_TILE == 0 and (fp + N_META) % POST_TILE == 0
    oa_s2 = oa_s[:, fs:].reshape(ns, VA)
    ob_s2 = ob_s[:, :dseq].reshape(ns, VB)
    x1, hm, ti, tg = _post(
        (x_prompt.reshape(npt, d), oa_p.reshape(bsz * lp, VA), ob_p.reshape(bsz * lp, VB), sga_p, sgb_p),
        (x_sample.reshape(ns, d), oa_s2, ob_s2, sga_s, sgb_s), (wa, wb, wo, n2, wr, br), npt, ns,
        lambda i: (i // tiles_b) * tiles_lp + skip + i % tiles_b)

    y = _moe(hm, ti, tg, x1, w_gu[l], b_gu[l], w_down[l], b_down[l])
    y_prompt = y[:npt].reshape(bsz, seq, d)
    y_sample = y[npt:].reshape(dbs, dseq, d)

    k_prompt = r3(kd_p, bsz, lp)[:, fp:].reshape(1, bsz, lreal, H_B, 2, DK_B)
    v_prompt = r3(vd_p, bsz, lp)[:, fp:].reshape(1, bsz, lreal, H_B, DV_B)
    conv_prompt = r3(conv_p, bsz, lp)[:, lp - (CONV_W - 1):][None]
    xpad_s = jnp.concatenate([state_conv[l].astype(F32), conv_s.reshape(dbs, dseq, CONV_DIM)], axis=1)
    conv_sample = xpad_s[:, dseq:][None]
    return (y_prompt, y_sample, k_prompt, v_prompt, ssm_p[None], conv_prompt,
            kd_s.reshape(1, dbs, dseq, H_B, 2, DK_B), vd_s.reshape(1, dbs, dseq, H_B, DV_B), ssm_s[None], conv_sample)
```

```python
import functools
import math

import numpy as np
import jax
import jax.numpy as jnp
from jax import lax
from jax.experimental import pallas as pl
from jax.experimental.pallas import tpu as pltpu
from jax.experimental.pallas import tpu_sc as plsc

F32 = jnp.float32
BF16 = jnp.bfloat16
HI = lax.Precision.HIGHEST

N_META = 16
H_A, DK_A, DV_A = 4, 128, 128
CONV_W = 4
H_B, DK_B = 4, 64
DV_B = 2 * DK_B
N_BUCKETS, MAX_DIST = 32, 128
N_EXPERTS, TOP_K = 32, 4
SWIGLU_LIMIT, SWIGLU_ALPHA = 7.0, 1.702
EPS = 1e-6
PAGE = 128
QA = H_A * DK_A
VA = H_A * DV_A
QB = H_B * 2 * DK_B
VB = H_B * DV_B
CONV_DIM = 2 * QA + VA

LANES = 128
SUBLANES = 8
VMEM_LIMIT = 56 * 1024 * 1024
SC_CORES, SC_SUBCORES, SC_WINDOW = 2, 16, 64

GDN_CHUNK = 64
ATT_BLK = 128
ATT_KV = 2 * ATT_BLK
ATT_Q = 2 * ATT_BLK
PAGES_PER_STEP = 16
MOE_BLK = 256
ROW_TILE = 128
POST_TILE = 256
NEG = -1e30


def _dot(a, b, prec=None):
    return jnp.dot(a, b, preferred_element_type=F32, precision=prec)


def _dot_nt(a, b, prec=None):
    return lax.dot_general(a, b, (((1,), (1,)), ((), ())), preferred_element_type=F32, precision=prec)


def _dot_tn(a, b, prec=None):
    return lax.dot_general(a, b, (((0,), (0,)), ((), ())), preferred_element_type=F32, precision=prec)


def _split(a):
    hi = a.astype(BF16)
    return hi, (a - hi.astype(F32)).astype(BF16)


def _lhs3(a):
    hi, lo = _split(a)
    return jnp.concatenate([hi, lo, hi], axis=1)


def _rhs3(b):
    hi, lo = _split(b)
    return jnp.concatenate([hi, hi, lo], axis=0)


def _pack_halves(x):
    w = x.shape[1] // 2
    bits = lambda v: pltpu.bitcast(v.astype(BF16).astype(F32), jnp.uint32)
    return (bits(x[:, :w]) >> 16) | (bits(x[:, w:]) & jnp.uint32(0xFFFF0000))


def _unpack_halves(u):
    lo = pltpu.bitcast(u << 16, F32)
    hi = pltpu.bitcast(u & jnp.uint32(0xFFFF0000), F32)
    return jnp.concatenate([lo, hi], axis=1)


def _sigmoid(x):
    return 1.0 / (1.0 + jnp.exp(-x))


def _cparams(sem, flags=None):
    return pltpu.CompilerParams(dimension_semantics=sem, vmem_limit_bytes=VMEM_LIMIT, flags=flags)


def _bucket_np(d):
    d = np.maximum(d, 0)
    df = np.maximum(d, 1).astype(np.float32)
    max_exact = N_BUCKETS // 2
    large = max_exact + (np.log(df / np.float32(max_exact)) / np.float32(math.log(MAX_DIST / max_exact))
                         * np.float32(N_BUCKETS - max_exact)).astype(np.int32)
    return np.where(d < max_exact, d, np.minimum(large, N_BUCKETS - 1)).astype(np.int32)


def _bias_kernel(tab_ref, bk_ref, o_ref):
    for t in range(bk_ref.shape[0]):
        bk = bk_ref[t]
        for h in range(H_B):
            acc = jnp.zeros(bk.shape, F32)
            for b in range(N_BUCKETS):
                acc = jnp.where(bk == b, tab_ref[b, h], acc)
            o_ref[h, t] = acc


def _bias_tiles(table, buckets):
    t, r, _ = buckets.shape
    return pl.pallas_call(
        _bias_kernel,
        out_shape=jax.ShapeDtypeStruct((H_B, t, r, LANES), F32),
        in_specs=[pl.BlockSpec(memory_space=pltpu.SMEM), pl.BlockSpec(memory_space=pltpu.VMEM)],
        out_specs=pl.BlockSpec(memory_space=pltpu.VMEM),
        name="rel_bias_tiles",
    )(table, jnp.asarray(buckets))


def _inproj_kernel(x_ref, n1_ref, w_ref, wba_ref, gm_ref, qn_ref, kn_ref, alog_ref, dtb_ref,
                   conv_ref, z_ref, bg_ref, qd_ref, kd_ref, vd_ref, sga_ref, sgb_ref, kdb_ref, vdb_ref):
    x = x_ref[...]
    ms = jnp.mean(x * x, axis=-1, keepdims=True)
    h = (x * lax.rsqrt(ms + EPS) * n1_ref[...]).astype(BF16)
    o = 0
    conv_ref[...] = _dot(h, w_ref[:, o:o + CONV_DIM]); o += CONV_DIM
    z_ref[...] = _dot(h, w_ref[:, o:o + VA]).astype(z_ref.dtype); o += VA

    def group_norm(y, g):
        sq = y * y
        hi = sq.astype(BF16)
        lo = (sq - hi.astype(F32)).astype(BF16)
        msq = _dot(hi, gm_ref[...]) + _dot(lo, gm_ref[...])
        return y * lax.rsqrt(msq + EPS) * g

    qb = _dot(h, w_ref[:, o:o + QB]); o += QB
    qd_ref[...] = (group_norm(qb, qn_ref[...]) * (DK_B ** -0.5)).astype(qd_ref.dtype)
    kb = _dot(h, w_ref[:, o:o + QB]); o += QB
    kd = group_norm(kb, kn_ref[...])
    kd_ref[...] = kd
    kdb_ref[...] = kd.astype(kdb_ref.dtype)
    vd = _dot(h, w_ref[:, o:o + VB]); o += VB
    vd_ref[...] = vd
    vdb_ref[...] = vd.astype(vdb_ref.dtype)
    d_model = x.shape[1]
    sga_ref[...] = _sigmoid(_dot(h, w_ref[:, o:o + d_model])).astype(sga_ref.dtype); o += d_model
    sgb_ref[...] = _sigmoid(_dot(h, w_ref[:, o:o + d_model])).astype(sgb_ref.dtype)
    t = _dot(h, wba_ref[...])
    lane = lax.broadcasted_iota(jnp.int32, t.shape, 1)
    ta = t + dtb_ref[...]
    sp = jnp.maximum(ta, 0.0) + jnp.log(1.0 + jnp.exp(-jnp.abs(ta)))
    bg_ref[...] = jnp.where(lane < H_A, _sigmoid(t), -jnp.exp(alog_ref[...]) * sp)


def _inproj(x2d, tm, n1, w_main, w_ba, gmat, qn_t, kn_t, alog_p, dtb_p):
    n, d = x2d.shape
    assert n % tm == 0
    row = lambda c: pl.BlockSpec((tm, c), lambda i: (i, 0))
    full = lambda a: pl.BlockSpec(a.shape, lambda i: (0,) * a.ndim)
    outs = [(CONV_DIM, F32), (VA, BF16), (LANES, F32), (QB, BF16), (QB, F32), (VB, F32), (d, BF16), (d, BF16),
            (QB, BF16), (VB, BF16)]
    return pl.pallas_call(
        _inproj_kernel,
        grid=(n // tm,),
        in_specs=[row(d)] + [full(a) for a in (n1, w_main, w_ba, gmat, qn_t, kn_t, alog_p, dtb_p)],
        out_specs=[row(c) for c, _ in outs],
        out_shape=[jax.ShapeDtypeStruct((n, c), dt) for c, dt in outs],
        compiler_params=_cparams(("parallel",)),
        name="in_proj",
    )(x2d, n1, w_main, w_ba, gmat, qn_t, kn_t, alog_p, dtb_p)


def _gdn_kernel(fv, c, q_ref, k_ref, v_ref, cwq_ref, cwk_ref, cwv_ref, bg_ref, z_ref, s0_ref, gn_ref,
                o_ref, st_ref):
    nb, r, wd = q_ref.shape
    hg = wd // LANES
    head0 = pl.program_id(1) * hg
    chains = [(b, hh) for b in range(nb) for hh in range(hg)]

    ri = lax.broadcasted_iota(jnp.int32, (c, c), 0)
    ci = lax.broadcasted_iota(jnp.int32, (c, c), 1)
    incl = ri >= ci
    strict = ri > ci
    eye = (ri == ci).astype(F32)
    lane = lax.broadcasted_iota(jnp.int32, (c, LANES), 1)
    sub_t = lax.broadcasted_iota(jnp.int32, (LANES, c), 0)
    rowid = lax.broadcasted_iota(jnp.int32, (c, 1), 0)
    colid = lax.broadcasted_iota(jnp.int32, (1, c), 1)

    def conv(x_ref, cw_ref, b, cols, r0):
        w = x_ref[b, pl.ds(r0 - SUBLANES, c + SUBLANES), cols]
        acc = w[SUBLANES - 3:SUBLANES - 3 + c] * cw_ref[0:1, cols]
        for j in range(1, CONV_W):
            acc = acc + w[SUBLANES - 3 + j:SUBLANES - 3 + j + c] * cw_ref[j:j + 1, cols]
        return acc * _sigmoid(acc)

    def elementwise(b, hh, r0):
        cols = slice(hh * LANES, (hh + 1) * LANES)
        head = head0 + hh
        vcol = ((r0 + rowid) >= fv).astype(F32)
        vrow = ((r0 + colid) >= fv).astype(F32)
        qv = conv(q_ref, cwq_ref, b, cols, r0)
        kv = conv(k_ref, cwk_ref, b, cols, r0)
        v = conv(v_ref, cwv_ref, b, cols, r0) * vcol
        q = qv * lax.rsqrt(jnp.sum(qv * qv, axis=-1, keepdims=True) + EPS) * (DK_A ** -0.5) * vcol
        k = kv * lax.rsqrt(jnp.sum(kv * kv, axis=-1, keepdims=True) + EPS) * vcol
        bgc = bg_ref[b, pl.ds(r0, c), :]
        beta = jnp.sum(jnp.where(lane == head, bgc, 0.0), axis=-1, keepdims=True) * vcol
        g_col = jnp.sum(jnp.where(lane == H_A + head, bgc, 0.0), axis=-1, keepdims=True) * vcol
        g_row = jnp.sum(jnp.where(sub_t == H_A + head, bgc.T, 0.0), axis=0, keepdims=True) * vrow
        gc = jnp.sum(jnp.where(incl, g_row, 0.0), axis=-1, keepdims=True)
        gr = jnp.sum(jnp.where(ri <= ci, g_col, 0.0), axis=0, keepdims=True)
        decay = jnp.where(incl, jnp.exp(jnp.where(incl, gc - gr, 0.0)), 0.0)
        kb = k * beta
        eg = jnp.exp(gc)
        g_last = jnp.sum(jnp.where(rowid == c - 1, gc, 0.0), axis=0, keepdims=True)
        k_dt = (k * jnp.exp(g_last - gc)).T
        lhs = jnp.concatenate([kb, q], axis=0).astype(BF16)
        rhs = jnp.concatenate([v * beta, kb * eg], axis=-1)
        return lhs, k.astype(BF16), decay, rhs, q * eg, k_dt, jnp.exp(g_last)

    def prepare_a(j):
        r0 = pl.multiple_of(j * c, c)
        vec = [elementwise(b, hh, r0) for b, hh in chains]
        kk = [_dot_nt(v_[0], v_[1]) for v_ in vec]
        return vec, kk

    def prepare_b(vec, kk):
        low = [jnp.where(strict, kk_[:c] * v_[2], 0.0) for v_, kk_ in zip(vec, kk)]
        intra = [kk_[c:] * v_[2] for v_, kk_ in zip(vec, kk)]
        inv = [eye - lw for lw in low]
        levels = int(math.log2(c)) - 1
        pw = [_dot(_lhs3(lw), _rhs3(lw)) for lw in low]
        for lev in range(levels):
            pw_r = [_rhs3(p) for p in pw]
            if lev + 1 < levels:
                pw = [_dot(_lhs3(p), r_) for p, r_ in zip(pw, pw_r)]
            inv = [iv + _dot(_lhs3(iv), r_) for iv, r_ in zip(inv, pw_r)]
        sol = [_dot(_lhs3(iv), _rhs3(v_[3])) for iv, v_ in zip(inv, vec)]
        out = []
        for v_, sl, it in zip(vec, sol, intra):
            on_state = jnp.concatenate([sl[:, DV_A:], v_[4]], axis=0).astype(BF16)
            on_u = jnp.concatenate([it, v_[5]], axis=0).astype(BF16)
            out.append((sl[:, :DV_A], on_state, on_u, v_[6]))
        return tuple(out)

    def apply_a(prepared, states):
        return [_dot(p[1], s.astype(BF16)) for p, s in zip(prepared, states)]

    def apply_b(j, prepared, states, ps):
        r0 = pl.multiple_of(j * c, c)
        u = [p[0] - ps_[:c] for p, ps_ in zip(prepared, ps)]
        pu = [_dot(p[2], u_.astype(BF16)) for p, u_ in zip(prepared, u)]
        new_states = []
        for (b, hh), p, s, ps_, pu_ in zip(chains, prepared, states, ps, pu):
            cols = slice(hh * LANES, (hh + 1) * LANES)
            o = ps_[c:] + pu_[:c]
            on = o * lax.rsqrt(jnp.mean(o * o, axis=-1, keepdims=True) + EPS) * gn_ref[...]
            zc = z_ref[b, pl.ds(r0, c), cols].astype(F32)
            o_ref[b, pl.ds(r0, c), cols] = (on * (zc * _sigmoid(zc))).astype(o_ref.dtype)
            new_states.append(s * p[3] + pu_[c:])
        return tuple(new_states)

    def chunk(j, carry):
        prepared, states = carry
        ps = apply_a(prepared, states)
        vec, kk = prepare_a(j + 1)
        states = apply_b(j, prepared, states, ps)
        return prepare_b(vec, kk), states

    j0 = fv // c
    last = r // c - 1
    o_ref[:, 0:j0 * c, :] = jnp.zeros((nb, j0 * c, wd), o_ref.dtype)
    init = (prepare_b(*prepare_a(j0)), tuple(s0_ref[b, hh] for b, hh in chains))
    prepared, states = lax.fori_loop(j0, last, chunk, init)
    for (b, hh), s in zip(chains, apply_b(last, prepared, states, apply_a(prepared, states))):
        st_ref[b, hh] = s


def _gdn(conv3, bg3, z3, s0, conv_w, gn, fv, c, nb, hg):
    bsz, r, _ = conv3.shape
    assert r % c == 0 and fv // c >= 1 and c >= SUBLANES and bsz % nb == 0 and H_A % hg == 0
    ng = H_A // hg
    wd = hg * LANES
    blk = lambda off: pl.BlockSpec((nb, r, wd), lambda i, g: (i, 0, off * ng + g))
    cw = lambda off: pl.BlockSpec((CONV_W, wd), lambda i, g: (0, off * ng + g))
    st = pl.BlockSpec((nb, hg, DK_A, DV_A), lambda i, g: (i, g, 0, 0))
    return pl.pallas_call(
        functools.partial(_gdn_kernel, fv, c),
        grid=(bsz // nb, ng),
        in_specs=[blk(0), blk(1), blk(2), cw(0), cw(1), cw(2),
                  pl.BlockSpec((nb, r, LANES), lambda i, g: (i, 0, 0)), blk(0), st,
                  pl.BlockSpec((1, LANES), lambda i, g: (0, 0))],
        out_specs=[blk(0), st],
        out_shape=[jax.ShapeDtypeStruct((bsz, r, VA), F32), jax.ShapeDtypeStruct((bsz, H_A, DK_A, DV_A), F32)],
        compiler_params=_cparams(("parallel", "parallel")),
        name="gdn",
    )(conv3, conv3, conv3, conv_w, conv_w, conv_w, bg3, z3, s0, gn)


def _lambda(lam_ref, lam_init):
    l1 = jnp.sum(lam_ref[0:1, :] * lam_ref[1:2, :], axis=-1, keepdims=True)
    l2 = jnp.sum(lam_ref[2:3, :] * lam_ref[3:4, :], axis=-1, keepdims=True)
    return jnp.exp(l1) - jnp.exp(l2) + lam_init


def _sub_norm(o0, o1, lam, sn, lam_init):
    ob = o0 - lam * o1
    return ob * lax.rsqrt(jnp.mean(ob * ob, axis=-1, keepdims=True) + EPS) * sn * (1.0 - lam_init)


def _attn_prompt_kernel(first_q, lam_init, q_ref, k_ref, v_ref, bt_ref, far_ref, kb_ref, lam_ref, sn_ref, o_ref):
    qi = pl.program_id(2)

    @pl.when(qi < first_q)
    def _():
        o_ref[0] = jnp.zeros(o_ref.shape[1:], o_ref.dtype)

    @pl.when(qi >= first_q)
    def _():
        q = q_ref[0]
        lane = lax.broadcasted_iota(jnp.int32, q.shape, 1)
        zero = jnp.zeros_like(q)
        qs = jnp.concatenate([jnp.where(lane < DK_B, q, zero), jnp.where(lane >= DK_B, q, zero)], axis=0)

        def scores(j):
            k0 = pl.multiple_of(j * ATT_KV, ATT_KV)
            return _dot_nt(qs, k_ref[0, pl.ds(k0, ATT_KV), :])

        def step(j, carry, bias, last=False):
            m, l, acc, s_raw = carry
            s_next = s_raw if last else scores(j + 1)
            k0 = pl.multiple_of(j * ATT_KV, ATT_KV)
            s = s_raw + bias
            m_new = jnp.maximum(m, jnp.max(s, axis=-1, keepdims=True))
            alpha = jnp.exp(m - m_new)
            p = jnp.exp(s - m_new)
            l = alpha * l + jnp.sum(p, axis=-1, keepdims=True)
            acc = alpha * acc + _dot(p.astype(BF16), v_ref[0, pl.ds(k0, ATT_KV), :])
            return m_new, l, acc, s_next

        def tile_bias(j, which):
            t = bt_ref[0, which] + kb_ref[pl.ds(j, 1), :]
            return jnp.concatenate([t, t], axis=0)

        carry = (jnp.full((2 * ATT_Q, 1), NEG, F32), jnp.zeros((2 * ATT_Q, 1), F32),
                 jnp.zeros((2 * ATT_Q, DV_B), F32), scores(0))
        carry = lax.fori_loop(0, qi - 1, lambda j, c: step(j, c, far_ref[0] + kb_ref[pl.ds(j, 1), :]), carry)
        carry = lax.cond(qi >= 1, lambda c: step(qi - 1, c, tile_bias(qi - 1, 1)), lambda c: c, carry)
        _, l, acc, _ = step(qi, carry, tile_bias(qi, 0), last=True)
        o = acc / l
        o_ref[0] = _sub_norm(o[:ATT_Q], o[ATT_Q:], _lambda(lam_ref, lam_init), sn_ref[...],
                             lam_init).astype(o_ref.dtype)


def _attn_prompt(qd3, kd3, vd3, btiles, far, kbias, lam4, sn, first_q, lam_init):
    bsz, lp, _ = qd3.shape
    assert lp % ATT_KV == 0 and ATT_Q == ATT_KV
    kv = pl.BlockSpec((1, lp, LANES), lambda b, h, i: (b, 0, h))
    qo = pl.BlockSpec((1, ATT_Q, LANES), lambda b, h, i: (b, i, h))
    full = lambda a: pl.BlockSpec(a.shape, lambda b, h, i: (0,) * a.ndim)
    return pl.pallas_call(
        functools.partial(_attn_prompt_kernel, first_q, lam_init),
        grid=(bsz, H_B, lp // ATT_Q),
        in_specs=[qo, kv, kv,
                  pl.BlockSpec((1, 2, ATT_Q, ATT_KV), lambda b, h, i: (h, 0, 0, 0)),
                  pl.BlockSpec((1, 1, ATT_KV), lambda b, h, i: (h, 0, 0)),
                  full(kbias), full(lam4), full(sn)],
        out_specs=qo,
        out_shape=jax.ShapeDtypeStruct((bsz, lp, VB), F32),
        compiler_params=_cparams(("parallel", "parallel", "parallel")),
        name="diff_attn_prompt",
    )(qd3, kd3, vd3, btiles, far, kbias, lam4, sn)


def _attn_sample_kernel(npp, t_new, lam_init, pt_ref, q_ref, kn_ref, vn_ref, ck_ref, cv_ref, last_ref, self_ref,
                        far_ref, lam_ref, sn_ref, o_ref, qm, kself, vself, m_s, l_s, acc_s, kbuf, vbuf, sem):
    b = pl.program_id(0)
    st = pl.program_id(1)
    nb = pl.num_programs(0)
    nst = pl.num_programs(1)
    nrow = H_B * 2 * SUBLANES
    hrows = 2 * SUBLANES
    t = b * nst + st
    slot = t % 2

    def page_copies(bb, ss, sl):
        out = []
        for g in range(npp):
            page = pt_ref[bb, ss * npp + g]
            out.append(pltpu.make_async_copy(ck_ref.at[page], kbuf.at[sl, g], sem.at[sl]))
            out.append(pltpu.make_async_copy(cv_ref.at[page], vbuf.at[sl, g], sem.at[sl]))
        return out

    @pl.when(t == 0)
    def _():
        for cp in page_copies(0, 0, 0):
            cp.start()

    @pl.when(t + 1 < nb * nst)
    def _():
        wrap = st + 1 == nst
        for cp in page_copies(jnp.where(wrap, b + 1, b), jnp.where(wrap, 0, st + 1), 1 - slot):
            cp.start()

    for cp in page_copies(b, st, slot):
        cp.wait()

    def update(s, values_of_head):
        m_old = m_s[...]
        m_new = jnp.maximum(m_old, jnp.max(s, axis=-1, keepdims=True))
        alpha = jnp.exp(m_old - m_new)
        pr = jnp.exp(s - m_new)
        l_s[...] = alpha * l_s[...] + jnp.sum(pr, axis=-1, keepdims=True)
        prb = pr.astype(BF16)
        pv = [_dot(prb[h * hrows:(h + 1) * hrows], values_of_head(h)) for h in range(H_B)]
        acc_s[...] = alpha * acc_s[...] + jnp.concatenate(pv, axis=0)
        m_s[...] = m_new

    @pl.when(st == 0)
    def _():
        q8 = q_ref[0]
        lane = lax.broadcasted_iota(jnp.int32, q8.shape, 1)
        qm[...] = jnp.concatenate([jnp.where(lane // DK_B == g, q8, 0.0) for g in range(2 * H_B)],
                                  axis=0).astype(qm.dtype)
        kself[...] = jnp.zeros(kself.shape, F32)
        vself[...] = jnp.zeros(vself.shape, F32)
        kself[0:SUBLANES, :] = kn_ref[0]
        vself[0:SUBLANES, :] = vn_ref[0]
        m_s[...] = jnp.full(m_s.shape, NEG, F32)
        l_s[...] = jnp.zeros(l_s.shape, F32)
        acc_s[...] = jnp.zeros(acc_s.shape, F32)
        rows = lax.broadcasted_iota(jnp.int32, (nrow, PAGE), 0) % SUBLANES
        cols = lax.broadcasted_iota(jnp.int32, (nrow, PAGE), 1)
        ok = (cols <= rows) & (cols < t_new)
        s = jnp.where(ok, _dot_nt(qm[...], kself[...].astype(BF16)) + self_ref[...], NEG)
        update(s, lambda h: vself[:, h * DV_B:(h + 1) * DV_B].astype(BF16))

    far = far_ref[:, 0:1]
    parts = []
    for g in range(npp):
        bias = far if g < npp - 1 else jnp.where(st == nst - 1, last_ref[...], far)
        parts.append(_dot(qm[...], kbuf[slot, g].astype(BF16)) + bias)

    def page_values(h):
        return jnp.concatenate([vbuf[slot, g, pl.ds(h, PAGE, stride=H_B), :] for g in range(npp)],
                               axis=0).astype(BF16)

    update(jnp.concatenate(parts, axis=1), page_values)

    @pl.when(st == nst - 1)
    def _():
        o = acc_s[...] / l_s[...]
        lam = _lambda(lam_ref, lam_init)
        for h in range(H_B):
            r0 = h * hrows
            o_ref[0, :, h * DV_B:(h + 1) * DV_B] = _sub_norm(o[r0:r0 + SUBLANES], o[r0 + SUBLANES:r0 + hrows], lam,
                                                             sn_ref[...], lam_init)


def _attn_sample(page_table, q8, kn8, vn8, ck_t, cv_i, last_t, self_t, far_col, lam4, sn, t_new, lam_init):
    bsz, npg = page_table.shape
    npp = math.gcd(PAGES_PER_STEP, npg)
    nrow = H_B * 2 * SUBLANES
    tok = pl.BlockSpec((1, SUBLANES, QB), lambda b, s, pt: (b, 0, 0))
    hbm = pl.BlockSpec(memory_space=pl.ANY)
    full = lambda a: pl.BlockSpec(a.shape, lambda b, s, pt: (0,) * a.ndim)
    return pl.pallas_call(
        functools.partial(_attn_sample_kernel, npp, t_new, lam_init),
        grid_spec=pltpu.PrefetchScalarGridSpec(
            num_scalar_prefetch=1,
            grid=(bsz, npg // npp),
            in_specs=[tok, tok, tok, hbm, hbm, full(last_t), full(self_t), full(far_col), full(lam4), full(sn)],
            out_specs=tok,
            scratch_shapes=[pltpu.VMEM((nrow, QB), BF16), pltpu.VMEM((PAGE, QB), F32), pltpu.VMEM((PAGE, VB), F32),
                            pltpu.VMEM((nrow, 1), F32), pltpu.VMEM((nrow, 1), F32), pltpu.VMEM((nrow, DV_B), F32),
                            pltpu.VMEM((2, npp, QB, PAGE), F32), pltpu.VMEM((2, npp, PAGE * H_B, DV_B), F32),
                            pltpu.SemaphoreType.DMA((2,))],
        ),
        out_shape=jax.ShapeDtypeStruct((bsz, SUBLANES, VB), F32),
        compiler_params=_cparams(("arbitrary", "arbitrary")),
        name="diff_attn_sample",
    )(page_table, q8, kn8, vn8, ck_t, cv_i, last_t, self_t, far_col, lam4, sn)


def _post_kernel(prompt_tiles, *refs):
    wa_ref, wb_ref, wo_ref, n2_ref, wr_ref, br_ref, x1_ref, hm_ref, ti_ref, tg_ref = refs[10:]
    is_sample = pl.program_id(0) >= prompt_tiles
    x, oa, ob, sga, sgb = (jnp.where(is_sample, s_ref[...], p_ref[...]) for p_ref, s_ref in zip(refs[:5], refs[5:10]))
    ya = _dot(oa.astype(BF16), wa_ref[...])
    yb = _dot(ob.astype(BF16), wb_ref[...])
    merged = sga.astype(F32) * ya + sgb.astype(F32) * yb
    x1 = x + _dot(merged.astype(BF16), wo_ref[...])
    x1_ref[...] = x1
    hm = x1 * lax.rsqrt(jnp.mean(x1 * x1, axis=-1, keepdims=True) + EPS) * n2_ref[...]
    hm_ref[...] = _pack_halves(hm)
    logits = _dot(hm.astype(BF16), wr_ref[...]) + br_ref[...]
    lane = lax.broadcasted_iota(jnp.int32, logits.shape, 1)
    lane_f = lane.astype(F32)
    work = jnp.where(lane < N_EXPERTS, logits, -jnp.inf)
    ti = jnp.zeros(logits.shape, F32)
    tg = jnp.zeros(logits.shape, F32)
    v0 = None
    den = None
    for k in range(TOP_K):
        vk = jnp.max(work, axis=-1, keepdims=True)
        ik = jnp.min(jnp.where(work == vk, lane_f, float(LANES)), axis=-1, keepdims=True)
        work = jnp.where(lane_f == ik, -jnp.inf, work)
        if k == 0:
            v0 = vk
        ek = jnp.exp(vk - v0)
        den = ek if k == 0 else den + ek
        ti = jnp.where(lane == k, ik, ti)
        tg = jnp.where(lane == k, ek, tg)
    ti_ref[...] = ti.astype(jnp.int32)
    tg_ref[...] = tg / den


def _post(prompt_in, sample_in, dense, n_prompt, n_sample, prompt_map):
    d = prompt_in[0].shape[1]
    tm = POST_TILE
    tp = n_prompt // tm
    n_total = n_prompt + n_sample
    widths = (d, VA, VB, d, d)
    full = lambda a: pl.BlockSpec(a.shape, lambda i: (0,) * a.ndim)
    p_spec = lambda k, c: pl.BlockSpec((tm, c), lambda i: ((jnp.minimum(i, tp - 1) if k == 0
                                                             else prompt_map(jnp.minimum(i, tp - 1))), 0))
    s_spec = lambda c: pl.BlockSpec((tm, c), lambda i: (jnp.maximum(i - tp, 0), 0))
    rout = lambda c: pl.BlockSpec((tm, c), lambda i: (i, 0))
    return pl.pallas_call(
        functools.partial(_post_kernel, tp),
        grid=(n_total // tm,),
        in_specs=[p_spec(k, c) for k, c in enumerate(widths)] + [s_spec(c) for c in widths]
                 + [full(a) for a in dense],
        out_specs=[rout(d), rout(d // 2), rout(LANES), rout(LANES)],
        out_shape=[jax.ShapeDtypeStruct((n_total, d), F32), jax.ShapeDtypeStruct((n_total, d // 2), jnp.uint32),
                   jax.ShapeDtypeStruct((n_total, LANES), jnp.int32), jax.ShapeDtypeStruct((n_total, LANES), F32)],
        compiler_params=_cparams(("parallel",)),
        name="merge_outproj_router",
    )(*prompt_in, *sample_in, *dense)


def _rank_kernel(ti_ref, rank_ref, cnt_ref, base):
    i = pl.program_id(0)
    tm = ti_ref.shape[0]

    @pl.when(i == 0)
    def _():
        base[...] = jnp.zeros(base.shape, F32)

    ti = ti_ref[...].astype(F32)
    lane = lax.broadcasted_iota(jnp.int32, ti.shape, 1)
    lane_f = lane.astype(F32)
    sel = [jnp.sum(jnp.where(lane == k, ti, 0.0), axis=-1, keepdims=True) for k in range(TOP_K)]
    onehot = jnp.zeros(ti.shape, F32)
    for k in range(TOP_K):
        onehot = onehot + (lane_f == sel[k]).astype(F32)
    ri = lax.broadcasted_iota(jnp.int32, (tm, tm), 0)
    ci = lax.broadcasted_iota(jnp.int32, (tm, tm), 1)
    before = _dot((ri > ci).astype(BF16), onehot.astype(BF16)) + base[...]
    rank = jnp.zeros(ti.shape, jnp.int32)
    for k in range(TOP_K):
        rk = jnp.sum(jnp.where(lane_f == sel[k], before, 0.0), axis=-1, keepdims=True)
        rank = jnp.where(lane == k, rk.astype(jnp.int32), rank)
    rank_ref[...] = rank
    base[...] = base[...] + jnp.sum(onehot, axis=0, keepdims=True)
    cnt_ref[...] = base[...]


def _rank(ti):
    n = ti.shape[0]
    tm = ROW_TILE
    return pl.pallas_call(
        _rank_kernel,
        grid=(n // tm,),
        in_specs=[pl.BlockSpec((tm, LANES), lambda i: (i, 0))],
        out_specs=[pl.BlockSpec((tm, LANES), lambda i: (i, 0)), pl.BlockSpec((1, LANES), lambda i: (0, 0))],
        out_shape=[jax.ShapeDtypeStruct((n, LANES), jnp.int32), jax.ShapeDtypeStruct((1, LANES), F32)],
        scratch_shapes=[pltpu.VMEM((1, LANES), F32)],
        compiler_params=_cparams(("arbitrary",)),
        name="moe_rank",
    )(ti)


def _row_scatter(rows, dest_km, pad_slots, n_slots):
    n, d = rows.shape
    workers = SC_CORES * SC_SUBCORES
    per_worker = n // workers
    window = max(w for w in range(SUBLANES, SC_WINDOW + 33, SUBLANES) if per_worker % w == 0)
    chunks = per_worker // window
    pad_per_worker = pad_slots.shape[0] // workers
    pad_window = SC_WINDOW // 2
    assert n % workers == 0 and pad_slots.shape[0] % workers == 0 and pad_per_worker % pad_window == 0
    assert dest_km.shape[0] == TOP_K * n and pad_slots.shape[0] + TOP_K * n == n_slots
    mesh = plsc.VectorSubcoreMesh(core_axis_name="core", subcore_axis_name="subcore",
                                  num_cores=SC_CORES, num_subcores=SC_SUBCORES)

    @pl.kernel(out_type=jax.ShapeDtypeStruct((n_slots, d), rows.dtype), mesh=mesh,
               scratch_types=[pltpu.VMEM((2, TOP_K, window), jnp.int32), pltpu.VMEM((2, window, d), rows.dtype),
                              pltpu.VMEM((pad_window,), jnp.int32), pltpu.VMEM((pad_window, d), rows.dtype),
                              pltpu.SemaphoreType.DMA((2,)), pltpu.SemaphoreType.DMA],
               name="moe_row_scatter")
    def scatter(rows_hbm, dest_hbm, pad_hbm, zero_hbm, out_hbm, idx_v, rows_v, pad_v, zero_v, sem, pad_sem):
        worker = lax.axis_index("subcore") * SC_CORES + lax.axis_index("core")

        def stores(b):
            return [pltpu.make_async_copy(rows_v.at[b], out_hbm.at[idx_v.at[b, k]], sem.at[b])
                    for k in range(TOP_K)]

        for c in range(chunks):
            b = c % 2
            if c >= 2:
                for cp in stores(b):
                    cp.wait()
            first = pl.multiple_of(worker * per_worker + c * window, SUBLANES)
            pltpu.sync_copy(rows_hbm.at[pl.ds(first, window)], rows_v.at[b])
            for k in range(TOP_K):
                pltpu.sync_copy(dest_hbm.at[pl.ds(k * n + first, window)], idx_v.at[b, k])
            for cp in stores(b):
                cp.start()
        for b in range(min(chunks, 2)):
            for cp in stores(b):
                cp.wait()

        pltpu.sync_copy(zero_hbm, zero_v)
        for c in range(pad_per_worker // pad_window):
            first = pl.multiple_of(worker * pad_per_worker + c * pad_window, pad_window)
            pltpu.sync_copy(pad_hbm.at[pl.ds(first, pad_window)], pad_v)
            pltpu.async_copy(zero_v, out_hbm.at[pad_v], pad_sem).wait()

    return scatter(rows, dest_km, pad_slots, jnp.zeros((pad_window, d), rows.dtype))


def _row_gather(table, idx):
    n = idx.shape[0]
    d = table.shape[1]
    workers = SC_CORES * SC_SUBCORES
    per_worker = n // workers
    chunks = per_worker // SC_WINDOW
    assert n % workers == 0 and per_worker % SC_WINDOW == 0 and chunks >= 2
    mesh = plsc.VectorSubcoreMesh(core_axis_name="core", subcore_axis_name="subcore",
                                  num_cores=SC_CORES, num_subcores=SC_SUBCORES)

    @pl.kernel(out_type=jax.ShapeDtypeStruct((n, d), table.dtype), mesh=mesh,
               scratch_types=[pltpu.VMEM((2, SC_WINDOW), jnp.int32), pltpu.VMEM((2, SC_WINDOW, d), table.dtype),
                              pltpu.SemaphoreType.DMA((2,))],
               name="moe_row_gather")
    def gather(table_hbm, idx_hbm, out_hbm, idx_v, rows_v, sem):
        worker = lax.axis_index("subcore") * SC_CORES + lax.axis_index("core")

        def rows_of(c):
            return pl.ds(pl.multiple_of(worker * per_worker + c * SC_WINDOW, SC_WINDOW), SC_WINDOW)

        def fetch(c, b):
            return pltpu.make_async_copy(table_hbm.at[idx_v.at[b]], rows_v.at[b], sem.at[b])

        def start(c, b):
            pltpu.sync_copy(idx_hbm.at[rows_of(c)], idx_v.at[b])
            fetch(c, b).start()

        for b in range(2):
            start(b, b)

        @pl.loop(0, chunks, step=2)
        def _(c0):
            for b in range(2):
                c = c0 + b

                @pl.when(c < chunks)
                def _():
                    fetch(c, b).wait()
                    pltpu.sync_copy(rows_v.at[b], out_hbm.at[rows_of(c)])

                    @pl.when(c + 2 < chunks)
                    def _():
                        start(c + 2, b)

    return gather(table, idx)


def _expert_kernel(be_ref, bv_ref, xs_ref, wgu_ref, bgu_ref, wd_ref, bd_ref, y_ref, wgu_bf, wd_bf):
    i = pl.program_id(0)
    d_ff = wd_ref.shape[1]

    @pl.when((i == 0) | (be_ref[i] != be_ref[jnp.maximum(i - 1, 0)]))
    def _():
        wgu_bf[...] = wgu_ref[0].astype(BF16)
        wd_bf[...] = wd_ref[0].astype(BF16)

    @pl.when(bv_ref[i] > 0)
    def _():
        x = _unpack_halves(xs_ref[...]).astype(BF16)
        hgu = _dot(x, wgu_bf[...]) + bgu_ref[0]
        gate = jnp.minimum(hgu[:, :d_ff], SWIGLU_LIMIT)
        up = jnp.clip(hgu[:, d_ff:], -SWIGLU_LIMIT, SWIGLU_LIMIT)
        act = (up + 1.0) * gate * _sigmoid(SWIGLU_ALPHA * gate)
        y_ref[...] = _pack_halves(_dot(act.astype(BF16), wd_bf[...]) + bd_ref[0])

    @pl.when(bv_ref[i] == 0)
    def _():
        y_ref[...] = jnp.zeros(y_ref.shape, y_ref.dtype)


def _experts(block_expert, block_valid, xs, w_gu, b_gu3, w_down, b_down3):
    n_slots, dp = xs.shape
    n_blocks = n_slots // MOE_BLK
    d_ff, d = w_down.shape[1:]
    assert dp * 2 == d
    return pl.pallas_call(
        _expert_kernel,
        grid_spec=pltpu.PrefetchScalarGridSpec(
            num_scalar_prefetch=2,
            grid=(n_blocks,),
            in_specs=[pl.BlockSpec((MOE_BLK, dp), lambda i, be, bv: (i, 0)),
                      pl.BlockSpec((1, d, 2 * d_ff), lambda i, be, bv: (be[i], 0, 0)),
                      pl.BlockSpec((1, 1, 2 * d_ff), lambda i, be, bv: (be[i], 0, 0)),
                      pl.BlockSpec((1, d_ff, d), lambda i, be, bv: (be[i], 0, 0)),
                      pl.BlockSpec((1, 1, d), lambda i, be, bv: (be[i], 0, 0))],
            out_specs=pl.BlockSpec((MOE_BLK, dp), lambda i, be, bv: (i, 0)),
            scratch_shapes=[pltpu.VMEM((d, 2 * d_ff), BF16), pltpu.VMEM((d_ff, d), BF16)],
        ),
        out_shape=jax.ShapeDtypeStruct((n_slots, dp), jnp.uint32),
        compiler_params=_cparams(("arbitrary",)),
        name="moe_experts",
    )(block_expert, block_valid, xs, w_gu, b_gu3, w_down, b_down3)


def _combine_kernel(yk_ref, tg_ref, x1_ref, y_ref):
    tg = tg_ref[...]
    lane = lax.broadcasted_iota(jnp.int32, tg.shape, 1)
    acc = x1_ref[...]
    for k in range(TOP_K):
        gk = jnp.sum(jnp.where(lane == k, tg, 0.0), axis=-1, keepdims=True)
        acc = acc + gk * _unpack_halves(yk_ref[k])
    y_ref[...] = acc


def _combine(yk, tg, x1):
    n, d = x1.shape
    tm = POST_TILE
    return pl.pallas_call(
        _combine_kernel,
        grid=(n // tm,),
        in_specs=[pl.BlockSpec((TOP_K, tm, yk.shape[2]), lambda i: (0, i, 0)),
                  pl.BlockSpec((tm, LANES), lambda i: (i, 0)),
                  pl.BlockSpec((tm, d), lambda i: (i, 0))],
        out_specs=pl.BlockSpec((tm, d), lambda i: (i, 0)),
        out_shape=jax.ShapeDtypeStruct((n, d), F32),
        compiler_params=_cparams(("parallel",)),
        name="moe_combine",
    )(yk, tg, x1)


def _moe(hm, ti, tg, x1, w_gu, b_gu, w_down, b_down):
    n, d = hm.shape
    rank, cnt = _rank(ti)
    counts = cnt[0, :N_EXPERTS].astype(jnp.int32)
    padded = (counts + MOE_BLK - 1) // MOE_BLK * MOE_BLK
    pad_end = jnp.cumsum(padded)
    pad_start = pad_end - padded
    top_i = ti[:, :TOP_K]
    dest_km = (pad_start[top_i] + rank[:, :TOP_K]).astype(jnp.int32).T.reshape(TOP_K * n)
    n_blocks = (n * TOP_K) // MOE_BLK + N_EXPERTS
    n_slots = n_blocks * MOE_BLK
    blk_start = jnp.arange(n_blocks, dtype=jnp.int32) * MOE_BLK
    block_expert = jnp.minimum(jnp.sum((pad_end[None, :] <= blk_start[:, None]).astype(jnp.int32), axis=1),
                               N_EXPERTS - 1)
    block_valid = (blk_start < pad_end[-1]).astype(jnp.int32)
    n_pad = padded - counts
    pad_cum = jnp.cumsum(n_pad)
    j = jnp.arange(n_slots - TOP_K * n, dtype=jnp.int32)
    owner = jnp.sum((pad_cum[None, :] <= j[:, None]).astype(jnp.int32), axis=1)
    e = jnp.minimum(owner, N_EXPERTS - 1)
    in_expert = pad_start[e] + counts[e] + (j - (pad_cum[e] - n_pad[e]))
    pad_slots = jnp.where(owner < N_EXPERTS, in_expert, pad_end[-1] + (j - pad_cum[-1])).astype(jnp.int32)
    xs = _row_scatter(hm, dest_km, pad_slots, n_slots)
    yb = _experts(block_expert, block_valid, xs, w_gu, b_gu.reshape(N_EXPERTS, 1, -1), w_down,
                  b_down.reshape(N_EXPERTS, 1, -1))
    yk = _row_gather(yb, dest_km).reshape(TOP_K, n, d)
    return _combine(yk, tg, x1)


def _pad_lanes(v, fill=0.0):
    v = v.reshape(1, -1).astype(F32)
    return jnp.pad(v, ((0, 0), (0, LANES - v.shape[1])), constant_values=fill)


def kernel(x_prompt, x_sample, cache_k, cache_v, state_ssm, state_conv, page_table, meta_tokens, rel_bias_table, norm1, w_in, conv_w, a_log, dt_bias, gdn_norm, q_norm, k_norm, lam_q1, lam_k1, lam_q2, lam_k2, sub_norm, w_br_a, w_br_b, w_out, norm2, w_router, b_router, w_gu, b_gu, w_down, b_down):
    bsz, seq, d = x_prompt.shape
    dbs, dseq, _ = x_sample.shape
    depth = w_in.shape[0]
    assert depth == 1 and dseq <= SUBLANES - (CONV_W - 1) and seq % ATT_BLK == 0
    lam_init = 0.8 - 0.6 * math.exp(-0.3 * 0)
    l = 0

    w = w_in[l]
    o_z = CONV_DIM + VA
    w_main = jnp.concatenate([w[:, :o_z], w[:, o_z + 2 * H_A:]], axis=1).astype(BF16)
    w_ba = jnp.pad(w[:, o_z:o_z + 2 * H_A], ((0, 0), (0, LANES - 2 * H_A))).astype(BF16)
    grp = np.arange(QB) // DK_B
    gmat = jnp.asarray((grp[:, None] == grp[None, :]).astype(np.float32) / DK_B, dtype=BF16)
    qn_t = jnp.tile(q_norm[l].astype(F32), QB // DK_B).reshape(1, QB)
    kn_t = jnp.tile(k_norm[l].astype(F32), QB // DK_B).reshape(1, QB)
    alog_p = jnp.pad(a_log[l].astype(F32), (H_A, LANES - 2 * H_A)).reshape(1, LANES)
    dtb_p = jnp.pad(dt_bias[l].astype(F32), (H_A, LANES - 2 * H_A)).reshape(1, LANES)
    n1 = norm1[l].reshape(1, d).astype(F32)
    proj = functools.partial(_inproj, n1=n1, w_main=w_main, w_ba=w_ba, gmat=gmat, qn_t=qn_t, kn_t=kn_t,
                             alog_p=alog_p, dtb_p=dtb_p)
    lam4 = jnp.stack([lam_q1[l], lam_k1[l], lam_q2[l], lam_k2[l]]).astype(F32)
    sn = sub_norm[l].reshape(1, DV_B).astype(F32)
    gn = gdn_norm[l].reshape(1, DV_A).astype(F32)
    cw = conv_w[l].astype(F32)

    ii = np.arange(ATT_BLK)[:, None]
    jj = np.arange(LANES)[None, :]
    bk_prompt = np.stack([_bucket_np(ii - jj), _bucket_np(ATT_BLK + ii - jj)])
    t8 = np.arange(SUBLANES)[:, None]
    bk_sample = np.stack([_bucket_np(PAGE + t8 - jj), _bucket_np(t8 - jj)])
    bt_prompt = _bias_tiles(rel_bias_table.astype(F32), bk_prompt)
    bt_sample = _bias_tiles(rel_bias_table.astype(F32), bk_sample)
    far_h = rel_bias_table[N_BUCKETS - 1].astype(F32)
    far_p = jnp.broadcast_to(far_h[:, None, None], (H_B, 1, ATT_KV))
    diag_t = jnp.where(jnp.asarray(ii >= jj), bt_prompt[:, 0], NEG)
    near_t = bt_prompt[:, 1]
    far_t = jnp.broadcast_to(far_h[:, None, None], near_t.shape)
    neg_t = jnp.full(near_t.shape, NEG, F32)
    tile2 = lambda a, b_, c_, d_: jnp.concatenate([jnp.concatenate([a, b_], axis=2),
                                                   jnp.concatenate([c_, d_], axis=2)], axis=1)
    bt_prompt = jnp.stack([tile2(diag_t, neg_t, near_t, diag_t), tile2(far_t, near_t, far_t, far_t)], axis=1)
    nrow = H_B * 2 * SUBLANES
    rows_of = lambda t: jnp.broadcast_to(t[:, None], (H_B, 2, SUBLANES, LANES)).reshape(nrow, LANES)
    last_t, self_t = rows_of(bt_sample[:, 0]), rows_of(bt_sample[:, 1])
    far_col = jnp.broadcast_to(far_h[:, None, None], (H_B, 2 * SUBLANES, LANES)).reshape(nrow, LANES)

    lreal = N_META + seq
    lp = -(-lreal // ATT_KV) * ATT_KV
    fp = lp - lreal
    assert fp % SUBLANES == 0 and fp >= GDN_CHUNK and (fp + N_META) % ATT_BLK == 0 and lp % GDN_CHUNK == 0
    xp = jnp.concatenate([jnp.zeros((bsz, fp, d), F32),
                          jnp.broadcast_to(meta_tokens.astype(F32)[None], (bsz, N_META, d)), x_prompt], axis=1)
    conv_p, z_p, bg_p, qd_p, kd_p, vd_p, sga_p, sgb_p, kdb_p, vdb_p = proj(xp.reshape(bsz * lp, d), ATT_KV)
    r3 = lambda a, b_, r_: a.reshape(b_, r_, a.shape[-1])
    oa_p, ssm_p = _gdn(r3(conv_p, bsz, lp), r3(bg_p, bsz, lp), r3(z_p, bsz, lp),
                       jnp.zeros((bsz, H_A, DK_A, DV_A), F32), cw, gn, fp, GDN_CHUNK, 1, H_A)
    kbias = jnp.where(jnp.arange(lp) < fp, NEG, 0.0).astype(F32).reshape(lp // ATT_KV, ATT_KV)
    ob_p = _attn_prompt(r3(qd_p, bsz, lp), r3(kdb_p, bsz, lp), r3(vdb_p, bsz, lp), bt_prompt, far_p, kbias,
                        lam4, sn, fp // ATT_Q, lam_init)

    ns = dbs * dseq
    conv_s, z_s, bg_s, qd_s, kd_s, vd_s, sga_s, sgb_s, _, _ = proj(x_sample.reshape(ns, d), min(ns, 256))
    cs = SUBLANES
    rs = 2 * cs
    fs = rs - dseq

    def chunk_rows(a, head=None):
        a = a.reshape(dbs, dseq, a.shape[-1]).astype(F32)
        parts = [jnp.zeros((dbs, fs - (0 if head is None else head.shape[1]), a.shape[-1]), F32)]
        if head is not None:
            parts.append(head.astype(F32))
        return jnp.concatenate(parts + [a], axis=1)

    oa_s, ssm_s = _gdn(chunk_rows(conv_s, state_conv[l]), chunk_rows(bg_s), chunk_rows(z_s),
                       state_ssm[l].astype(F32), cw, gn, fs, cs, math.gcd(dbs, 4), H_A)
    pad8 = lambda a: jnp.pad(a.reshape(dbs, dseq, a.shape[-1]), ((0, 0), (0, SUBLANES - dseq), (0, 0)))
    n_pool = cache_k.shape[1]
    ck_t = jnp.transpose(cache_k[l], (0, 2, 3, 4, 1)).reshape(n_pool, QB, PAGE)
    cv_i = cache_v[l].reshape(n_pool, PAGE * H_B, DV_B)
    ob_s = _attn_sample(page_table, pad8(qd_s).astype(F32), pad8(kd_s), pad8(vd_s), ck_t, cv_i,
                        last_t, self_t, far_col, lam4, sn, dseq, lam_init)

    wa, wb, wo = w_br_a[l].astype(BF16), w_br_b[l].astype(BF16), w_out[l].astype(BF16)
    n2 = norm2[l].reshape(1, d).astype(F32)
    wr = jnp.pad(w_router[l], ((0, 0), (0, LANES - N_EXPERTS))).astype(BF16)
    br = _pad_lanes(b_router[l])
    tiles_b = seq // POST_TILE
    tiles_lp = lp // POST_TILE
    skip = (fp + N_META) // POST_TILE
    npt = bsz * seq
    assert ns % POST_TILE == 0 and seq % POST_TILE == 0 and (fp + N_META) % POST_TILE == 0
    oa_s2 = oa_s[:, fs:].reshape(ns, VA)
    ob_s2 = ob_s[:, :dseq].reshape(ns, VB)
    x1, hm, ti, tg = _post(
        (x_prompt.reshape(npt, d), oa_p.reshape(bsz * lp, VA), ob_p.reshape(bsz * lp, VB), sga_p, sgb_p),
        (x_sample.reshape(ns, d), oa_s2, ob_s2, sga_s, sgb_s), (wa, wb, wo, n2, wr, br), npt, ns,
        lambda i: (i // tiles_b) * tiles_lp + skip + i % tiles_b)

    y = _moe(hm, ti, tg, x1, w_gu[l], b_gu[l], w_down[l], b_down[l])
    y_prompt = y[:npt].reshape(bsz, seq, d)
    y_sample = y[npt:].reshape(dbs, dseq, d)

    k_prompt = r3(kd_p, bsz, lp)[:, fp:].reshape(1, bsz, lreal, H_B, 2, DK_B)
    v_prompt = r3(vd_p, bsz, lp)[:, fp:].reshape(1, bsz, lreal, H_B, DV_B)
    conv_prompt = r3(conv_p, bsz, lp)[:, lp - (CONV_W - 1):][None]
    xpad_s = jnp.concatenate([state_conv[l].astype(F32), conv_s.reshape(dbs, dseq, CONV_DIM)], axis=1)
    conv_sample = xpad_s[:, dseq:][None]
    return (y_prompt, y_sample, k_prompt, v_prompt, ssm_p[None], conv_prompt,
            kd_s.reshape(1, dbs, dseq, H_B, 2, DK_B), vd_s.reshape(1, dbs, dseq, H_B, DV_B), ssm_s[None], conv_sample)
```

```python
import functools
import math

import numpy as np
import jax
import jax.numpy as jnp
from jax import lax
from jax.experimental import pallas as pl
from jax.experimental.pallas import tpu as pltpu
from jax.experimental.pallas import tpu_sc as plsc

F32 = jnp.float32
BF16 = jnp.bfloat16
HI = lax.Precision.HIGHEST

N_META = 16
H_A, DK_A, DV_A = 4, 128, 128
CONV_W = 4
H_B, DK_B = 4, 64
DV_B = 2 * DK_B
N_BUCKETS, MAX_DIST = 32, 128
N_EXPERTS, TOP_K = 32, 4
SWIGLU_LIMIT, SWIGLU_ALPHA = 7.0, 1.702
EPS = 1e-6
PAGE = 128
QA = H_A * DK_A
VA = H_A * DV_A
QB = H_B * 2 * DK_B
VB = H_B * DV_B
CONV_DIM = 2 * QA + VA

LANES = 128
SUBLANES = 8
VMEM_LIMIT = 56 * 1024 * 1024
SC_CORES, SC_SUBCORES, SC_WINDOW = 2, 16, 64

GDN_CHUNK = 64
ATT_BLK = 128
ATT_KV = 2 * ATT_BLK
ATT_Q = 2 * ATT_BLK
PAGES_PER_STEP = 16
MOE_BLK = 256
ROW_TILE = 256
POST_TILE = 256
NEG = -1e30


def _dot(a, b, prec=None):
    return jnp.dot(a, b, preferred_element_type=F32, precision=prec)


def _dot_nt(a, b, prec=None):
    return lax.dot_general(a, b, (((1,), (1,)), ((), ())), preferred_element_type=F32, precision=prec)


def _dot_tn(a, b, prec=None):
    return lax.dot_general(a, b, (((0,), (0,)), ((), ())), preferred_element_type=F32, precision=prec)


def _split(a):
    hi = a.astype(BF16)
    return hi, (a - hi.astype(F32)).astype(BF16)


def _lhs3(a):
    hi, lo = _split(a)
    return jnp.concatenate([hi, lo, hi], axis=1)


def _rhs3(b):
    hi, lo = _split(b)
    return jnp.concatenate([hi, hi, lo], axis=0)


def _pack_halves(x):
    w = x.shape[1] // 2
    bits = lambda v: pltpu.bitcast(v.astype(BF16).astype(F32), jnp.uint32)
    return (bits(x[:, :w]) >> 16) | (bits(x[:, w:]) & jnp.uint32(0xFFFF0000))


def _unpack_halves(u):
    lo = pltpu.bitcast(u << 16, F32)
    hi = pltpu.bitcast(u & jnp.uint32(0xFFFF0000), F32)
    return jnp.concatenate([lo, hi], axis=1)


def _sigmoid(x):
    return 1.0 / (1.0 + jnp.exp(-x))


def _cparams(sem, flags=None):
    return pltpu.CompilerParams(dimension_semantics=sem, vmem_limit_bytes=VMEM_LIMIT, flags=flags)


def _bucket_np(d):
    d = np.maximum(d, 0)
    df = np.maximum(d, 1).astype(np.float32)
    max_exact = N_BUCKETS // 2
    large = max_exact + (np.log(df / np.float32(max_exact)) / np.float32(math.log(MAX_DIST / max_exact))
                         * np.float32(N_BUCKETS - max_exact)).astype(np.int32)
    return np.where(d < max_exact, d, np.minimum(large, N_BUCKETS - 1)).astype(np.int32)


def _bias_kernel(tab_ref, bk_ref, o_ref):
    for t in range(bk_ref.shape[0]):
        bk = bk_ref[t]
        for h in range(H_B):
            acc = jnp.zeros(bk.shape, F32)
            for b in range(N_BUCKETS):
                acc = jnp.where(bk == b, tab_ref[b, h], acc)
            o_ref[h, t] = acc


def _bias_tiles(table, buckets):
    t, r, _ = buckets.shape
    return pl.pallas_call(
        _bias_kernel,
        out_shape=jax.ShapeDtypeStruct((H_B, t, r, LANES), F32),
        in_specs=[pl.BlockSpec(memory_space=pltpu.SMEM), pl.BlockSpec(memory_space=pltpu.VMEM)],
        out_specs=pl.BlockSpec(memory_space=pltpu.VMEM),
        name="rel_bias_tiles",
    )(table, jnp.asarray(buckets))


def _inproj_kernel(x_ref, n1_ref, w_ref, wba_ref, gm_ref, qn_ref, kn_ref, alog_ref, dtb_ref,
                   conv_ref, z_ref, bg_ref, qd_ref, kd_ref, vd_ref, sga_ref, sgb_ref, kdb_ref, vdb_ref):
    x = x_ref[...]
    ms = jnp.mean(x * x, axis=-1, keepdims=True)
    h = (x * lax.rsqrt(ms + EPS) * n1_ref[...]).astype(BF16)
    o = 0
    conv_ref[...] = _dot(h, w_ref[:, o:o + CONV_DIM]); o += CONV_DIM
    z_ref[...] = _dot(h, w_ref[:, o:o + VA]).astype(z_ref.dtype); o += VA

    def group_norm(y, g):
        sq = y * y
        hi = sq.astype(BF16)
        lo = (sq - hi.astype(F32)).astype(BF16)
        msq = _dot(hi, gm_ref[...]) + _dot(lo, gm_ref[...])
        return y * lax.rsqrt(msq + EPS) * g

    qb = _dot(h, w_ref[:, o:o + QB]); o += QB
    qd_ref[...] = (group_norm(qb, qn_ref[...]) * (DK_B ** -0.5)).astype(qd_ref.dtype)
    kb = _dot(h, w_ref[:, o:o + QB]); o += QB
    kd = group_norm(kb, kn_ref[...])
    kd_ref[...] = kd
    kdb_ref[...] = kd.astype(kdb_ref.dtype)
    vd = _dot(h, w_ref[:, o:o + VB]); o += VB
    vd_ref[...] = vd
    vdb_ref[...] = vd.astype(vdb_ref.dtype)
    d_model = x.shape[1]
    sga_ref[...] = _sigmoid(_dot(h, w_ref[:, o:o + d_model])).astype(sga_ref.dtype); o += d_model
    sgb_ref[...] = _sigmoid(_dot(h, w_ref[:, o:o + d_model])).astype(sgb_ref.dtype)
    t = _dot(h, wba_ref[...])
    lane = lax.broadcasted_iota(jnp.int32, t.shape, 1)
    ta = t + dtb_ref[...]
    sp = jnp.maximum(ta, 0.0) + jnp.log(1.0 + jnp.exp(-jnp.abs(ta)))
    bg_ref[...] = jnp.where(lane < H_A, _sigmoid(t), -jnp.exp(alog_ref[...]) * sp)


def _inproj(x2d, tm, n1, w_main, w_ba, gmat, qn_t, kn_t, alog_p, dtb_p):
    n, d = x2d.shape
    assert n % tm == 0
    row = lambda c: pl.BlockSpec((tm, c), lambda i: (i, 0))
    full = lambda a: pl.BlockSpec(a.shape, lambda i: (0,) * a.ndim)
    outs = [(CONV_DIM, F32), (VA, BF16), (LANES, F32), (QB, BF16), (QB, F32), (VB, F32), (d, BF16), (d, BF16),
            (QB, BF16), (VB, BF16)]
    return pl.pallas_call(
        _inproj_kernel,
        grid=(n // tm,),
        in_specs=[row(d)] + [full(a) for a in (n1, w_main, w_ba, gmat, qn_t, kn_t, alog_p, dtb_p)],
        out_specs=[row(c) for c, _ in outs],
        out_shape=[jax.ShapeDtypeStruct((n, c), dt) for c, dt in outs],
        compiler_params=_cparams(("parallel",)),
        name="in_proj",
    )(x2d, n1, w_main, w_ba, gmat, qn_t, kn_t, alog_p, dtb_p)


def _gdn_kernel(fv, c, q_ref, k_ref, v_ref, cwq_ref, cwk_ref, cwv_ref, bg_ref, z_ref, s0_ref, gn_ref,
                o_ref, st_ref):
    nb, r, wd = q_ref.shape
    hg = wd // LANES
    head0 = pl.program_id(1) * hg
    chains = [(b, hh) for b in range(nb) for hh in range(hg)]

    ri = lax.broadcasted_iota(jnp.int32, (c, c), 0)
    ci = lax.broadcasted_iota(jnp.int32, (c, c), 1)
    incl = ri >= ci
    strict = ri > ci
    eye = (ri == ci).astype(F32)
    lane = lax.broadcasted_iota(jnp.int32, (c, LANES), 1)
    sub_t = lax.broadcasted_iota(jnp.int32, (LANES, c), 0)
    rowid = lax.broadcasted_iota(jnp.int32, (c, 1), 0)
    colid = lax.broadcasted_iota(jnp.int32, (1, c), 1)

    def conv(x_ref, cw_ref, b, cols, r0):
        w = x_ref[b, pl.ds(r0 - SUBLANES, c + SUBLANES), cols]
        acc = w[SUBLANES - 3:SUBLANES - 3 + c] * cw_ref[0:1, cols]
        for j in range(1, CONV_W):
            acc = acc + w[SUBLANES - 3 + j:SUBLANES - 3 + j + c] * cw_ref[j:j + 1, cols]
        return acc * _sigmoid(acc)

    def elementwise(b, hh, r0):
        cols = slice(hh * LANES, (hh + 1) * LANES)
        head = head0 + hh
        vcol = ((r0 + rowid) >= fv).astype(F32)
        vrow = ((r0 + colid) >= fv).astype(F32)
        qv = conv(q_ref, cwq_ref, b, cols, r0)
        kv = conv(k_ref, cwk_ref, b, cols, r0)
        v = conv(v_ref, cwv_ref, b, cols, r0) * vcol
        q = qv * lax.rsqrt(jnp.sum(qv * qv, axis=-1, keepdims=True) + EPS) * (DK_A ** -0.5) * vcol
        k = kv * lax.rsqrt(jnp.sum(kv * kv, axis=-1, keepdims=True) + EPS) * vcol
        bgc = bg_ref[b, pl.ds(r0, c), :]
        beta = jnp.sum(jnp.where(lane == head, bgc, 0.0), axis=-1, keepdims=True) * vcol
        g_col = jnp.sum(jnp.where(lane == H_A + head, bgc, 0.0), axis=-1, keepdims=True) * vcol
        g_row = jnp.sum(jnp.where(sub_t == H_A + head, bgc.T, 0.0), axis=0, keepdims=True) * vrow
        gc = jnp.sum(jnp.where(incl, g_row, 0.0), axis=-1, keepdims=True)
        gr = jnp.sum(jnp.where(ri <= ci, g_col, 0.0), axis=0, keepdims=True)
        decay = jnp.where(incl, jnp.exp(jnp.where(incl, gc - gr, 0.0)), 0.0)
        kb = k * beta
        eg = jnp.exp(gc)
        g_last = jnp.sum(jnp.where(rowid == c - 1, gc, 0.0), axis=0, keepdims=True)
        k_dt = (k * jnp.exp(g_last - gc)).T
        lhs = jnp.concatenate([kb, q], axis=0).astype(BF16)
        rhs = jnp.concatenate([v * beta, kb * eg], axis=-1)
        return lhs, k.astype(BF16), decay, rhs, q * eg, k_dt, jnp.exp(g_last)

    def prepare_a(j):
        r0 = pl.multiple_of(j * c, c)
        vec = [elementwise(b, hh, r0) for b, hh in chains]
        kk = [_dot_nt(v_[0], v_[1]) for v_ in vec]
        return vec, kk

    def prepare_b(vec, kk):
        low = [jnp.where(strict, kk_[:c] * v_[2], 0.0) for v_, kk_ in zip(vec, kk)]
        intra = [kk_[c:] * v_[2] for v_, kk_ in zip(vec, kk)]
        inv = [eye - lw for lw in low]
        levels = int(math.log2(c)) - 1
        pw = [_dot(_lhs3(lw), _rhs3(lw)) for lw in low]
        for lev in range(levels):
            pw_r = [_rhs3(p) for p in pw]
            if lev + 1 < levels:
                pw = [_dot(_lhs3(p), r_) for p, r_ in zip(pw, pw_r)]
            inv = [iv + _dot(_lhs3(iv), r_) for iv, r_ in zip(inv, pw_r)]
        sol = [_dot(_lhs3(iv), _rhs3(v_[3])) for iv, v_ in zip(inv, vec)]
        out = []
        for v_, sl, it in zip(vec, sol, intra):
            on_state = jnp.concatenate([sl[:, DV_A:], v_[4]], axis=0).astype(BF16)
            on_u = jnp.concatenate([it, v_[5]], axis=0).astype(BF16)
            out.append((sl[:, :DV_A], on_state, on_u, v_[6]))
        return tuple(out)

    def apply_a(prepared, states):
        return [_dot(p[1], s.astype(BF16)) for p, s in zip(prepared, states)]

    def apply_b(j, prepared, states, ps):
        r0 = pl.multiple_of(j * c, c)
        u = [p[0] - ps_[:c] for p, ps_ in zip(prepared, ps)]
        pu = [_dot(p[2], u_.astype(BF16)) for p, u_ in zip(prepared, u)]
        new_states = []
        for (b, hh), p, s, ps_, pu_ in zip(chains, prepared, states, ps, pu):
            cols = slice(hh * LANES, (hh + 1) * LANES)
            o = ps_[c:] + pu_[:c]
            on = o * lax.rsqrt(jnp.mean(o * o, axis=-1, keepdims=True) + EPS) * gn_ref[...]
            zc = z_ref[b, pl.ds(r0, c), cols].astype(F32)
            o_ref[b, pl.ds(r0, c), cols] = (on * (zc * _sigmoid(zc))).astype(o_ref.dtype)
            new_states.append(s * p[3] + pu_[c:])
        return tuple(new_states)

    def chunk(j, carry):
        prepared, states = carry
        ps = apply_a(prepared, states)
        vec, kk = prepare_a(j + 1)
        states = apply_b(j, prepared, states, ps)
        return prepare_b(vec, kk), states

    j0 = fv // c
    last = r // c - 1
    o_ref[:, 0:j0 * c, :] = jnp.zeros((nb, j0 * c, wd), o_ref.dtype)
    init = (prepare_b(*prepare_a(j0)), tuple(s0_ref[b, hh] for b, hh in chains))
    prepared, states = lax.fori_loop(j0, last, chunk, init)
    for (b, hh), s in zip(chains, apply_b(last, prepared, states, apply_a(prepared, states))):
        st_ref[b, hh] = s


def _gdn(conv3, bg3, z3, s0, conv_w, gn, fv, c, nb, hg):
    bsz, r, _ = conv3.shape
    assert r % c == 0 and fv // c >= 1 and c >= SUBLANES and bsz % nb == 0 and H_A % hg == 0
    ng = H_A // hg
    wd = hg * LANES
    blk = lambda off: pl.BlockSpec((nb, r, wd), lambda i, g: (i, 0, off * ng + g))
    cw = lambda off: pl.BlockSpec((CONV_W, wd), lambda i, g: (0, off * ng + g))
    st = pl.BlockSpec((nb, hg, DK_A, DV_A), lambda i, g: (i, g, 0, 0))
    return pl.pallas_call(
        functools.partial(_gdn_kernel, fv, c),
        grid=(bsz // nb, ng),
        in_specs=[blk(0), blk(1), blk(2), cw(0), cw(1), cw(2),
                  pl.BlockSpec((nb, r, LANES), lambda i, g: (i, 0, 0)), blk(0), st,
                  pl.BlockSpec((1, LANES), lambda i, g: (0, 0))],
        out_specs=[blk(0), st],
        out_shape=[jax.ShapeDtypeStruct((bsz, r, VA), F32), jax.ShapeDtypeStruct((bsz, H_A, DK_A, DV_A), F32)],
        compiler_params=_cparams(("parallel", "parallel")),
        name="gdn",
    )(conv3, conv3, conv3, conv_w, conv_w, conv_w, bg3, z3, s0, gn)


def _lambda(lam_ref, lam_init):
    l1 = jnp.sum(lam_ref[0:1, :] * lam_ref[1:2, :], axis=-1, keepdims=True)
    l2 = jnp.sum(lam_ref[2:3, :] * lam_ref[3:4, :], axis=-1, keepdims=True)
    return jnp.exp(l1) - jnp.exp(l2) + lam_init


def _sub_norm(o0, o1, lam, sn, lam_init):
    ob = o0 - lam * o1
    return ob * lax.rsqrt(jnp.mean(ob * ob, axis=-1, keepdims=True) + EPS) * sn * (1.0 - lam_init)


def _attn_prompt_kernel(first_q, lam_init, q_ref, k_ref, v_ref, bt_ref, far_ref, kb_ref, lam_ref, sn_ref, o_ref):
    qi = pl.program_id(2)

    @pl.when(qi < first_q)
    def _():
        o_ref[0] = jnp.zeros(o_ref.shape[1:], o_ref.dtype)

    @pl.when(qi >= first_q)
    def _():
        q = q_ref[0]
        lane = lax.broadcasted_iota(jnp.int32, q.shape, 1)
        zero = jnp.zeros_like(q)
        qs = jnp.concatenate([jnp.where(lane < DK_B, q, zero), jnp.where(lane >= DK_B, q, zero)], axis=0)

        def scores(j):
            k0 = pl.multiple_of(j * ATT_KV, ATT_KV)
            return _dot_nt(qs, k_ref[0, pl.ds(k0, ATT_KV), :])

        def step(j, carry, bias, last=False):
            m, l, acc, s_raw = carry
            s_next = s_raw if last else scores(j + 1)
            k0 = pl.multiple_of(j * ATT_KV, ATT_KV)
            s = s_raw + bias
            m_new = jnp.maximum(m, jnp.max(s, axis=-1, keepdims=True))
            alpha = jnp.exp(m - m_new)
            p = jnp.exp(s - m_new)
            l = alpha * l + jnp.sum(p, axis=-1, keepdims=True)
            acc = alpha * acc + _dot(p.astype(BF16), v_ref[0, pl.ds(k0, ATT_KV), :])
            return m_new, l, acc, s_next

        def tile_bias(j, which):
            t = bt_ref[0, which] + kb_ref[pl.ds(j, 1), :]
            return jnp.concatenate([t, t], axis=0)

        carry = (jnp.full((2 * ATT_Q, 1), NEG, F32), jnp.zeros((2 * ATT_Q, 1), F32),
                 jnp.zeros((2 * ATT_Q, DV_B), F32), scores(0))
        carry = lax.fori_loop(0, qi - 1, lambda j, c: step(j, c, far_ref[0] + kb_ref[pl.ds(j, 1), :]), carry)
        carry = lax.cond(qi >= 1, lambda c: step(qi - 1, c, tile_bias(qi - 1, 1)), lambda c: c, carry)
        _, l, acc, _ = step(qi, carry, tile_bias(qi, 0), last=True)
        o = acc / l
        o_ref[0] = _sub_norm(o[:ATT_Q], o[ATT_Q:], _lambda(lam_ref, lam_init), sn_ref[...],
                             lam_init).astype(o_ref.dtype)


def _attn_prompt(qd3, kd3, vd3, btiles, far, kbias, lam4, sn, first_q, lam_init):
    bsz, lp, _ = qd3.shape
    assert lp % ATT_KV == 0 and ATT_Q == ATT_KV
    kv = pl.BlockSpec((1, lp, LANES), lambda b, h, i: (b, 0, h))
    qo = pl.BlockSpec((1, ATT_Q, LANES), lambda b, h, i: (b, i, h))
    full = lambda a: pl.BlockSpec(a.shape, lambda b, h, i: (0,) * a.ndim)
    return pl.pallas_call(
        functools.partial(_attn_prompt_kernel, first_q, lam_init),
        grid=(bsz, H_B, lp // ATT_Q),
        in_specs=[qo, kv, kv,
                  pl.BlockSpec((1, 2, ATT_Q, ATT_KV), lambda b, h, i: (h, 0, 0, 0)),
                  pl.BlockSpec((1, 1, ATT_KV), lambda b, h, i: (h, 0, 0)),
                  full(kbias), full(lam4), full(sn)],
        out_specs=qo,
        out_shape=jax.ShapeDtypeStruct((bsz, lp, VB), F32),
        compiler_params=_cparams(("parallel", "parallel", "parallel")),
        name="diff_attn_prompt",
    )(qd3, kd3, vd3, btiles, far, kbias, lam4, sn)


def _attn_sample_kernel(npp, t_new, lam_init, pt_ref, q_ref, kn_ref, vn_ref, ck_ref, cv_ref, last_ref, self_ref,
                        far_ref, lam_ref, sn_ref, o_ref, qm, kself, vself, m_s, l_s, acc_s, kbuf, vbuf, sem):
    b = pl.program_id(0)
    st = pl.program_id(1)
    nb = pl.num_programs(0)
    nst = pl.num_programs(1)
    nrow = H_B * 2 * SUBLANES
    hrows = 2 * SUBLANES
    t = b * nst + st
    slot = t % 2

    def page_copies(bb, ss, sl):
        out = []
        for g in range(npp):
            page = pt_ref[bb, ss * npp + g]
            out.append(pltpu.make_async_copy(ck_ref.at[page], kbuf.at[sl, g], sem.at[sl]))
            out.append(pltpu.make_async_copy(cv_ref.at[page], vbuf.at[sl, g], sem.at[sl]))
        return out

    @pl.when(t == 0)
    def _():
        for cp in page_copies(0, 0, 0):
            cp.start()

    @pl.when(t + 1 < nb * nst)
    def _():
        wrap = st + 1 == nst
        for cp in page_copies(jnp.where(wrap, b + 1, b), jnp.where(wrap, 0, st + 1), 1 - slot):
            cp.start()

    for cp in page_copies(b, st, slot):
        cp.wait()

    def update(s, values_of_head):
        m_old = m_s[...]
        m_new = jnp.maximum(m_old, jnp.max(s, axis=-1, keepdims=True))
        alpha = jnp.exp(m_old - m_new)
        pr = jnp.exp(s - m_new)
        l_s[...] = alpha * l_s[...] + jnp.sum(pr, axis=-1, keepdims=True)
        prb = pr.astype(BF16)
        pv = [_dot(prb[h * hrows:(h + 1) * hrows], values_of_head(h)) for h in range(H_B)]
        acc_s[...] = alpha * acc_s[...] + jnp.concatenate(pv, axis=0)
        m_s[...] = m_new

    @pl.when(st == 0)
    def _():
        q8 = q_ref[0]
        lane = lax.broadcasted_iota(jnp.int32, q8.shape, 1)
        qm[...] = jnp.concatenate([jnp.where(lane // DK_B == g, q8, 0.0) for g in range(2 * H_B)],
                                  axis=0).astype(qm.dtype)
        kself[...] = jnp.zeros(kself.shape, F32)
        vself[...] = jnp.zeros(vself.shape, F32)
        kself[0:SUBLANES, :] = kn_ref[0]
        vself[0:SUBLANES, :] = vn_ref[0]
        m_s[...] = jnp.full(m_s.shape, NEG, F32)
        l_s[...] = jnp.zeros(l_s.shape, F32)
        acc_s[...] = jnp.zeros(acc_s.shape, F32)
        rows = lax.broadcasted_iota(jnp.int32, (nrow, PAGE), 0) % SUBLANES
        cols = lax.broadcasted_iota(jnp.int32, (nrow, PAGE), 1)
        ok = (cols <= rows) & (cols < t_new)
        s = jnp.where(ok, _dot_nt(qm[...], kself[...].astype(BF16)) + self_ref[...], NEG)
        update(s, lambda h: vself[:, h * DV_B:(h + 1) * DV_B].astype(BF16))

    far = far_ref[:, 0:1]
    parts = []
    for g in range(npp):
        bias = far if g < npp - 1 else jnp.where(st == nst - 1, last_ref[...], far)
        parts.append(_dot(qm[...], kbuf[slot, g].astype(BF16)) + bias)

    def page_values(h):
        return jnp.concatenate([vbuf[slot, g, pl.ds(h, PAGE, stride=H_B), :] for g in range(npp)],
                               axis=0).astype(BF16)

    update(jnp.concatenate(parts, axis=1), page_values)

    @pl.when(st == nst - 1)
    def _():
        o = acc_s[...] / l_s[...]
        lam = _lambda(lam_ref, lam_init)
        for h in range(H_B):
            r0 = h * hrows
            o_ref[0, :, h * DV_B:(h + 1) * DV_B] = _sub_norm(o[r0:r0 + SUBLANES], o[r0 + SUBLANES:r0 + hrows], lam,
                                                             sn_ref[...], lam_init)


def _attn_sample(page_table, q8, kn8, vn8, ck_t, cv_i, last_t, self_t, far_col, lam4, sn, t_new, lam_init):
    bsz, npg = page_table.shape
    npp = math.gcd(PAGES_PER_STEP, npg)
    nrow = H_B * 2 * SUBLANES
    tok = pl.BlockSpec((1, SUBLANES, QB), lambda b, s, pt: (b, 0, 0))
    hbm = pl.BlockSpec(memory_space=pl.ANY)
    full = lambda a: pl.BlockSpec(a.shape, lambda b, s, pt: (0,) * a.ndim)
    return pl.pallas_call(
        functools.partial(_attn_sample_kernel, npp, t_new, lam_init),
        grid_spec=pltpu.PrefetchScalarGridSpec(
            num_scalar_prefetch=1,
            grid=(bsz, npg // npp),
            in_specs=[tok, tok, tok, hbm, hbm, full(last_t), full(self_t), full(far_col), full(lam4), full(sn)],
            out_specs=tok,
            scratch_shapes=[pltpu.VMEM((nrow, QB), BF16), pltpu.VMEM((PAGE, QB), F32), pltpu.VMEM((PAGE, VB), F32),
                            pltpu.VMEM((nrow, 1), F32), pltpu.VMEM((nrow, 1), F32), pltpu.VMEM((nrow, DV_B), F32),
                            pltpu.VMEM((2, npp, QB, PAGE), F32), pltpu.VMEM((2, npp, PAGE * H_B, DV_B), F32),
                            pltpu.SemaphoreType.DMA((2,))],
        ),
        out_shape=jax.ShapeDtypeStruct((bsz, SUBLANES, VB), F32),
        compiler_params=_cparams(("arbitrary", "arbitrary")),
        name="diff_attn_sample",
    )(page_table, q8, kn8, vn8, ck_t, cv_i, last_t, self_t, far_col, lam4, sn)


def _post_kernel(prompt_tiles, *refs):
    wa_ref, wb_ref, wo_ref, n2_ref, wr_ref, br_ref, x1_ref, hm_ref, ti_ref, tg_ref = refs[10:]
    is_sample = pl.program_id(0) >= prompt_tiles
    x, oa, ob, sga, sgb = (jnp.where(is_sample, s_ref[...], p_ref[...]) for p_ref, s_ref in zip(refs[:5], refs[5:10]))
    ya = _dot(oa.astype(BF16), wa_ref[...])
    yb = _dot(ob.astype(BF16), wb_ref[...])
    merged = sga.astype(F32) * ya + sgb.astype(F32) * yb
    x1 = x + _dot(merged.astype(BF16), wo_ref[...])
    x1_ref[...] = x1
    hm = x1 * lax.rsqrt(jnp.mean(x1 * x1, axis=-1, keepdims=True) + EPS) * n2_ref[...]
    hm_ref[...] = _pack_halves(hm)
    logits = _dot(hm.astype(BF16), wr_ref[...]) + br_ref[...]
    lane = lax.broadcasted_iota(jnp.int32, logits.shape, 1)
    lane_f = lane.astype(F32)
    work = jnp.where(lane < N_EXPERTS, logits, -jnp.inf)
    ti = jnp.zeros(logits.shape, F32)
    tg = jnp.zeros(logits.shape, F32)
    v0 = None
    den = None
    for k in range(TOP_K):
        vk = jnp.max(work, axis=-1, keepdims=True)
        ik = jnp.min(jnp.where(work == vk, lane_f, float(LANES)), axis=-1, keepdims=True)
        work = jnp.where(lane_f == ik, -jnp.inf, work)
        if k == 0:
            v0 = vk
        ek = jnp.exp(vk - v0)
        den = ek if k == 0 else den + ek
        ti = jnp.where(lane == k, ik, ti)
        tg = jnp.where(lane == k, ek, tg)
    ti_ref[...] = ti.astype(jnp.int32)
    tg_ref[...] = tg / den


def _post(prompt_in, sample_in, dense, n_prompt, n_sample, prompt_map):
    d = prompt_in[0].shape[1]
    tm = POST_TILE
    tp = n_prompt // tm
    n_total = n_prompt + n_sample
    widths = (d, VA, VB, d, d)
    full = lambda a: pl.BlockSpec(a.shape, lambda i: (0,) * a.ndim)
    p_spec = lambda k, c: pl.BlockSpec((tm, c), lambda i: ((jnp.minimum(i, tp - 1) if k == 0
                                                             else prompt_map(jnp.minimum(i, tp - 1))), 0))
    s_spec = lambda c: pl.BlockSpec((tm, c), lambda i: (jnp.maximum(i - tp, 0), 0))
    rout = lambda c: pl.BlockSpec((tm, c), lambda i: (i, 0))
    return pl.pallas_call(
        functools.partial(_post_kernel, tp),
        grid=(n_total // tm,),
        in_specs=[p_spec(k, c) for k, c in enumerate(widths)] + [s_spec(c) for c in widths]
                 + [full(a) for a in dense],
        out_specs=[rout(d), rout(d // 2), rout(LANES), rout(LANES)],
        out_shape=[jax.ShapeDtypeStruct((n_total, d), F32), jax.ShapeDtypeStruct((n_total, d // 2), jnp.uint32),
                   jax.ShapeDtypeStruct((n_total, LANES), jnp.int32), jax.ShapeDtypeStruct((n_total, LANES), F32)],
        compiler_params=_cparams(("parallel",)),
        name="merge_outproj_router",
    )(*prompt_in, *sample_in, *dense)


def _rank_kernel(ti_ref, rank_ref, cnt_ref, base):
    i = pl.program_id(0)
    tm = ti_ref.shape[0]

    @pl.when(i == 0)
    def _():
        base[...] = jnp.zeros(base.shape, F32)

    ti = ti_ref[...].astype(F32)
    lane = lax.broadcasted_iota(jnp.int32, ti.shape, 1)
    lane_f = lane.astype(F32)
    sel = [jnp.sum(jnp.where(lane == k, ti, 0.0), axis=-1, keepdims=True) for k in range(TOP_K)]
    onehot = jnp.zeros(ti.shape, F32)
    for k in range(TOP_K):
        onehot = onehot + (lane_f == sel[k]).astype(F32)
    ri = lax.broadcasted_iota(jnp.int32, (tm, tm), 0)
    ci = lax.broadcasted_iota(jnp.int32, (tm, tm), 1)
    before = _dot((ri > ci).astype(BF16), onehot.astype(BF16)) + base[...]
    rank = jnp.zeros(ti.shape, jnp.int32)
    for k in range(TOP_K):
        rk = jnp.sum(jnp.where(lane_f == sel[k], before, 0.0), axis=-1, keepdims=True)
        rank = jnp.where(lane == k, rk.astype(jnp.int32), rank)
    rank_ref[...] = rank
    base[...] = base[...] + jnp.sum(onehot, axis=0, keepdims=True)
    cnt_ref[...] = base[...]


def _rank(ti):
    n = ti.shape[0]
    tm = ROW_TILE
    return pl.pallas_call(
        _rank_kernel,
        grid=(n // tm,),
        in_specs=[pl.BlockSpec((tm, LANES), lambda i: (i, 0))],
        out_specs=[pl.BlockSpec((tm, LANES), lambda i: (i, 0)), pl.BlockSpec((1, LANES), lambda i: (0, 0))],
        out_shape=[jax.ShapeDtypeStruct((n, LANES), jnp.int32), jax.ShapeDtypeStruct((1, LANES), F32)],
        scratch_shapes=[pltpu.VMEM((1, LANES), F32)],
        compiler_params=_cparams(("arbitrary",)),
        name="moe_rank",
    )(ti)


def _row_scatter(rows, dest_km, pad_slots, n_slots):
    n, d = rows.shape
    workers = SC_CORES * SC_SUBCORES
    per_worker = n // workers
    window = max(w for w in range(SUBLANES, SC_WINDOW + 33, SUBLANES) if per_worker % w == 0)
    chunks = per_worker // window
    pad_per_worker = pad_slots.shape[0] // workers
    pad_window = SC_WINDOW // 2
    assert n % workers == 0 and pad_slots.shape[0] % workers == 0 and pad_per_worker % pad_window == 0
    assert dest_km.shape[0] == TOP_K * n and pad_slots.shape[0] + TOP_K * n == n_slots
    mesh = plsc.VectorSubcoreMesh(core_axis_name="core", subcore_axis_name="subcore",
                                  num_cores=SC_CORES, num_subcores=SC_SUBCORES)

    @pl.kernel(out_type=jax.ShapeDtypeStruct((n_slots, d), rows.dtype), mesh=mesh,
               scratch_types=[pltpu.VMEM((2, TOP_K, window), jnp.int32), pltpu.VMEM((2, window, d), rows.dtype),
                              pltpu.VMEM((pad_window,), jnp.int32), pltpu.VMEM((pad_window, d), rows.dtype),
                              pltpu.SemaphoreType.DMA((2,)), pltpu.SemaphoreType.DMA],
               name="moe_row_scatter")
    def scatter(rows_hbm, dest_hbm, pad_hbm, zero_hbm, out_hbm, idx_v, rows_v, pad_v, zero_v, sem, pad_sem):
        worker = lax.axis_index("subcore") * SC_CORES + lax.axis_index("core")

        def stores(b):
            return [pltpu.make_async_copy(rows_v.at[b], out_hbm.at[idx_v.at[b, k]], sem.at[b])
                    for k in range(TOP_K)]

        for c in range(chunks):
            b = c % 2
            if c >= 2:
                for cp in stores(b):
                    cp.wait()
            first = pl.multiple_of(worker * per_worker + c * window, SUBLANES)
            pltpu.sync_copy(rows_hbm.at[pl.ds(first, window)], rows_v.at[b])
            for k in range(TOP_K):
                pltpu.sync_copy(dest_hbm.at[pl.ds(k * n + first, window)], idx_v.at[b, k])
            for cp in stores(b):
                cp.start()
        for b in range(min(chunks, 2)):
            for cp in stores(b):
                cp.wait()

        pltpu.sync_copy(zero_hbm, zero_v)
        for c in range(pad_per_worker // pad_window):
            first = pl.multiple_of(worker * pad_per_worker + c * pad_window, pad_window)
            pltpu.sync_copy(pad_hbm.at[pl.ds(first, pad_window)], pad_v)
            pltpu.async_copy(zero_v, out_hbm.at[pad_v], pad_sem).wait()

    return scatter(rows, dest_km, pad_slots, jnp.zeros((pad_window, d), rows.dtype))


def _row_gather(table, idx):
    n = idx.shape[0]
    d = table.shape[1]
    workers = SC_CORES * SC_SUBCORES
    per_worker = n // workers
    chunks = per_worker // SC_WINDOW
    assert n % workers == 0 and per_worker % SC_WINDOW == 0 and chunks >= 2
    mesh = plsc.VectorSubcoreMesh(core_axis_name="core", subcore_axis_name="subcore",
                                  num_cores=SC_CORES, num_subcores=SC_SUBCORES)

    @pl.kernel(out_type=jax.ShapeDtypeStruct((n, d), table.dtype), mesh=mesh,
               scratch_types=[pltpu.VMEM((2, SC_WINDOW), jnp.int32), pltpu.VMEM((2, SC_WINDOW, d), table.dtype),
                              pltpu.SemaphoreType.DMA((2,))],
               name="moe_row_gather")
    def gather(table_hbm, idx_hbm, out_hbm, idx_v, rows_v, sem):
        worker = lax.axis_index("subcore") * SC_CORES + lax.axis_index("core")

        def rows_of(c):
            return pl.ds(pl.multiple_of(worker * per_worker + c * SC_WINDOW, SC_WINDOW), SC_WINDOW)

        def fetch(c, b):
            return pltpu.make_async_copy(table_hbm.at[idx_v.at[b]], rows_v.at[b], sem.at[b])

        def start(c, b):
            pltpu.sync_copy(idx_hbm.at[rows_of(c)], idx_v.at[b])
            fetch(c, b).start()

        for b in range(2):
            start(b, b)

        @pl.loop(0, chunks, step=2)
        def _(c0):
            for b in range(2):
                c = c0 + b

                @pl.when(c < chunks)
                def _():
                    fetch(c, b).wait()
                    pltpu.sync_copy(rows_v.at[b], out_hbm.at[rows_of(c)])

                    @pl.when(c + 2 < chunks)
                    def _():
                        start(c + 2, b)

    return gather(table, idx)


def _expert_kernel(be_ref, bv_ref, xs_ref, wgu_ref, bgu_ref, wd_ref, bd_ref, y_ref, wgu_bf, wd_bf):
    i = pl.program_id(0)
    d_ff = wd_ref.shape[1]

    @pl.when((i == 0) | (be_ref[i] != be_ref[jnp.maximum(i - 1, 0)]))
    def _():
        wgu_bf[...] = wgu_ref[0].astype(BF16)
        wd_bf[...] = wd_ref[0].astype(BF16)

    @pl.when(bv_ref[i] > 0)
    def _():
        x = _unpack_halves(xs_ref[...]).astype(BF16)
        hgu = _dot(x, wgu_bf[...]) + bgu_ref[0]
        gate = jnp.minimum(hgu[:, :d_ff], SWIGLU_LIMIT)
        up = jnp.clip(hgu[:, d_ff:], -SWIGLU_LIMIT, SWIGLU_LIMIT)
        act = (up + 1.0) * gate * _sigmoid(SWIGLU_ALPHA * gate)
        y_ref[...] = _pack_halves(_dot(act.astype(BF16), wd_bf[...]) + bd_ref[0])

    @pl.when(bv_ref[i] == 0)
    def _():
        y_ref[...] = jnp.zeros(y_ref.shape, y_ref.dtype)


def _experts(block_expert, block_valid, xs, w_gu, b_gu3, w_down, b_down3):
    n_slots, dp = xs.shape
    n_blocks = n_slots // MOE_BLK
    d_ff, d = w_down.shape[1:]
    assert dp * 2 == d
    return pl.pallas_call(
        _expert_kernel,
        grid_spec=pltpu.PrefetchScalarGridSpec(
            num_scalar_prefetch=2,
            grid=(n_blocks,),
            in_specs=[pl.BlockSpec((MOE_BLK, dp), lambda i, be, bv: (i, 0)),
                      pl.BlockSpec((1, d, 2 * d_ff), lambda i, be, bv: (be[i], 0, 0)),
                      pl.BlockSpec((1, 1, 2 * d_ff), lambda i, be, bv: (be[i], 0, 0)),
                      pl.BlockSpec((1, d_ff, d), lambda i, be, bv: (be[i], 0, 0)),
                      pl.BlockSpec((1, 1, d), lambda i, be, bv: (be[i], 0, 0))],
            out_specs=pl.BlockSpec((MOE_BLK, dp), lambda i, be, bv: (i, 0)),
            scratch_shapes=[pltpu.VMEM((d, 2 * d_ff), BF16), pltpu.VMEM((d_ff, d), BF16)],
        ),
        out_shape=jax.ShapeDtypeStruct((n_slots, dp), jnp.uint32),
        compiler_params=_cparams(("arbitrary",)),
        name="moe_experts",
    )(block_expert, block_valid, xs, w_gu, b_gu3, w_down, b_down3)


def _combine_kernel(yk_ref, tg_ref, x1_ref, y_ref):
    tg = tg_ref[...]
    lane = lax.broadcasted_iota(jnp.int32, tg.shape, 1)
    acc = x1_ref[...]
    for k in range(TOP_K):
        gk = jnp.sum(jnp.where(lane == k, tg, 0.0), axis=-1, keepdims=True)
        acc = acc + gk * _unpack_halves(yk_ref[k])
    y_ref[...] = acc


def _combine(yk, tg, x1):
    n, d = x1.shape
    tm = POST_TILE
    return pl.pallas_call(
        _combine_kernel,
        grid=(n // tm,),
        in_specs=[pl.BlockSpec((TOP_K, tm, yk.shape[2]), lambda i: (0, i, 0)),
                  pl.BlockSpec((tm, LANES), lambda i: (i, 0)),
                  pl.BlockSpec((tm, d), lambda i: (i, 0))],
        out_specs=pl.BlockSpec((tm, d), lambda i: (i, 0)),
        out_shape=jax.ShapeDtypeStruct((n, d), F32),
        compiler_params=_cparams(("parallel",)),
        name="moe_combine",
    )(yk, tg, x1)


def _moe(hm, ti, tg, x1, w_gu, b_gu, w_down, b_down):
    n, d = hm.shape
    rank, cnt = _rank(ti)
    counts = cnt[0, :N_EXPERTS].astype(jnp.int32)
    padded = (counts + MOE_BLK - 1) // MOE_BLK * MOE_BLK
    pad_end = jnp.cumsum(padded)
    pad_start = pad_end - padded
    top_i = ti[:, :TOP_K]
    dest_km = (pad_start[top_i] + rank[:, :TOP_K]).astype(jnp.int32).T.reshape(TOP_K * n)
    n_blocks = (n * TOP_K) // MOE_BLK + N_EXPERTS
    n_slots = n_blocks * MOE_BLK
    blk_start = jnp.arange(n_blocks, dtype=jnp.int32) * MOE_BLK
    block_expert = jnp.minimum(jnp.sum((pad_end[None, :] <= blk_start[:, None]).astype(jnp.int32), axis=1),
                               N_EXPERTS - 1)
    block_valid = (blk_start < pad_end[-1]).astype(jnp.int32)
    n_pad = padded - counts
    pad_cum = jnp.cumsum(n_pad)
    j = jnp.arange(n_slots - TOP_K * n, dtype=jnp.int32)
    owner = jnp.sum((pad_cum[None, :] <= j[:, None]).astype(jnp.int32), axis=1)
    e = jnp.minimum(owner, N_EXPERTS - 1)
    in_expert = pad_start[e] + counts[e] + (j - (pad_cum[e] - n_pad[e]))
    pad_slots = jnp.where(owner < N_EXPERTS, in_expert, pad_end[-1] + (j - pad_cum[-1])).astype(jnp.int32)
    xs = _row_scatter(hm, dest_km, pad_slots, n_slots)
    yb = _experts(block_expert, block_valid, xs, w_gu, b_gu.reshape(N_EXPERTS, 1, -1), w_down,
                  b_down.reshape(N_EXPERTS, 1, -1))
    yk = _row_gather(yb, dest_km).reshape(TOP_K, n, d)
    return _combine(yk, tg, x1)


def _pad_lanes(v, fill=0.0):
    v = v.reshape(1, -1).astype(F32)
    return jnp.pad(v, ((0, 0), (0, LANES - v.shape[1])), constant_values=fill)


def kernel(x_prompt, x_sample, cache_k, cache_v, state_ssm, state_conv, page_table, meta_tokens, rel_bias_table, norm1, w_in, conv_w, a_log, dt_bias, gdn_norm, q_norm, k_norm, lam_q1, lam_k1, lam_q2, lam_k2, sub_norm, w_br_a, w_br_b, w_out, norm2, w_router, b_router, w_gu, b_gu, w_down, b_down):
    bsz, seq, d = x_prompt.shape
    dbs, dseq, _ = x_sample.shape
    depth = w_in.shape[0]
    assert depth == 1 and dseq <= SUBLANES - (CONV_W - 1) and seq % ATT_BLK == 0
    lam_init = 0.8 - 0.6 * math.exp(-0.3 * 0)
    l = 0

    w = w_in[l]
    o_z = CONV_DIM + VA
    w_main = jnp.concatenate([w[:, :o_z], w[:, o_z + 2 * H_A:]], axis=1).astype(BF16)
    w_ba = jnp.pad(w[:, o_z:o_z + 2 * H_A], ((0, 0), (0, LANES - 2 * H_A))).astype(BF16)
    grp = np.arange(QB) // DK_B
    gmat = jnp.asarray((grp[:, None] == grp[None, :]).astype(np.float32) / DK_B, dtype=BF16)
    qn_t = jnp.tile(q_norm[l].astype(F32), QB // DK_B).reshape(1, QB)
    kn_t = jnp.tile(k_norm[l].astype(F32), QB // DK_B).reshape(1, QB)
    alog_p = jnp.pad(a_log[l].astype(F32), (H_A, LANES - 2 * H_A)).reshape(1, LANES)
    dtb_p = jnp.pad(dt_bias[l].astype(F32), (H_A, LANES - 2 * H_A)).reshape(1, LANES)
    n1 = norm1[l].reshape(1, d).astype(F32)
    proj = functools.partial(_inproj, n1=n1, w_main=w_main, w_ba=w_ba, gmat=gmat, qn_t=qn_t, kn_t=kn_t,
                             alog_p=alog_p, dtb_p=dtb_p)
    lam4 = jnp.stack([lam_q1[l], lam_k1[l], lam_q2[l], lam_k2[l]]).astype(F32)
    sn = sub_norm[l].reshape(1, DV_B).astype(F32)
    gn = gdn_norm[l].reshape(1, DV_A).astype(F32)
    cw = conv_w[l].astype(F32)

    ii = np.arange(ATT_BLK)[:, None]
    jj = np.arange(LANES)[None, :]
    bk_prompt = np.stack([_bucket_np(ii - jj), _bucket_np(ATT_BLK + ii - jj)])
    t8 = np.arange(SUBLANES)[:, None]
    bk_sample = np.stack([_bucket_np(PAGE + t8 - jj), _bucket_np(t8 - jj)])
    bt_prompt = _bias_tiles(rel_bias_table.astype(F32), bk_prompt)
    bt_sample = _bias_tiles(rel_bias_table.astype(F32), bk_sample)
    far_h = rel_bias_table[N_BUCKETS - 1].astype(F32)
    far_p = jnp.broadcast_to(far_h[:, None, None], (H_B, 1, ATT_KV))
    diag_t = jnp.where(jnp.asarray(ii >= jj), bt_prompt[:, 0], NEG)
    near_t = bt_prompt[:, 1]
    far_t = jnp.broadcast_to(far_h[:, None, None], near_t.shape)
    neg_t = jnp.full(near_t.shape, NEG, F32)
    tile2 = lambda a, b_, c_, d_: jnp.concatenate([jnp.concatenate([a, b_], axis=2),
                                                   jnp.concatenate([c_, d_], axis=2)], axis=1)
    bt_prompt = jnp.stack([tile2(diag_t, neg_t, near_t, diag_t), tile2(far_t, near_t, far_t, far_t)], axis=1)
    nrow = H_B * 2 * SUBLANES
    rows_of = lambda t: jnp.broadcast_to(t[:, None], (H_B, 2, SUBLANES, LANES)).reshape(nrow, LANES)
    last_t, self_t = rows_of(bt_sample[:, 0]), rows_of(bt_sample[:, 1])
    far_col = jnp.broadcast_to(far_h[:, None, None], (H_B, 2 * SUBLANES, LANES)).reshape(nrow, LANES)

    lreal = N_META + seq
    lp = -(-lreal // ATT_KV) * ATT_KV
    fp = lp - lreal
    assert fp % SUBLANES == 0 and fp >= GDN_CHUNK and (fp + N_META) % ATT_BLK == 0 and lp % GDN_CHUNK == 0
    xp = jnp.concatenate([jnp.zeros((bsz, fp, d), F32),
                          jnp.broadcast_to(meta_tokens.astype(F32)[None], (bsz, N_META, d)), x_prompt], axis=1)
    conv_p, z_p, bg_p, qd_p, kd_p, vd_p, sga_p, sgb_p, kdb_p, vdb_p = proj(xp.reshape(bsz * lp, d), ATT_KV)
    r3 = lambda a, b_, r_: a.reshape(b_, r_, a.shape[-1])
    oa_p, ssm_p = _gdn(r3(conv_p, bsz, lp), r3(bg_p, bsz, lp), r3(z_p, bsz, lp),
                       jnp.zeros((bsz, H_A, DK_A, DV_A), F32), cw, gn, fp, GDN_CHUNK, 1, H_A)
    kbias = jnp.where(jnp.arange(lp) < fp, NEG, 0.0).astype(F32).reshape(lp // ATT_KV, ATT_KV)
    ob_p = _attn_prompt(r3(qd_p, bsz, lp), r3(kdb_p, bsz, lp), r3(vdb_p, bsz, lp), bt_prompt, far_p, kbias,
                        lam4, sn, fp // ATT_Q, lam_init)

    ns = dbs * dseq
    conv_s, z_s, bg_s, qd_s, kd_s, vd_s, sga_s, sgb_s, _, _ = proj(x_sample.reshape(ns, d), min(ns, 256))
    cs = SUBLANES
    rs = 2 * cs
    fs = rs - dseq

    def chunk_rows(a, head=None):
        a = a.reshape(dbs, dseq, a.shape[-1]).astype(F32)
        parts = [jnp.zeros((dbs, fs - (0 if head is None else head.shape[1]), a.shape[-1]), F32)]
        if head is not None:
            parts.append(head.astype(F32))
        return jnp.concatenate(parts + [a], axis=1)

    oa_s, ssm_s = _gdn(chunk_rows(conv_s, state_conv[l]), chunk_rows(bg_s), chunk_rows(z_s),
                       state_ssm[l].astype(F32), cw, gn, fs, cs, math.gcd(dbs, 4), H_A)
    pad8 = lambda a: jnp.pad(a.reshape(dbs, dseq, a.shape[-1]), ((0, 0), (0, SUBLANES - dseq), (0, 0)))
    n_pool = cache_k.shape[1]
    ck_t = jnp.transpose(cache_k[l], (0, 2, 3, 4, 1)).reshape(n_pool, QB, PAGE)
    cv_i = cache_v[l].reshape(n_pool, PAGE * H_B, DV_B)
    ob_s = _attn_sample(page_table, pad8(qd_s).astype(F32), pad8(kd_s), pad8(vd_s), ck_t, cv_i,
                        last_t, self_t, far_col, lam4, sn, dseq, lam_init)

    wa, wb, wo = w_br_a[l].astype(BF16), w_br_b[l].astype(BF16), w_out[l].astype(BF16)
    n2 = norm2[l].reshape(1, d).astype(F32)
    wr = jnp.pad(w_router[l], ((0, 0), (0, LANES - N_EXPERTS))).astype(BF16)
    br = _pad_lanes(b_router[l])
    tiles_b = seq // POST_TILE
    tiles_lp = lp // POST_TILE
    skip = (fp + N_META) // POST_TILE
    npt = bsz * seq
    assert ns % POST_TILE == 0 and seq % POST_TILE == 0 and (fp + N_META) % POST_TILE == 0
    oa_s2 = oa_s[:, fs:].reshape(ns, VA)
    ob_s2 = ob_s[:, :dseq].reshape(ns, VB)
    x1, hm, ti, tg = _post(
        (x_prompt.reshape(npt, d), oa_p.reshape(bsz * lp, VA), ob_p.reshape(bsz * lp, VB), sga_p, sgb_p),
        (x_sample.reshape(ns, d), oa_s2, ob_s2, sga_s, sgb_s), (wa, wb, wo, n2, wr, br), npt, ns,
        lambda i: (i // tiles_b) * tiles_lp + skip + i % tiles_b)

    y = _moe(hm, ti, tg, x1, w_gu[l], b_gu[l], w_down[l], b_down[l])
    y_prompt = y[:npt].reshape(bsz, seq, d)
    y_sample = y[npt:].reshape(dbs, dseq, d)

    k_prompt = r3(kd_p, bsz, lp)[:, fp:].reshape(1, bsz, lreal, H_B, 2, DK_B)
    v_prompt = r3(vd_p, bsz, lp)[:, fp:].reshape(1, bsz, lreal, H_B, DV_B)
    conv_prompt = r3(conv_p, bsz, lp)[:, lp - (CONV_W - 1):][None]
    xpad_s = jnp.concatenate([state_conv[l].astype(F32), conv_s.reshape(dbs, dseq, CONV_DIM)], axis=1)
    conv_sample = xpad_s[:, dseq:][None]
    return (y_prompt, y_sample, k_prompt, v_prompt, ssm_p[None], conv_prompt,
            kd_s.reshape(1, dbs, dseq, H_B, 2, DK_B), vd_s.reshape(1, dbs, dseq, H_B, DV_B), ssm_s[None], conv_sample)
```

```python
import functools
import math

import numpy as np
import jax
import jax.numpy as jnp
from jax import lax
from jax.experimental import pallas as pl
from jax.experimental.pallas import tpu as pltpu
from jax.experimental.pallas import tpu_sc as plsc

F32 = jnp.float32
BF16 = jnp.bfloat16
HI = lax.Precision.HIGHEST

N_META = 16
H_A, DK_A, DV_A = 4, 128, 128
CONV_W = 4
H_B, DK_B = 4, 64
DV_B = 2 * DK_B
N_BUCKETS, MAX_DIST = 32, 128
N_EXPERTS, TOP_K = 32, 4
SWIGLU_LIMIT, SWIGLU_ALPHA = 7.0, 1.702
EPS = 1e-6
PAGE = 128
QA = H_A * DK_A
VA = H_A * DV_A
QB = H_B * 2 * DK_B
VB = H_B * DV_B
CONV_DIM = 2 * QA + VA

LANES = 128
SUBLANES = 8
VMEM_LIMIT = 56 * 1024 * 1024
SC_CORES, SC_SUBCORES, SC_WINDOW = 2, 16, 64

GDN_CHUNK = 64
ATT_BLK = 128
ATT_KV = 2 * ATT_BLK
ATT_Q = 2 * ATT_BLK
PAGES_PER_STEP = 16
MOE_BLK = 256
ROW_TILE = 256
POST_TILE = 256
NEG = -1e30


def _dot(a, b, prec=None):
    return jnp.dot(a, b, preferred_element_type=F32, precision=prec)


def _dot_nt(a, b, prec=None):
    return lax.dot_general(a, b, (((1,), (1,)), ((), ())), preferred_element_type=F32, precision=prec)


def _dot_tn(a, b, prec=None):
    return lax.dot_general(a, b, (((0,), (0,)), ((), ())), preferred_element_type=F32, precision=prec)


def _split(a):
    hi = a.astype(BF16)
    return hi, (a - hi.astype(F32)).astype(BF16)


def _lhs3(a):
    hi, lo = _split(a)
    return jnp.concatenate([hi, lo, hi], axis=1)


def _rhs3(b):
    hi, lo = _split(b)
    return jnp.concatenate([hi, hi, lo], axis=0)


def _pack_halves(x):
    w = x.shape[1] // 2
    bits = lambda v: pltpu.bitcast(v.astype(BF16).astype(F32), jnp.uint32)
    return (bits(x[:, :w]) >> 16) | (bits(x[:, w:]) & jnp.uint32(0xFFFF0000))


def _unpack_halves(u):
    lo = pltpu.bitcast(u << 16, F32)
    hi = pltpu.bitcast(u & jnp.uint32(0xFFFF0000), F32)
    return jnp.concatenate([lo, hi], axis=1)


def _sigmoid(x):
    return 1.0 / (1.0 + jnp.exp(-x))


def _cparams(sem, flags=None):
    return pltpu.CompilerParams(dimension_semantics=sem, vmem_limit_bytes=VMEM_LIMIT, flags=flags)


def _bucket_np(d):
    d = np.maximum(d, 0)
    df = np.maximum(d, 1).astype(np.float32)
    max_exact = N_BUCKETS // 2
    large = max_exact + (np.log(df / np.float32(max_exact)) / np.float32(math.log(MAX_DIST / max_exact))
                         * np.float32(N_BUCKETS - max_exact)).astype(np.int32)
    return np.where(d < max_exact, d, np.minimum(large, N_BUCKETS - 1)).astype(np.int32)


def _bias_kernel(tab_ref, bk_ref, o_ref):
    for t in range(bk_ref.shape[0]):
        bk = bk_ref[t]
        for h in range(H_B):
            acc = jnp.zeros(bk.shape, F32)
            for b in range(N_BUCKETS):
                acc = jnp.where(bk == b, tab_ref[b, h], acc)
            o_ref[h, t] = acc


def _bias_tiles(table, buckets):
    t, r, _ = buckets.shape
    return pl.pallas_call(
        _bias_kernel,
        out_shape=jax.ShapeDtypeStruct((H_B, t, r, LANES), F32),
        in_specs=[pl.BlockSpec(memory_space=pltpu.SMEM), pl.BlockSpec(memory_space=pltpu.VMEM)],
        out_specs=pl.BlockSpec(memory_space=pltpu.VMEM),
        name="rel_bias_tiles",
    )(table, jnp.asarray(buckets))


def _inproj_kernel(x_ref, n1_ref, w_ref, wba_ref, gm_ref, qn_ref, kn_ref, alog_ref, dtb_ref,
                   conv_ref, z_ref, bg_ref, qd_ref, kd_ref, vd_ref, sga_ref, sgb_ref, kdb_ref, vdb_ref):
    x = x_ref[...]
    ms = jnp.mean(x * x, axis=-1, keepdims=True)
    h = (x * lax.rsqrt(ms + EPS) * n1_ref[...]).astype(BF16)
    o = 0
    conv_ref[...] = _dot(h, w_ref[:, o:o + CONV_DIM]); o += CONV_DIM
    z_ref[...] = _dot(h, w_ref[:, o:o + VA]).astype(z_ref.dtype); o += VA

    def group_norm(y, g):
        sq = y * y
        hi = sq.astype(BF16)
        lo = (sq - hi.astype(F32)).astype(BF16)
        msq = _dot(hi, gm_ref[...]) + _dot(lo, gm_ref[...])
        return y * lax.rsqrt(msq + EPS) * g

    qb = _dot(h, w_ref[:, o:o + QB]); o += QB
    qd_ref[...] = (group_norm(qb, qn_ref[...]) * (DK_B ** -0.5)).astype(qd_ref.dtype)
    kb = _dot(h, w_ref[:, o:o + QB]); o += QB
    kd = group_norm(kb, kn_ref[...])
    kd_ref[...] = kd
    kdb_ref[...] = kd.astype(kdb_ref.dtype)
    vd = _dot(h, w_ref[:, o:o + VB]); o += VB
    vd_ref[...] = vd
    vdb_ref[...] = vd.astype(vdb_ref.dtype)
    d_model = x.shape[1]
    sga_ref[...] = _sigmoid(_dot(h, w_ref[:, o:o + d_model])).astype(sga_ref.dtype); o += d_model
    sgb_ref[...] = _sigmoid(_dot(h, w_ref[:, o:o + d_model])).astype(sgb_ref.dtype)
    t = _dot(h, wba_ref[...])
    lane = lax.broadcasted_iota(jnp.int32, t.shape, 1)
    ta = t + dtb_ref[...]
    sp = jnp.maximum(ta, 0.0) + jnp.log(1.0 + jnp.exp(-jnp.abs(ta)))
    bg_ref[...] = jnp.where(lane < H_A, _sigmoid(t), -jnp.exp(alog_ref[...]) * sp)


def _inproj(x2d, tm, n1, w_main, w_ba, gmat, qn_t, kn_t, alog_p, dtb_p):
    n, d = x2d.shape
    assert n % tm == 0
    row = lambda c: pl.BlockSpec((tm, c), lambda i: (i, 0))
    full = lambda a: pl.BlockSpec(a.shape, lambda i: (0,) * a.ndim)
    outs = [(CONV_DIM, F32), (VA, BF16), (LANES, F32), (QB, BF16), (QB, F32), (VB, F32), (d, BF16), (d, BF16),
            (QB, BF16), (VB, BF16)]
    return pl.pallas_call(
        _inproj_kernel,
        grid=(n // tm,),
        in_specs=[row(d)] + [full(a) for a in (n1, w_main, w_ba, gmat, qn_t, kn_t, alog_p, dtb_p)],
        out_specs=[row(c) for c, _ in outs],
        out_shape=[jax.ShapeDtypeStruct((n, c), dt) for c, dt in outs],
        compiler_params=_cparams(("parallel",)),
        name="in_proj",
    )(x2d, n1, w_main, w_ba, gmat, qn_t, kn_t, alog_p, dtb_p)


def _gdn_kernel(fv, c, q_ref, k_ref, v_ref, cwq_ref, cwk_ref, cwv_ref, bg_ref, z_ref, s0_ref, gn_ref,
                o_ref, st_ref):
    nb, r, wd = q_ref.shape
    hg = wd // LANES
    head0 = pl.program_id(1) * hg
    chains = [(b, hh) for b in range(nb) for hh in range(hg)]

    ri = lax.broadcasted_iota(jnp.int32, (c, c), 0)
    ci = lax.broadcasted_iota(jnp.int32, (c, c), 1)
    incl = ri >= ci
    strict = ri > ci
    eye = (ri == ci).astype(F32)
    lane = lax.broadcasted_iota(jnp.int32, (c, LANES), 1)
    sub_t = lax.broadcasted_iota(jnp.int32, (LANES, c), 0)
    rowid = lax.broadcasted_iota(jnp.int32, (c, 1), 0)
    colid = lax.broadcasted_iota(jnp.int32, (1, c), 1)

    def conv(x_ref, cw_ref, b, cols, r0):
        w = x_ref[b, pl.ds(r0 - SUBLANES, c + SUBLANES), cols]
        acc = w[SUBLANES - 3:SUBLANES - 3 + c] * cw_ref[0:1, cols]
        for j in range(1, CONV_W):
            acc = acc + w[SUBLANES - 3 + j:SUBLANES - 3 + j + c] * cw_ref[j:j + 1, cols]
        return acc * _sigmoid(acc)

    def elementwise(b, hh, r0):
        cols = slice(hh * LANES, (hh + 1) * LANES)
        head = head0 + hh
        vcol = ((r0 + rowid) >= fv).astype(F32)
        vrow = ((r0 + colid) >= fv).astype(F32)
        qv = conv(q_ref, cwq_ref, b, cols, r0)
        kv = conv(k_ref, cwk_ref, b, cols, r0)
        v = conv(v_ref, cwv_ref, b, cols, r0) * vcol
        q = qv * lax.rsqrt(jnp.sum(qv * qv, axis=-1, keepdims=True) + EPS) * (DK_A ** -0.5) * vcol
        k = kv * lax.rsqrt(jnp.sum(kv * kv, axis=-1, keepdims=True) + EPS) * vcol
        bgc = bg_ref[b, pl.ds(r0, c), :]
        beta = jnp.sum(jnp.where(lane == head, bgc, 0.0), axis=-1, keepdims=True) * vcol
        g_col = jnp.sum(jnp.where(lane == H_A + head, bgc, 0.0), axis=-1, keepdims=True) * vcol
        g_row = jnp.sum(jnp.where(sub_t == H_A + head, bgc.T, 0.0), axis=0, keepdims=True) * vrow
        gc = jnp.sum(jnp.where(incl, g_row, 0.0), axis=-1, keepdims=True)
        gr = jnp.sum(jnp.where(ri <= ci, g_col, 0.0), axis=0, keepdims=True)
        decay = jnp.where(incl, jnp.exp(jnp.where(incl, gc - gr, 0.0)), 0.0)
        kb = k * beta
        eg = jnp.exp(gc)
        g_last = jnp.sum(jnp.where(rowid == c - 1, gc, 0.0), axis=0, keepdims=True)
        k_dt = (k * jnp.exp(g_last - gc)).T
        lhs = jnp.concatenate([kb, q], axis=0).astype(BF16)
        rhs = jnp.concatenate([v * beta, kb * eg], axis=-1)
        return lhs, k.astype(BF16), decay, rhs, q * eg, k_dt, jnp.exp(g_last)

    def prepare_a(j):
        r0 = pl.multiple_of(j * c, c)
        vec = [elementwise(b, hh, r0) for b, hh in chains]
        kk = [_dot_nt(v_[0], v_[1]) for v_ in vec]
        return vec, kk

    def prepare_b(vec, kk):
        low = [jnp.where(strict, kk_[:c] * v_[2], 0.0) for v_, kk_ in zip(vec, kk)]
        intra = [kk_[c:] * v_[2] for v_, kk_ in zip(vec, kk)]
        inv = [eye - lw for lw in low]
        levels = int(math.log2(c)) - 1
        pw = [_dot(_lhs3(lw), _rhs3(lw)) for lw in low]
        for lev in range(levels):
            pw_r = [_rhs3(p) for p in pw]
            if lev + 1 < levels:
                pw = [_dot(_lhs3(p), r_) for p, r_ in zip(pw, pw_r)]
            inv = [iv + _dot(_lhs3(iv), r_) for iv, r_ in zip(inv, pw_r)]
        sol = [_dot(_lhs3(iv), _rhs3(v_[3])) for iv, v_ in zip(inv, vec)]
        out = []
        for v_, sl, it in zip(vec, sol, intra):
            on_state = jnp.concatenate([sl[:, DV_A:], v_[4]], axis=0).astype(BF16)
            on_u = jnp.concatenate([it, v_[5]], axis=0).astype(BF16)
            out.append((sl[:, :DV_A], on_state, on_u, v_[6]))
        return tuple(out)

    def apply_a(prepared, states):
        return [_dot(p[1], s.astype(BF16)) for p, s in zip(prepared, states)]

    def apply_b(j, prepared, states, ps):
        r0 = pl.multiple_of(j * c, c)
        u = [p[0] - ps_[:c] for p, ps_ in zip(prepared, ps)]
        pu = [_dot(p[2], u_.astype(BF16)) for p, u_ in zip(prepared, u)]
        new_states = []
        for (b, hh), p, s, ps_, pu_ in zip(chains, prepared, states, ps, pu):
            cols = slice(hh * LANES, (hh + 1) * LANES)
            o = ps_[c:] + pu_[:c]
            on = o * lax.rsqrt(jnp.mean(o * o, axis=-1, keepdims=True) + EPS) * gn_ref[...]
            zc = z_ref[b, pl.ds(r0, c), cols].astype(F32)
            o_ref[b, pl.ds(r0, c), cols] = (on * (zc * _sigmoid(zc))).astype(o_ref.dtype)
            new_states.append(s * p[3] + pu_[c:])
        return tuple(new_states)

    def chunk(j, carry):
        prepared, states = carry
        ps = apply_a(prepared, states)
        vec, kk = prepare_a(j + 1)
        states = apply_b(j, prepared, states, ps)
        return prepare_b(vec, kk), states

    j0 = fv // c
    last = r // c - 1
    o_ref[:, 0:j0 * c, :] = jnp.zeros((nb, j0 * c, wd), o_ref.dtype)
    init = (prepare_b(*prepare_a(j0)), tuple(s0_ref[b, hh] for b, hh in chains))
    prepared, states = lax.fori_loop(j0, last, chunk, init)
    for (b, hh), s in zip(chains, apply_b(last, prepared, states, apply_a(prepared, states))):
        st_ref[b, hh] = s


def _gdn(conv3, bg3, z3, s0, conv_w, gn, fv, c, nb, hg):
    bsz, r, _ = conv3.shape
    assert r % c == 0 and fv // c >= 1 and c >= SUBLANES and bsz % nb == 0 and H_A % hg == 0
    ng = H_A // hg
    wd = hg * LANES
    blk = lambda off: pl.BlockSpec((nb, r, wd), lambda i, g: (i, 0, off * ng + g))
    cw = lambda off: pl.BlockSpec((CONV_W, wd), lambda i, g: (0, off * ng + g))
    st = pl.BlockSpec((nb, hg, DK_A, DV_A), lambda i, g: (i, g, 0, 0))
    return pl.pallas_call(
        functools.partial(_gdn_kernel, fv, c),
        grid=(bsz // nb, ng),
        in_specs=[blk(0), blk(1), blk(2), cw(0), cw(1), cw(2),
                  pl.BlockSpec((nb, r, LANES), lambda i, g: (i, 0, 0)), blk(0), st,
                  pl.BlockSpec((1, LANES), lambda i, g: (0, 0))],
        out_specs=[blk(0), st],
        out_shape=[jax.ShapeDtypeStruct((bsz, r, VA), F32), jax.ShapeDtypeStruct((bsz, H_A, DK_A, DV_A), F32)],
        compiler_params=_cparams(("parallel", "parallel")),
        name="gdn",
    )(conv3, conv3, conv3, conv_w, conv_w, conv_w, bg3, z3, s0, gn)


def _lambda(lam_ref, lam_init):
    l1 = jnp.sum(lam_ref[0:1, :] * lam_ref[1:2, :], axis=-1, keepdims=True)
    l2 = jnp.sum(lam_ref[2:3, :] * lam_ref[3:4, :], axis=-1, keepdims=True)
    return jnp.exp(l1) - jnp.exp(l2) + lam_init


def _sub_norm(o0, o1, lam, sn, lam_init):
    ob = o0 - lam * o1
    return ob * lax.rsqrt(jnp.mean(ob * ob, axis=-1, keepdims=True) + EPS) * sn * (1.0 - lam_init)


def _attn_prompt_kernel(first_q, lam_init, q_ref, k_ref, v_ref, bt_ref, far_ref, kb_ref, lam_ref, sn_ref, o_ref):
    qi = pl.program_id(2)

    @pl.when(qi < first_q)
    def _():
        o_ref[0] = jnp.zeros(o_ref.shape[1:], o_ref.dtype)

    @pl.when(qi >= first_q)
    def _():
        q = q_ref[0]
        lane = lax.broadcasted_iota(jnp.int32, q.shape, 1)
        zero = jnp.zeros_like(q)
        qs = jnp.concatenate([jnp.where(lane < DK_B, q, zero), jnp.where(lane >= DK_B, q, zero)], axis=0)

        def scores(j):
            k0 = pl.multiple_of(j * ATT_KV, ATT_KV)
            return _dot_nt(qs, k_ref[0, pl.ds(k0, ATT_KV), :])

        def step(j, carry, bias, last=False):
            m, l, acc, s_raw = carry
            s_next = s_raw if last else scores(j + 1)
            k0 = pl.multiple_of(j * ATT_KV, ATT_KV)
            s = s_raw + bias
            m_new = jnp.maximum(m, jnp.max(s, axis=-1, keepdims=True))
            alpha = jnp.exp(m - m_new)
            p = jnp.exp(s - m_new)
            l = alpha * l + jnp.sum(p, axis=-1, keepdims=True)
            acc = alpha * acc + _dot(p.astype(BF16), v_ref[0, pl.ds(k0, ATT_KV), :])
            return m_new, l, acc, s_next

        def tile_bias(j, which):
            t = bt_ref[0, which] + kb_ref[pl.ds(j, 1), :]
            return jnp.concatenate([t, t], axis=0)

        carry = (jnp.full((2 * ATT_Q, 1), NEG, F32), jnp.zeros((2 * ATT_Q, 1), F32),
                 jnp.zeros((2 * ATT_Q, DV_B), F32), scores(0))
        carry = lax.fori_loop(0, qi - 1, lambda j, c: step(j, c, far_ref[0] + kb_ref[pl.ds(j, 1), :]), carry)
        carry = lax.cond(qi >= 1, lambda c: step(qi - 1, c, tile_bias(qi - 1, 1)), lambda c: c, carry)
        _, l, acc, _ = step(qi, carry, tile_bias(qi, 0), last=True)
        o = acc / l
        o_ref[0] = _sub_norm(o[:ATT_Q], o[ATT_Q:], _lambda(lam_ref, lam_init), sn_ref[...],
                             lam_init).astype(o_ref.dtype)


def _attn_prompt(qd3, kd3, vd3, btiles, far, kbias, lam4, sn, first_q, lam_init):
    bsz, lp, _ = qd3.shape
    assert lp % ATT_KV == 0 and ATT_Q == ATT_KV
    kv = pl.BlockSpec((1, lp, LANES), lambda b, h, i: (b, 0, h))
    qo = pl.BlockSpec((1, ATT_Q, LANES), lambda b, h, i: (b, i, h))
    full = lambda a: pl.BlockSpec(a.shape, lambda b, h, i: (0,) * a.ndim)
    return pl.pallas_call(
        functools.partial(_attn_prompt_kernel, first_q, lam_init),
        grid=(bsz, H_B, lp // ATT_Q),
        in_specs=[qo, kv, kv,
                  pl.BlockSpec((1, 2, ATT_Q, ATT_KV), lambda b, h, i: (h, 0, 0, 0)),
                  pl.BlockSpec((1, 1, ATT_KV), lambda b, h, i: (h, 0, 0)),
                  full(kbias), full(lam4), full(sn)],
        out_specs=qo,
        out_shape=jax.ShapeDtypeStruct((bsz, lp, VB), F32),
        compiler_params=_cparams(("parallel", "parallel", "parallel")),
        name="diff_attn_prompt",
    )(qd3, kd3, vd3, btiles, far, kbias, lam4, sn)


def _attn_sample_kernel(npp, t_new, lam_init, pt_ref, q_ref, kn_ref, vn_ref, ck_ref, cv_ref, last_ref, self_ref,
                        far_ref, lam_ref, sn_ref, o_ref, qm, kself, vself, m_s, l_s, acc_s, kbuf, vbuf, sem):
    b = pl.program_id(0)
    st = pl.program_id(1)
    nb = pl.num_programs(0)
    nst = pl.num_programs(1)
    nrow = H_B * 2 * SUBLANES
    hrows = 2 * SUBLANES
    t = b * nst + st
    slot = t % 2

    def page_copies(bb, ss, sl):
        out = []
        for g in range(npp):
            page = pt_ref[bb, ss * npp + g]
            out.append(pltpu.make_async_copy(ck_ref.at[page], kbuf.at[sl, g], sem.at[sl]))
            out.append(pltpu.make_async_copy(cv_ref.at[page], vbuf.at[sl, g], sem.at[sl]))
        return out

    @pl.when(t == 0)
    def _():
        for cp in page_copies(0, 0, 0):
            cp.start()

    @pl.when(t + 1 < nb * nst)
    def _():
        wrap = st + 1 == nst
        for cp in page_copies(jnp.where(wrap, b + 1, b), jnp.where(wrap, 0, st + 1), 1 - slot):
            cp.start()

    for cp in page_copies(b, st, slot):
        cp.wait()

    def update(s, values_of_head):
        m_old = m_s[...]
        m_new = jnp.maximum(m_old, jnp.max(s, axis=-1, keepdims=True))
        alpha = jnp.exp(m_old - m_new)
        pr = jnp.exp(s - m_new)
        l_s[...] = alpha * l_s[...] + jnp.sum(pr, axis=-1, keepdims=True)
        prb = pr.astype(BF16)
        pv = [_dot(prb[h * hrows:(h + 1) * hrows], values_of_head(h)) for h in range(H_B)]
        acc_s[...] = alpha * acc_s[...] + jnp.concatenate(pv, axis=0)
        m_s[...] = m_new

    @pl.when(st == 0)
    def _():
        q8 = q_ref[0]
        lane = lax.broadcasted_iota(jnp.int32, q8.shape, 1)
        qm[...] = jnp.concatenate([jnp.where(lane // DK_B == g, q8, 0.0) for g in range(2 * H_B)],
                                  axis=0).astype(qm.dtype)
        kself[...] = jnp.zeros(kself.shape, F32)
        vself[...] = jnp.zeros(vself.shape, F32)
        kself[0:SUBLANES, :] = kn_ref[0]
        vself[0:SUBLANES, :] = vn_ref[0]
        m_s[...] = jnp.full(m_s.shape, NEG, F32)
        l_s[...] = jnp.zeros(l_s.shape, F32)
        acc_s[...] = jnp.zeros(acc_s.shape, F32)
        rows = lax.broadcasted_iota(jnp.int32, (nrow, PAGE), 0) % SUBLANES
        cols = lax.broadcasted_iota(jnp.int32, (nrow, PAGE), 1)
        ok = (cols <= rows) & (cols < t_new)
        s = jnp.where(ok, _dot_nt(qm[...], kself[...].astype(BF16)) + self_ref[...], NEG)
        update(s, lambda h: vself[:, h * DV_B:(h + 1) * DV_B].astype(BF16))

    far = far_ref[:, 0:1]
    parts = []
    for g in range(npp):
        bias = far if g < npp - 1 else jnp.where(st == nst - 1, last_ref[...], far)
        parts.append(_dot(qm[...], kbuf[slot, g].astype(BF16)) + bias)

    def page_values(h):
        return jnp.concatenate([vbuf[slot, g, pl.ds(h, PAGE, stride=H_B), :] for g in range(npp)],
                               axis=0).astype(BF16)

    update(jnp.concatenate(parts, axis=1), page_values)

    @pl.when(st == nst - 1)
    def _():
        o = acc_s[...] / l_s[...]
        lam = _lambda(lam_ref, lam_init)
        for h in range(H_B):
            r0 = h * hrows
            o_ref[0, :, h * DV_B:(h + 1) * DV_B] = _sub_norm(o[r0:r0 + SUBLANES], o[r0 + SUBLANES:r0 + hrows], lam,
                                                             sn_ref[...], lam_init)


def _attn_sample(page_table, q8, kn8, vn8, ck_t, cv_i, last_t, self_t, far_col, lam4, sn, t_new, lam_init):
    bsz, npg = page_table.shape
    npp = math.gcd(PAGES_PER_STEP, npg)
    nrow = H_B * 2 * SUBLANES
    tok = pl.BlockSpec((1, SUBLANES, QB), lambda b, s, pt: (b, 0, 0))
    hbm = pl.BlockSpec(memory_space=pl.ANY)
    full = lambda a: pl.BlockSpec(a.shape, lambda b, s, pt: (0,) * a.ndim)
    return pl.pallas_call(
        functools.partial(_attn_sample_kernel, npp, t_new, lam_init),
        grid_spec=pltpu.PrefetchScalarGridSpec(
            num_scalar_prefetch=1,
            grid=(bsz, npg // npp),
            in_specs=[tok, tok, tok, hbm, hbm, full(last_t), full(self_t), full(far_col), full(lam4), full(sn)],
            out_specs=tok,
            scratch_shapes=[pltpu.VMEM((nrow, QB), BF16), pltpu.VMEM((PAGE, QB), F32), pltpu.VMEM((PAGE, VB), F32),
                            pltpu.VMEM((nrow, 1), F32), pltpu.VMEM((nrow, 1), F32), pltpu.VMEM((nrow, DV_B), F32),
                            pltpu.VMEM((2, npp, QB, PAGE), F32), pltpu.VMEM((2, npp, PAGE * H_B, DV_B), F32),
                            pltpu.SemaphoreType.DMA((2,))],
        ),
        out_shape=jax.ShapeDtypeStruct((bsz, SUBLANES, VB), F32),
        compiler_params=_cparams(("arbitrary", "arbitrary")),
        name="diff_attn_sample",
    )(page_table, q8, kn8, vn8, ck_t, cv_i, last_t, self_t, far_col, lam4, sn)


def _post_kernel(prompt_tiles, *refs):
    wa_ref, wb_ref, wo_ref, n2_ref, wr_ref, br_ref, x1_ref, hm_ref, ti_ref, tg_ref = refs[10:]
    is_sample = pl.program_id(0) >= prompt_tiles
    x, oa, ob, sga, sgb = (jnp.where(is_sample, s_ref[...], p_ref[...]) for p_ref, s_ref in zip(refs[:5], refs[5:10]))
    ya = _dot(oa.astype(BF16), wa_ref[...])
    yb = _dot(ob.astype(BF16), wb_ref[...])
    merged = sga.astype(F32) * ya + sgb.astype(F32) * yb
    x1 = x + _dot(merged.astype(BF16), wo_ref[...])
    x1_ref[...] = x1
    hm = x1 * lax.rsqrt(jnp.mean(x1 * x1, axis=-1, keepdims=True) + EPS) * n2_ref[...]
    hm_ref[...] = _pack_halves(hm)
    logits = _dot(hm.astype(BF16), wr_ref[...]) + br_ref[...]
    lane = lax.broadcasted_iota(jnp.int32, logits.shape, 1)
    lane_f = lane.astype(F32)
    work = jnp.where(lane < N_EXPERTS, logits, -jnp.inf)
    ti = jnp.zeros(logits.shape, F32)
    tg = jnp.zeros(logits.shape, F32)
    v0 = None
    den = None
    for k in range(TOP_K):
        vk = jnp.max(work, axis=-1, keepdims=True)
        ik = jnp.min(jnp.where(work == vk, lane_f, float(LANES)), axis=-1, keepdims=True)
        work = jnp.where(lane_f == ik, -jnp.inf, work)
        if k == 0:
            v0 = vk
        ek = jnp.exp(vk - v0)
        den = ek if k == 0 else den + ek
        ti = jnp.where(lane == k, ik, ti)
        tg = jnp.where(lane == k, ek, tg)
    ti_ref[...] = ti.astype(jnp.int32)
    tg_ref[...] = tg / den


def _post(prompt_in, sample_in, dense, n_prompt, n_sample, prompt_map):
    d = prompt_in[0].shape[1]
    tm = POST_TILE
    tp = n_prompt // tm
    n_total = n_prompt + n_sample
    widths = (d, VA, VB, d, d)
    full = lambda a: pl.BlockSpec(a.shape, lambda i: (0,) * a.ndim)
    p_spec = lambda k, c: pl.BlockSpec((tm, c), lambda i: ((jnp.minimum(i, tp - 1) if k == 0
                                                             else prompt_map(jnp.minimum(i, tp - 1))), 0))
    s_spec = lambda c: pl.BlockSpec((tm, c), lambda i: (jnp.maximum(i - tp, 0), 0))
    rout = lambda c: pl.BlockSpec((tm, c), lambda i: (i, 0))
    return pl.pallas_call(
        functools.partial(_post_kernel, tp),
        grid=(n_total // tm,),
        in_specs=[p_spec(k, c) for k, c in enumerate(widths)] + [s_spec(c) for c in widths]
                 + [full(a) for a in dense],
        out_specs=[rout(d), rout(d // 2), rout(LANES), rout(LANES)],
        out_shape=[jax.ShapeDtypeStruct((n_total, d), F32), jax.ShapeDtypeStruct((n_total, d // 2), jnp.uint32),
                   jax.ShapeDtypeStruct((n_total, LANES), jnp.int32), jax.ShapeDtypeStruct((n_total, LANES), F32)],
        compiler_params=_cparams(("parallel",)),
        name="merge_outproj_router",
    )(*prompt_in, *sample_in, *dense)


def _rank_kernel(ti_ref, rank_ref, cnt_ref, base):
    i = pl.program_id(0)
    tm = ti_ref.shape[0]

    @pl.when(i == 0)
    def _():
        base[...] = jnp.zeros(base.shape, F32)

    ti = ti_ref[...].astype(F32)
    lane = lax.broadcasted_iota(jnp.int32, ti.shape, 1)
    lane_f = lane.astype(F32)
    sel = [jnp.sum(jnp.where(lane == k, ti, 0.0), axis=-1, keepdims=True) for k in range(TOP_K)]
    onehot = jnp.zeros(ti.shape, F32)
    for k in range(TOP_K):
        onehot = onehot + (lane_f == sel[k]).astype(F32)
    ri = lax.broadcasted_iota(jnp.int32, (tm, tm), 0)
    ci = lax.broadcasted_iota(jnp.int32, (tm, tm), 1)
    before = _dot((ri > ci).astype(BF16), onehot.astype(BF16)) + base[...]
    rank = jnp.zeros(ti.shape, jnp.int32)
    for k in range(TOP_K):
        rk = jnp.sum(jnp.where(lane_f == sel[k], before, 0.0), axis=-1, keepdims=True)
        rank = jnp.where(lane == k, rk.astype(jnp.int32), rank)
    rank_ref[...] = rank
    base[...] = base[...] + jnp.sum(onehot, axis=0, keepdims=True)
    cnt_ref[...] = base[...]


def _rank(ti):
    n = ti.shape[0]
    tm = ROW_TILE
    return pl.pallas_call(
        _rank_kernel,
        grid=(n // tm,),
        in_specs=[pl.BlockSpec((tm, LANES), lambda i: (i, 0))],
        out_specs=[pl.BlockSpec((tm, LANES), lambda i: (i, 0)), pl.BlockSpec((1, LANES), lambda i: (0, 0))],
        out_shape=[jax.ShapeDtypeStruct((n, LANES), jnp.int32), jax.ShapeDtypeStruct((1, LANES), F32)],
        scratch_shapes=[pltpu.VMEM((1, LANES), F32)],
        compiler_params=_cparams(("arbitrary",)),
        name="moe_rank",
    )(ti)


def _row_scatter(rows, dest_km, pad_slots, n_slots):
    n, d = rows.shape
    workers = SC_CORES * SC_SUBCORES
    per_worker = n // workers
    window = max(w for w in range(SUBLANES, SC_WINDOW + 33, SUBLANES) if per_worker % w == 0)
    chunks = per_worker // window
    pad_per_worker = pad_slots.shape[0] // workers
    pad_window = SC_WINDOW // 2
    assert n % workers == 0 and pad_slots.shape[0] % workers == 0 and pad_per_worker % pad_window == 0
    assert dest_km.shape[0] == TOP_K * n and pad_slots.shape[0] + TOP_K * n == n_slots
    mesh = plsc.VectorSubcoreMesh(core_axis_name="core", subcore_axis_name="subcore",
                                  num_cores=SC_CORES, num_subcores=SC_SUBCORES)

    @pl.kernel(out_type=jax.ShapeDtypeStruct((n_slots, d), rows.dtype), mesh=mesh,
               scratch_types=[pltpu.VMEM((2, TOP_K, window), jnp.int32), pltpu.VMEM((2, window, d), rows.dtype),
                              pltpu.VMEM((pad_window,), jnp.int32), pltpu.VMEM((pad_window, d), rows.dtype),
                              pltpu.SemaphoreType.DMA((2,)), pltpu.SemaphoreType.DMA],
               name="moe_row_scatter")
    def scatter(rows_hbm, dest_hbm, pad_hbm, zero_hbm, out_hbm, idx_v, rows_v, pad_v, zero_v, sem, pad_sem):
        worker = lax.axis_index("subcore") * SC_CORES + lax.axis_index("core")

        def stores(b):
            return [pltpu.make_async_copy(rows_v.at[b], out_hbm.at[idx_v.at[b, k]], sem.at[b])
                    for k in range(TOP_K)]

        for c in range(chunks):
            b = c % 2
            if c >= 2:
                for cp in stores(b):
                    cp.wait()
            first = pl.multiple_of(worker * per_worker + c * window, SUBLANES)
            pltpu.sync_copy(rows_hbm.at[pl.ds(first, window)], rows_v.at[b])
            for k in range(TOP_K):
                pltpu.sync_copy(dest_hbm.at[pl.ds(k * n + first, window)], idx_v.at[b, k])
            for cp in stores(b):
                cp.start()
        for b in range(min(chunks, 2)):
            for cp in stores(b):
                cp.wait()

        pltpu.sync_copy(zero_hbm, zero_v)
        for c in range(pad_per_worker // pad_window):
            first = pl.multiple_of(worker * pad_per_worker + c * pad_window, pad_window)
            pltpu.sync_copy(pad_hbm.at[pl.ds(first, pad_window)], pad_v)
            pltpu.async_copy(zero_v, out_hbm.at[pad_v], pad_sem).wait()

    return scatter(rows, dest_km, pad_slots, jnp.zeros((pad_window, d), rows.dtype))


def _row_gather(table, idx):
    n = idx.shape[0]
    d = table.shape[1]
    workers = SC_CORES * SC_SUBCORES
    per_worker = n // workers
    chunks = per_worker // SC_WINDOW
    assert n % workers == 0 and per_worker % SC_WINDOW == 0 and chunks >= 2
    mesh = plsc.VectorSubcoreMesh(core_axis_name="core", subcore_axis_name="subcore",
                                  num_cores=SC_CORES, num_subcores=SC_SUBCORES)

    @pl.kernel(out_type=jax.ShapeDtypeStruct((n, d), table.dtype), mesh=mesh,
               scratch_types=[pltpu.VMEM((2, SC_WINDOW), jnp.int32), pltpu.VMEM((2, SC_WINDOW, d), table.dtype),
                              pltpu.SemaphoreType.DMA((2,))],
               name="moe_row_gather")
    def gather(table_hbm, idx_hbm, out_hbm, idx_v, rows_v, sem):
        worker = lax.axis_index("subcore") * SC_CORES + lax.axis_index("core")

        def rows_of(c):
            return pl.ds(pl.multiple_of(worker * per_worker + c * SC_WINDOW, SC_WINDOW), SC_WINDOW)

        def fetch(c, b):
            return pltpu.make_async_copy(table_hbm.at[idx_v.at[b]], rows_v.at[b], sem.at[b])

        def start(c, b):
            pltpu.sync_copy(idx_hbm.at[rows_of(c)], idx_v.at[b])
            fetch(c, b).start()

        for b in range(2):
            start(b, b)

        @pl.loop(0, chunks, step=2)
        def _(c0):
            for b in range(2):
                c = c0 + b

                @pl.when(c < chunks)
                def _():
                    fetch(c, b).wait()
                    pltpu.sync_copy(rows_v.at[b], out_hbm.at[rows_of(c)])

                    @pl.when(c + 2 < chunks)
                    def _():
                        start(c + 2, b)

    return gather(table, idx)


def _expert_kernel(be_ref, bv_ref, xs_ref, wgu_ref, bgu_ref, wd_ref, bd_ref, y_ref, wgu_bf, wd_bf):
    i = pl.program_id(0)
    d_ff = wd_ref.shape[1]

    @pl.when((i == 0) | (be_ref[i] != be_ref[jnp.maximum(i - 1, 0)]))
    def _():
        wgu_bf[...] = wgu_ref[0].astype(BF16)
        wd_bf[...] = wd_ref[0].astype(BF16)

    @pl.when(bv_ref[i] > 0)
    def _():
        x = _unpack_halves(xs_ref[...]).astype(BF16)
        hgu = _dot(x, wgu_bf[...]) + bgu_ref[0]
        gate = jnp.minimum(hgu[:, :d_ff], SWIGLU_LIMIT)
        up = jnp.clip(hgu[:, d_ff:], -SWIGLU_LIMIT, SWIGLU_LIMIT)
        act = (up + 1.0) * gate * _sigmoid(SWIGLU_ALPHA * gate)
        y_ref[...] = _pack_halves(_dot(act.astype(BF16), wd_bf[...]) + bd_ref[0])

    @pl.when(bv_ref[i] == 0)
    def _():
        y_ref[...] = jnp.zeros(y_ref.shape, y_ref.dtype)


def _experts(block_expert, block_valid, xs, w_gu, b_gu3, w_down, b_down3):
    n_slots, dp = xs.shape
    n_blocks = n_slots // MOE_BLK
    d_ff, d = w_down.shape[1:]
    assert dp * 2 == d
    return pl.pallas_call(
        _expert_kernel,
        grid_spec=pltpu.PrefetchScalarGridSpec(
            num_scalar_prefetch=2,
            grid=(n_blocks,),
            in_specs=[pl.BlockSpec((MOE_BLK, dp), lambda i, be, bv: (i, 0)),
                      pl.BlockSpec((1, d, 2 * d_ff), lambda i, be, bv: (be[i], 0, 0)),
                      pl.BlockSpec((1, 1, 2 * d_ff), lambda i, be, bv: (be[i], 0, 0)),
                      pl.BlockSpec((1, d_ff, d), lambda i, be, bv: (be[i], 0, 0)),
                      pl.BlockSpec((1, 1, d), lambda i, be, bv: (be[i], 0, 0))],
            out_specs=pl.BlockSpec((MOE_BLK, dp), lambda i, be, bv: (i, 0)),
            scratch_shapes=[pltpu.VMEM((d, 2 * d_ff), BF16), pltpu.VMEM((d_ff, d), BF16)],
        ),
        out_shape=jax.ShapeDtypeStruct((n_slots, dp), jnp.uint32),
        compiler_params=_cparams(("arbitrary",)),
        name="moe_experts",
    )(block_expert, block_valid, xs, w_gu, b_gu3, w_down, b_down3)


def _combine_kernel(prompt_tiles, yk_ref, tg_ref, x1_ref, yp_ref, ys_ref):
    i = pl.program_id(0)
    tg = tg_ref[...]
    lane = lax.broadcasted_iota(jnp.int32, tg.shape, 1)
    acc = x1_ref[...]
    for k in range(TOP_K):
        gk = jnp.sum(jnp.where(lane == k, tg, 0.0), axis=-1, keepdims=True)
        acc = acc + gk * _unpack_halves(yk_ref[k])

    @pl.when(i < prompt_tiles)
    def _():
        yp_ref[...] = acc

    @pl.when(i >= prompt_tiles)
    def _():
        ys_ref[...] = acc


def _combine(yk, tg, x1, n_prompt):
    n, d = x1.shape
    tm = POST_TILE
    tp = n_prompt // tm
    assert n_prompt % tm == 0 and 0 < tp < n // tm
    return pl.pallas_call(
        functools.partial(_combine_kernel, tp),
        grid=(n // tm,),
        in_specs=[pl.BlockSpec((TOP_K, tm, yk.shape[2]), lambda i: (0, i, 0)),
                  pl.BlockSpec((tm, LANES), lambda i: (i, 0)),
                  pl.BlockSpec((tm, d), lambda i: (i, 0))],
        out_specs=[pl.BlockSpec((tm, d), lambda i: (jnp.minimum(i, tp - 1), 0)),
                   pl.BlockSpec((tm, d), lambda i: (jnp.maximum(i - tp, 0), 0))],
        out_shape=[jax.ShapeDtypeStruct((n_prompt, d), F32), jax.ShapeDtypeStruct((n - n_prompt, d), F32)],
        compiler_params=_cparams(("arbitrary",)),
        name="moe_combine",
    )(yk, tg, x1)


def _moe(hm, ti, tg, x1, w_gu, b_gu, w_down, b_down, n_prompt):
    n, d = hm.shape
    rank, cnt = _rank(ti)
    counts = cnt[0, :N_EXPERTS].astype(jnp.int32)
    padded = (counts + MOE_BLK - 1) // MOE_BLK * MOE_BLK
    pad_end = jnp.cumsum(padded)
    pad_start = pad_end - padded
    top_i = ti[:, :TOP_K]
    dest_km = (pad_start[top_i] + rank[:, :TOP_K]).astype(jnp.int32).T.reshape(TOP_K * n)
    n_blocks = (n * TOP_K) // MOE_BLK + N_EXPERTS
    n_slots = n_blocks * MOE_BLK
    blk_start = jnp.arange(n_blocks, dtype=jnp.int32) * MOE_BLK
    block_expert = jnp.minimum(jnp.sum((pad_end[None, :] <= blk_start[:, None]).astype(jnp.int32), axis=1),
                               N_EXPERTS - 1)
    block_valid = (blk_start < pad_end[-1]).astype(jnp.int32)
    n_pad = padded - counts
    pad_cum = jnp.cumsum(n_pad)
    j = jnp.arange(n_slots - TOP_K * n, dtype=jnp.int32)
    owner = jnp.sum((pad_cum[None, :] <= j[:, None]).astype(jnp.int32), axis=1)
    e = jnp.minimum(owner, N_EXPERTS - 1)
    in_expert = pad_start[e] + counts[e] + (j - (pad_cum[e] - n_pad[e]))
    pad_slots = jnp.where(owner < N_EXPERTS, in_expert, pad_end[-1] + (j - pad_cum[-1])).astype(jnp.int32)
    xs = _row_scatter(hm, dest_km, pad_slots, n_slots)
    yb = _experts(block_expert, block_valid, xs, w_gu, b_gu.reshape(N_EXPERTS, 1, -1), w_down,
                  b_down.reshape(N_EXPERTS, 1, -1))
    yk = _row_gather(yb, dest_km).reshape(TOP_K, n, d)
    return _combine(yk, tg, x1, n_prompt)


def _pad_lanes(v, fill=0.0):
    v = v.reshape(1, -1).astype(F32)
    return jnp.pad(v, ((0, 0), (0, LANES - v.shape[1])), constant_values=fill)


def kernel(x_prompt, x_sample, cache_k, cache_v, state_ssm, state_conv, page_table, meta_tokens, rel_bias_table, norm1, w_in, conv_w, a_log, dt_bias, gdn_norm, q_norm, k_norm, lam_q1, lam_k1, lam_q2, lam_k2, sub_norm, w_br_a, w_br_b, w_out, norm2, w_router, b_router, w_gu, b_gu, w_down, b_down):
    bsz, seq, d = x_prompt.shape
    dbs, dseq, _ = x_sample.shape
    depth = w_in.shape[0]
    assert depth == 1 and dseq <= SUBLANES - (CONV_W - 1) and seq % ATT_BLK == 0
    lam_init = 0.8 - 0.6 * math.exp(-0.3 * 0)
    l = 0

    w = w_in[l]
    o_z = CONV_DIM + VA
    w_main = jnp.concatenate([w[:, :o_z], w[:, o_z + 2 * H_A:]], axis=1).astype(BF16)
    w_ba = jnp.pad(w[:, o_z:o_z + 2 * H_A], ((0, 0), (0, LANES - 2 * H_A))).astype(BF16)
    grp = np.arange(QB) // DK_B
    gmat = jnp.asarray((grp[:, None] == grp[None, :]).astype(np.float32) / DK_B, dtype=BF16)
    qn_t = jnp.tile(q_norm[l].astype(F32), QB // DK_B).reshape(1, QB)
    kn_t = jnp.tile(k_norm[l].astype(F32), QB // DK_B).reshape(1, QB)
    alog_p = jnp.pad(a_log[l].astype(F32), (H_A, LANES - 2 * H_A)).reshape(1, LANES)
    dtb_p = jnp.pad(dt_bias[l].astype(F32), (H_A, LANES - 2 * H_A)).reshape(1, LANES)
    n1 = norm1[l].reshape(1, d).astype(F32)
    proj = functools.partial(_inproj, n1=n1, w_main=w_main, w_ba=w_ba, gmat=gmat, qn_t=qn_t, kn_t=kn_t,
                             alog_p=alog_p, dtb_p=dtb_p)
    lam4 = jnp.stack([lam_q1[l], lam_k1[l], lam_q2[l], lam_k2[l]]).astype(F32)
    sn = sub_norm[l].reshape(1, DV_B).astype(F32)
    gn = gdn_norm[l].reshape(1, DV_A).astype(F32)
    cw = conv_w[l].astype(F32)

    ii = np.arange(ATT_BLK)[:, None]
    jj = np.arange(LANES)[None, :]
    bk_prompt = np.stack([_bucket_np(ii - jj), _bucket_np(ATT_BLK + ii - jj)])
    t8 = np.arange(SUBLANES)[:, None]
    bk_sample = np.stack([_bucket_np(PAGE + t8 - jj), _bucket_np(t8 - jj)])
    bt_prompt = _bias_tiles(rel_bias_table.astype(F32), bk_prompt)
    bt_sample = _bias_tiles(rel_bias_table.astype(F32), bk_sample)
    far_h = rel_bias_table[N_BUCKETS - 1].astype(F32)
    far_p = jnp.broadcast_to(far_h[:, None, None], (H_B, 1, ATT_KV))
    diag_t = jnp.where(jnp.asarray(ii >= jj), bt_prompt[:, 0], NEG)
    near_t = bt_prompt[:, 1]
    far_t = jnp.broadcast_to(far_h[:, None, None], near_t.shape)
    neg_t = jnp.full(near_t.shape, NEG, F32)
    tile2 = lambda a, b_, c_, d_: jnp.concatenate([jnp.concatenate([a, b_], axis=2),
                                                   jnp.concatenate([c_, d_], axis=2)], axis=1)
    bt_prompt = jnp.stack([tile2(diag_t, neg_t, near_t, diag_t), tile2(far_t, near_t, far_t, far_t)], axis=1)
    nrow = H_B * 2 * SUBLANES
    rows_of = lambda t: jnp.broadcast_to(t[:, None], (H_B, 2, SUBLANES, LANES)).reshape(nrow, LANES)
    last_t, self_t = rows_of(bt_sample[:, 0]), rows_of(bt_sample[:, 1])
    far_col = jnp.broadcast_to(far_h[:, None, None], (H_B, 2 * SUBLANES, LANES)).reshape(nrow, LANES)

    lreal = N_META + seq
    lp = -(-lreal // ATT_KV) * ATT_KV
    fp = lp - lreal
    assert fp % SUBLANES == 0 and fp >= GDN_CHUNK and (fp + N_META) % ATT_BLK == 0 and lp % GDN_CHUNK == 0
    xp = jnp.concatenate([jnp.zeros((bsz, fp, d), F32),
                          jnp.broadcast_to(meta_tokens.astype(F32)[None], (bsz, N_META, d)), x_prompt], axis=1)
    conv_p, z_p, bg_p, qd_p, kd_p, vd_p, sga_p, sgb_p, kdb_p, vdb_p = proj(xp.reshape(bsz * lp, d), ATT_KV)
    r3 = lambda a, b_, r_: a.reshape(b_, r_, a.shape[-1])
    oa_p, ssm_p = _gdn(r3(conv_p, bsz, lp), r3(bg_p, bsz, lp), r3(z_p, bsz, lp),
                       jnp.zeros((bsz, H_A, DK_A, DV_A), F32), cw, gn, fp, GDN_CHUNK, 1, H_A)
    kbias = jnp.where(jnp.arange(lp) < fp, NEG, 0.0).astype(F32).reshape(lp // ATT_KV, ATT_KV)
    ob_p = _attn_prompt(r3(qd_p, bsz, lp), r3(kdb_p, bsz, lp), r3(vdb_p, bsz, lp), bt_prompt, far_p, kbias,
                        lam4, sn, fp // ATT_Q, lam_init)

    ns = dbs * dseq
    conv_s, z_s, bg_s, qd_s, kd_s, vd_s, sga_s, sgb_s, _, _ = proj(x_sample.reshape(ns, d), min(ns, 256))
    cs = SUBLANES
    rs = 2 * cs
    fs = rs - dseq

    def chunk_rows(a, head=None):
        a = a.reshape(dbs, dseq, a.shape[-1]).astype(F32)
        parts = [jnp.zeros((dbs, fs - (0 if head is None else head.shape[1]), a.shape[-1]), F32)]
        if head is not None:
            parts.append(head.astype(F32))
        return jnp.concatenate(parts + [a], axis=1)

    oa_s, ssm_s = _gdn(chunk_rows(conv_s, state_conv[l]), chunk_rows(bg_s), chunk_rows(z_s),
                       state_ssm[l].astype(F32), cw, gn, fs, cs, math.gcd(dbs, 4), H_A)
    pad8 = lambda a: jnp.pad(a.reshape(dbs, dseq, a.shape[-1]), ((0, 0), (0, SUBLANES - dseq), (0, 0)))
    n_pool = cache_k.shape[1]
    ck_t = jnp.transpose(cache_k[l], (0, 2, 3, 4, 1)).reshape(n_pool, QB, PAGE)
    cv_i = cache_v[l].reshape(n_pool, PAGE * H_B, DV_B)
    ob_s = _attn_sample(page_table, pad8(qd_s).astype(F32), pad8(kd_s), pad8(vd_s), ck_t, cv_i,
                        last_t, self_t, far_col, lam4, sn, dseq, lam_init)

    wa, wb, wo = w_br_a[l].astype(BF16), w_br_b[l].astype(BF16), w_out[l].astype(BF16)
    n2 = norm2[l].reshape(1, d).astype(F32)
    wr = jnp.pad(w_router[l], ((0, 0), (0, LANES - N_EXPERTS))).astype(BF16)
    br = _pad_lanes(b_router[l])
    tiles_b = seq // POST_TILE
    tiles_lp = lp // POST_TILE
    skip = (fp + N_META) // POST_TILE
    npt = bsz * seq
    assert ns % POST_TILE == 0 and seq % POST_TILE == 0 and (fp + N_META) % POST_TILE == 0
    oa_s2 = oa_s[:, fs:].reshape(ns, VA)
    ob_s2 = ob_s[:, :dseq].reshape(ns, VB)
    x1, hm, ti, tg = _post(
        (x_prompt.reshape(npt, d), oa_p.reshape(bsz * lp, VA), ob_p.reshape(bsz * lp, VB), sga_p, sgb_p),
        (x_sample.reshape(ns, d), oa_s2, ob_s2, sga_s, sgb_s), (wa, wb, wo, n2, wr, br), npt, ns,
        lambda i: (i // tiles_b) * tiles_lp + skip + i % tiles_b)

    y_p, y_s = _moe(hm, ti, tg, x1, w_gu[l], b_gu[l], w_down[l], b_down[l], npt)
    y_prompt = y_p.reshape(bsz, seq, d)
    y_sample = y_s.reshape(dbs, dseq, d)

    k_prompt = r3(kd_p, bsz, lp)[:, fp:].reshape(1, bsz, lreal, H_B, 2, DK_B)
    v_prompt = r3(vd_p, bsz, lp)[:, fp:].reshape(1, bsz, lreal, H_B, DV_B)
    conv_prompt = r3(conv_p, bsz, lp)[:, lp - (CONV_W - 1):][None]
    xpad_s = jnp.concatenate([state_conv[l].astype(F32), conv_s.reshape(dbs, dseq, CONV_DIM)], axis=1)
    conv_sample = xpad_s[:, dseq:][None]
    return (y_prompt, y_sample, k_prompt, v_prompt, ssm_p[None], conv_prompt,
            kd_s.reshape(1, dbs, dseq, H_B, 2, DK_B), vd_s.reshape(1, dbs, dseq, H_B, DV_B), ssm_s[None], conv_sample)
```
